```python
import math
import jax, jax.numpy as jnp
from jax import lax
import numpy as np

D_MODEL = 1024
BATCH = 16
SEQ = 256
DEPTH = 4
DEC_BATCH = 8
DEC_SEQ = 1024
PAST_LEN = 512

GRID_W = 64
N_MIXERS = 4
Q_BLOCK = 128
RMS_EPS = 1e-6
ROPE_THETA = 10000.0
NEG_INF = -1e30
D_FF = 2816
FFN_RES = 0.5
N_MOD = 9
HEAD_DIM = 64
A_HEADS = 16
A_KV_HEADS = 4
A_WINDOW = 128
B_HEADS = 16
B_KV_HEADS = 4
C_HEADS = 16
C_Q_RANK = 384
C_KV_RANK = 256
C_NOPE = 64
C_ROPE = 32
C_V = 64
C_QK = C_NOPE + C_ROPE
D_HEADS = 16
NA_ROWS = 8
NA_COLS = 16
NA_QCOLS = 16
NA_KCOLS = 2 * NA_COLS
N_A = (DEPTH + 3) // 4
N_B = (DEPTH + 2) // 4
N_C = (DEPTH + 1) // 4
N_D = DEPTH // 4

kernel_name = 'hybrid_diffusion_interleaved_step'


def rms_norm(x, g):
    xf = x.astype(jnp.float32)
    y = xf * lax.rsqrt(jnp.mean(xf * xf, axis=-1, keepdims=True) + RMS_EPS)
    return (y * g.astype(jnp.float32)).astype(x.dtype)


def modulate(x, g, shift, scale):
    return rms_norm(x, g) * (1 + scale) + shift


def modulation(cond, w, b):
    return jnp.split(jax.nn.silu(cond) @ w + b, N_MOD, axis=-1)


def half_ffn(x, shift, scale, gate, g, w_gu, w_down):
    h = modulate(x, g, shift, scale)
    a, u = jnp.split(h @ w_gu, 2, axis=-1)
    return x + FFN_RES * gate * ((jax.nn.silu(a) * u) @ w_down)


def grid_angles(n, rot_dim):
    pos = jnp.arange(n, dtype=jnp.int32)
    row = (pos // GRID_W).astype(jnp.float32)
    col = (pos % GRID_W).astype(jnp.float32)
    n_ax = rot_dim // 4
    inv = ROPE_THETA ** (-jnp.arange(n_ax, dtype=jnp.float32) / n_ax)
    return jnp.concatenate([row[:, None] * inv, col[:, None] * inv], axis=-1)


def apply_rope(x, ang):
    shape = x.shape
    xr = x.astype(jnp.float32).reshape(shape[:-1] + (shape[-1] // 2, 2))
    ang = ang.reshape(ang.shape[:1] + (1,) * (x.ndim - 3) + ang.shape[1:])
    cos, sin = jnp.cos(ang), jnp.sin(ang)
    x0, x1 = xr[..., 0], xr[..., 1]
    out = jnp.stack([x0 * cos - x1 * sin, x0 * sin + x1 * cos], axis=-1)
    return out.reshape(shape).astype(x.dtype)


def block_attention(q, k, v, sink=None):
    b, n, hkv, g, dk = q.shape
    nb = n // Q_BLOCK
    scale = dk ** -0.5
    n_keys = k.shape[1]
    qb = jnp.moveaxis(q.reshape(b, nb, Q_BLOCK, hkv, g, dk), 1, 0)

    def one_block(qi):
        s = jnp.einsum('bqhgd,bkhd->bhgqk', qi, k).astype(jnp.float32) * scale
        if sink is not None:
            s_sink = jnp.broadcast_to(sink.astype(jnp.float32).reshape(1, hkv, g, 1, 1), s.shape[:-1] + (1,))
            s = jnp.concatenate([s, s_sink], axis=-1)
        p = jax.nn.softmax(s, axis=-1)[..., :n_keys].astype(v.dtype)
        return jnp.einsum('bhgqk,bkhd->bqhgd', p, v)

    out = lax.map(one_block, qb)
    return jnp.moveaxis(out, 0, 1).reshape(b, n, hkv, g, v.shape[-1])


def window_attention(q, k, v, kc, vc, sink):
    b, n, hkv, g, d = q.shape
    nb = n // Q_BLOCK
    scale = d ** -0.5
    pad = ((0, 0), (Q_BLOCK, Q_BLOCK), (0, 0), (0, 0))
    kp = jnp.pad(k, pad).reshape(b, nb + 2, Q_BLOCK, hkv, d)
    vp = jnp.pad(v, pad).reshape(b, nb + 2, Q_BLOCK, hkv, d)
    kw = jnp.moveaxis(jnp.concatenate([kp[:, :-2], kp[:, 1:-1], kp[:, 2:]], axis=2), 1, 0)
    vw = jnp.moveaxis(jnp.concatenate([vp[:, :-2], vp[:, 1:-1], vp[:, 2:]], axis=2), 1, 0)
    qb = jnp.moveaxis(q.reshape(b, nb, Q_BLOCK, hkv, g, d), 1, 0)
    blk = jnp.arange(nb)
    qpos = blk[:, None] * Q_BLOCK + jnp.arange(Q_BLOCK)[None, :]
    kpos = (blk[:, None] - 1) * Q_BLOCK + jnp.arange(3 * Q_BLOCK)[None, :]
    valid = ((kpos[:, None, :] >= 0) & (kpos[:, None, :] < n)
             & (jnp.abs(qpos[:, :, None] - kpos[:, None, :]) <= A_WINDOW))
    sink_f = sink.astype(jnp.float32).reshape(1, hkv, g, 1, 1)
    n_win = 3 * Q_BLOCK
    n_ctx = kc.shape[1]

    def one_block(args):
        qi, ki, vi, mi = args
        s_w = jnp.einsum('bqhgd,bkhd->bhgqk', qi, ki).astype(jnp.float32) * scale
        s_w = jnp.where(mi[None, None, None], s_w, NEG_INF)
        s_c = jnp.einsum('bqhgd,bkhd->bhgqk', qi, kc).astype(jnp.float32) * scale
        s_s = jnp.broadcast_to(sink_f, s_w.shape[:-1] + (1,))
        p = jax.nn.softmax(jnp.concatenate([s_w, s_c, s_s], axis=-1), axis=-1).astype(v.dtype)
        return (jnp.einsum('bhgqk,bkhd->bqhgd', p[..., :n_win], vi)
                + jnp.einsum('bhgqk,bkhd->bqhgd', p[..., n_win:n_win + n_ctx], vc))

    out = lax.map(one_block, (qb, kw, vw, valid))
    return jnp.moveaxis(out, 0, 1).reshape(b, n, hkv, g, d)


def neighbourhood_attention(q, k, v, kc, vc, rpb):
    b, n, h, d = q.shape
    rows = n // GRID_W
    wh = min(NA_ROWS, rows)
    ncb = GRID_W // NA_QCOLS
    nqb = rows * ncb
    nk = wh * NA_KCOLS
    blk = jnp.arange(nqb)
    r = blk // ncb
    c0 = (blk % ncb) * NA_QCOLS
    rs = jnp.clip(r - wh // 2, 0, rows - wh)
    cb = jnp.clip(c0 - NA_COLS // 2, 0, GRID_W - NA_KCOLS)
    koff = jnp.arange(nk)
    kr = rs[:, None] + (koff // NA_KCOLS)[None, :]
    kcol = cb[:, None] + (koff % NA_KCOLS)[None, :]
    kidx = kr * GRID_W + kcol
    qcol = c0[:, None] + jnp.arange(NA_QCOLS)[None, :]
    cs = jnp.clip(qcol - NA_COLS // 2, 0, GRID_W - NA_COLS)
    valid = (kcol[:, None, :] >= cs[:, :, None]) & (kcol[:, None, :] < cs[:, :, None] + NA_COLS)
    ri = jnp.broadcast_to(jnp.clip(kr - r[:, None] + NA_ROWS - 1, 0, 2 * NA_ROWS - 2)[:, None, :], valid.shape)
    ci = jnp.clip(kcol[:, None, :] - qcol[:, :, None] + NA_COLS - 1, 0, 2 * NA_COLS - 2)
    scale = d ** -0.5
    qb = jnp.moveaxis(q.reshape(b, nqb, NA_QCOLS, h, d), 1, 0)
    rpb_f = rpb.astype(jnp.float32)

    def one_block(args):
        qi, idx, m, ri_i, ci_i = args
        kb = jnp.take(k, idx, axis=1)
        vb = jnp.take(v, idx, axis=1)
        s_n = jnp.einsum('bqhd,bkhd->bhqk', qi, kb).astype(jnp.float32) * scale + rpb_f[:, ri_i, ci_i][None]
        s_n = jnp.where(m[None, None], s_n, NEG_INF)
        s_c = jnp.einsum('bqhd,bkhd->bhqk', qi, kc).astype(jnp.float32) * scale
        p = jax.nn.softmax(jnp.concatenate([s_n, s_c], axis=-1), axis=-1).astype(v.dtype)
        return (jnp.einsum('bhqk,bkhd->bqhd', p[..., :nk], vb)
                + jnp.einsum('bhqk,bkhd->bqhd', p[..., nk:], vc))

    out = lax.map(one_block, (qb, kidx, valid, ri, ci))
    return jnp.moveaxis(out, 0, 1).reshape(b, n, h, d)


def gqa_project(h, w_qkv, q_norm, k_norm, n_heads, n_kv):
    b, n, _ = h.shape
    q, k, v = jnp.split(h @ w_qkv, [n_heads * HEAD_DIM, (n_heads + n_kv) * HEAD_DIM], axis=-1)
    q = rms_norm(q.reshape(b, n, n_heads, HEAD_DIM), q_norm)
    k = rms_norm(k.reshape(b, n, n_kv, HEAD_DIM), k_norm)
    return q, k, v.reshape(b, n, n_kv, HEAD_DIM)


def group_q(q, n_kv):
    b, n, hq, d = q.shape
    return q.reshape(b, n, n_kv, hq // n_kv, d)


def merge_heads(o, w_o):
    b, n = o.shape[:2]
    return o.reshape(b, n, -1) @ w_o


def mla_compress(h, w_down, q_lnorm, kv_lnorm):
    cq, ckv, kr = jnp.split(h @ w_down, [C_Q_RANK, C_Q_RANK + C_KV_RANK], axis=-1)
    return rms_norm(cq, q_lnorm), rms_norm(ckv, kv_lnorm), kr


def rope_tail(x, ang):
    return jnp.concatenate([x[..., :C_NOPE], apply_rope(x[..., C_NOPE:], ang)], axis=-1)


def mla_queries(cq, w_uq, q_norm, ang):
    b, n, _ = cq.shape
    q = rms_norm((cq @ w_uq).reshape(b, n, C_HEADS, C_QK), q_norm)
    if ang is not None:
        q = rope_tail(q, ang)
    return q[:, :, :, None, :]


def mla_keys_values(ckv, kr, w_ukv, k_norm, ang):
    b, n, _ = ckv.shape
    k_nope, v = jnp.split((ckv @ w_ukv).reshape(b, n, C_HEADS, C_NOPE + C_V), [C_NOPE], axis=-1)
    k = jnp.concatenate([k_nope, jnp.broadcast_to(kr[:, :, None, :], (b, n, C_HEADS, C_ROPE))], axis=-1)
    k = rms_norm(k, k_norm)
    if ang is not None:
        k = rope_tail(k, ang)
    return k, v


def setup_inputs(seed: int = 0) -> dict:
    key = jax.random.key(seed)
    ks = iter(jax.random.split(key, 64))

    def nrm(shape, s=1.0):
        return s * jax.random.normal(next(ks), shape, jnp.float32)

    def lin(shape):
        return nrm(shape, shape[-2] ** -0.5)

    def gain(shape):
        return 1.0 + nrm(shape, 0.05)

    D = D_MODEL
    qkv_a = (A_HEADS + 2 * A_KV_HEADS) * HEAD_DIM
    qkv_b = (B_HEADS + 2 * B_KV_HEADS) * HEAD_DIM
    qkv_d = 3 * D_HEADS * HEAD_DIM
    return {
        'x_prompt': nrm((BATCH, SEQ, D)),
        'x_sample': nrm((DEC_BATCH, DEC_SEQ, D)),
        'cache_a_k': nrm((DEC_BATCH, N_A, PAST_LEN, A_KV_HEADS, HEAD_DIM)),
        'cache_a_v': nrm((DEC_BATCH, N_A, PAST_LEN, A_KV_HEADS, HEAD_DIM)),
        'cache_b_k': nrm((DEC_BATCH, N_B, PAST_LEN, B_KV_HEADS, HEAD_DIM)),
        'cache_b_v': nrm((DEC_BATCH, N_B, PAST_LEN, B_KV_HEADS, HEAD_DIM)),
        'cache_c_kv': nrm((DEC_BATCH, N_C, PAST_LEN, C_KV_RANK)),
        'cache_c_krope': nrm((DEC_BATCH, N_C, PAST_LEN, C_ROPE)),
        'cache_d_k': nrm((DEC_BATCH, N_D, PAST_LEN, D_HEADS, HEAD_DIM)),
        'cache_d_v': nrm((DEC_BATCH, N_D, PAST_LEN, D_HEADS, HEAD_DIM)),
        'c': nrm((DEC_BATCH, D)),
        'c_ctx': nrm((D,)),
        'mod_w': nrm((DEPTH, D, N_MOD * D), 0.5 * D ** -0.5),
        'mod_b': nrm((DEPTH, N_MOD * D), 0.02),
        'norm_ff1': gain((DEPTH, D)),
        'norm_mix': gain((DEPTH, D)),
        'norm_ff2': gain((DEPTH, D)),
        'ff1_w_gu': lin((DEPTH, D, 2 * D_FF)),
        'ff1_w_down': lin((DEPTH, D_FF, D)),
        'ff2_w_gu': lin((DEPTH, D, 2 * D_FF)),
        'ff2_w_down': lin((DEPTH, D_FF, D)),
        'a_w_qkv': lin((N_A, D, qkv_a)),
        'a_q_norm': gain((N_A, HEAD_DIM)),
        'a_k_norm': gain((N_A, HEAD_DIM)),
        'a_sink': nrm((N_A, A_HEADS)),
        'a_w_o': lin((N_A, A_HEADS * HEAD_DIM, D)),
        'b_w_qkv': lin((N_B, D, qkv_b)),
        'b_q_norm': gain((N_B, HEAD_DIM)),
        'b_k_norm': gain((N_B, HEAD_DIM)),
        'b_w_o': lin((N_B, B_HEADS * HEAD_DIM, D)),
        'c_w_down': lin((N_C, D, C_Q_RANK + C_KV_RANK + C_ROPE)),
        'c_q_lnorm': gain((N_C, C_Q_RANK)),
        'c_kv_lnorm': gain((N_C, C_KV_RANK)),
        'c_w_uq': lin((N_C, C_Q_RANK, C_HEADS * C_QK)),
        'c_w_ukv': lin((N_C, C_KV_RANK, C_HEADS * (C_NOPE + C_V))),
        'c_q_norm': gain((N_C, C_QK)),
        'c_k_norm': gain((N_C, C_QK)),
        'c_w_o': lin((N_C, C_HEADS * C_V, D)),
        'd_w_qkv': lin((N_D, D, qkv_d)),
        'd_q_norm': gain((N_D, HEAD_DIM)),
        'd_k_norm': gain((N_D, HEAD_DIM)),
        'd_rpb': nrm((N_D, D_HEADS, 2 * NA_ROWS - 1, 2 * NA_COLS - 1), 0.5),
        'd_w_o': lin((N_D, D_HEADS * HEAD_DIM, D)),
    }


def reference(x_prompt, x_sample, cache_a_k, cache_a_v, cache_b_k, cache_b_v, cache_c_kv, cache_c_krope,
              cache_d_k, cache_d_v, c, c_ctx, mod_w, mod_b, norm_ff1, norm_mix, norm_ff2,
              ff1_w_gu, ff1_w_down, ff2_w_gu, ff2_w_down,
              a_w_qkv, a_q_norm, a_k_norm, a_sink, a_w_o,
              b_w_qkv, b_q_norm, b_k_norm, b_w_o,
              c_w_down, c_q_lnorm, c_kv_lnorm, c_w_uq, c_w_ukv, c_q_norm, c_k_norm, c_w_o,
              d_w_qkv, d_q_norm, d_k_norm, d_rpb, d_w_o):
    xp, xs = x_prompt, x_sample
    n_lat = xs.shape[1]
    ang_hd = grid_angles(n_lat, HEAD_DIM)
    ang_c = grid_angles(n_lat, C_ROPE)
    new_a_k, new_a_v, new_b_k, new_b_v = [], [], [], []
    new_c_kv, new_c_krope, new_d_k, new_d_v = [], [], [], []
    for i in range(DEPTH):
        kind, j = i % N_MIXERS, i // N_MIXERS
        mp = modulation(c_ctx, mod_w[i], mod_b[i])
        ms = [m[:, None, :] for m in modulation(c, mod_w[i], mod_b[i])]
        xp = half_ffn(xp, mp[0], mp[1], mp[2], norm_ff1[i], ff1_w_gu[i], ff1_w_down[i])
        xs = half_ffn(xs, ms[0], ms[1], ms[2], norm_ff1[i], ff1_w_gu[i], ff1_w_down[i])
        hp = modulate(xp, norm_mix[i], mp[3], mp[4])
        hs = modulate(xs, norm_mix[i], ms[3], ms[4])
        if kind == 0:
            q, k, v = gqa_project(hp, a_w_qkv[j], a_q_norm[j], a_k_norm[j], A_HEADS, A_KV_HEADS)
            op = merge_heads(block_attention(group_q(q, A_KV_HEADS), k, v, a_sink[j]), a_w_o[j])
            new_a_k.append(k)
            new_a_v.append(v)
            q, k, v = gqa_project(hs, a_w_qkv[j], a_q_norm[j], a_k_norm[j], A_HEADS, A_KV_HEADS)
            q, k = apply_rope(q, ang_hd), apply_rope(k, ang_hd)
            o = window_attention(group_q(q, A_KV_HEADS), k, v, cache_a_k[:, j], cache_a_v[:, j], a_sink[j])
            os_ = merge_heads(o, a_w_o[j])
        elif kind == 1:
            q, k, v = gqa_project(hp, b_w_qkv[j], b_q_norm[j], b_k_norm[j], B_HEADS, B_KV_HEADS)
            op = merge_heads(block_attention(group_q(q, B_KV_HEADS), k, v), b_w_o[j])
            new_b_k.append(k)
            new_b_v.append(v)
            q, k, v = gqa_project(hs, b_w_qkv[j], b_q_norm[j], b_k_norm[j], B_HEADS, B_KV_HEADS)
            q, k = apply_rope(q, ang_hd), apply_rope(k, ang_hd)
            k_all = jnp.concatenate([k, cache_b_k[:, j]], axis=1)
            v_all = jnp.concatenate([v, cache_b_v[:, j]], axis=1)
            os_ = merge_heads(block_attention(group_q(q, B_KV_HEADS), k_all, v_all), b_w_o[j])
        elif kind == 2:
            cq, ckv, kr = mla_compress(hp, c_w_down[j], c_q_lnorm[j], c_kv_lnorm[j])
            q = mla_queries(cq, c_w_uq[j], c_q_norm[j], None)
            k, v = mla_keys_values(ckv, kr, c_w_ukv[j], c_k_norm[j], None)
            op = merge_heads(block_attention(q, k, v), c_w_o[j])
            new_c_kv.append(ckv)
            new_c_krope.append(kr)
            cq, ckv, kr = mla_compress(hs, c_w_down[j], c_q_lnorm[j], c_kv_lnorm[j])
            q = mla_queries(cq, c_w_uq[j], c_q_norm[j], ang_c)
            k_l, v_l = mla_keys_values(ckv, kr, c_w_ukv[j], c_k_norm[j], ang_c)
            k_c, v_c = mla_keys_values(cache_c_kv[:, j], cache_c_krope[:, j], c_w_ukv[j], c_k_norm[j], None)
            o = block_attention(q, jnp.concatenate([k_l, k_c], axis=1), jnp.concatenate([v_l, v_c], axis=1))
            os_ = merge_heads(o, c_w_o[j])
        else:
            q, k, v = gqa_project(hp, d_w_qkv[j], d_q_norm[j], d_k_norm[j], D_HEADS, D_HEADS)
            op = merge_heads(block_attention(group_q(q, D_HEADS), k, v), d_w_o[j])
            new_d_k.append(k)
            new_d_v.append(v)
            q, k, v = gqa_project(hs, d_w_qkv[j], d_q_norm[j], d_k_norm[j], D_HEADS, D_HEADS)
            o = neighbourhood_attention(q, k, v, cache_d_k[:, j], cache_d_v[:, j], d_rpb[j])
            os_ = merge_heads(o, d_w_o[j])
        xp = xp + mp[5] * op
        xs = xs + ms[5] * os_
        xp = half_ffn(xp, mp[6], mp[7], mp[8], norm_ff2[i], ff2_w_gu[i], ff2_w_down[i])
        xs = half_ffn(xs, ms[6], ms[7], ms[8], norm_ff2[i], ff2_w_gu[i], ff2_w_down[i])
    y_prompt, y_sample = xp, xs
    s_a_k = jnp.stack(new_a_k, axis=1)
    s_a_v = jnp.stack(new_a_v, axis=1)
    s_b_k = jnp.stack(new_b_k, axis=1)
    s_b_v = jnp.stack(new_b_v, axis=1)
    s_c_kv = jnp.stack(new_c_kv, axis=1)
    s_c_krope = jnp.stack(new_c_krope, axis=1)
    s_d_k = jnp.stack(new_d_k, axis=1)
    s_d_v = jnp.stack(new_d_v, axis=1)
    return (y_prompt, y_sample, s_a_k, s_a_v, s_b_k, s_b_v, s_c_kv, s_c_krope, s_d_k, s_d_v)
```

```python
import functools

import jax
import jax.numpy as jnp
from jax import lax
from jax.experimental import pallas as pl
from jax.experimental.pallas import tpu as pltpu

F32 = jnp.float32
BF16 = jnp.bfloat16

D_MODEL = 1024
BATCH = 16
SEQ = 256
DEPTH = 4
DEC_BATCH = 8
DEC_SEQ = 1024
PAST_LEN = 512
GRID_W = 64
N_MIXERS = 4
RMS_EPS = 1e-6
ROPE_THETA = 10000.0
NEG_INF = -1e30
D_FF = 2816
FFN_RES = 0.5
N_MOD = 9
HEAD_DIM = 64
N_HEADS = 16
GQA_KV_HEADS = 4
A_WINDOW = 128
C_Q_RANK = 384
C_KV_RANK = 256
C_NOPE = 64
C_ROPE = 32
C_V = 64
C_QK = C_NOPE + C_ROPE
NA_ROWS = 8
NA_COLS = 16

LANES = 128
MOD_ROWS = 16
C_DOWN_PAD = 768
C_HEAD_PAD = LANES
VMEM_LIMIT_BYTES = 56 * 1024 * 1024
TOKEN_TILE = 512
FF_CHUNK = 256
GRID_ROWS = DEC_SEQ // GRID_W
NA_WIN_KEYS = NA_ROWS * GRID_W


def _params(n_axes):
    return pltpu.CompilerParams(dimension_semantics=("arbitrary",) * n_axes,
                                vmem_limit_bytes=VMEM_LIMIT_BYTES)


def _resident(shape):
    nd = len(shape)
    return pl.BlockSpec(shape, lambda *_: (0,) * nd, pipeline_mode=pl.Buffered(1))


def _mod_spec(layer, part, row_fn):
    base = (layer * N_MOD + part) * MOD_ROWS
    return pl.BlockSpec((None, 1, D_MODEL), lambda i: (base + row_fn(i), 0, 0))


def _prompt_row(i):
    return 0


def _sample_row(tile):
    per_batch = DEC_SEQ // tile
    return lambda i: 1 + i // per_batch


def _silu(a):
    return a * (1.0 / (1.0 + jnp.exp(-a)))


def _modulate(x, g, shift, scale):
    y = x * lax.rsqrt(jnp.mean(x * x, axis=-1, keepdims=True) + RMS_EPS)
    return (y * g) * (1.0 + scale) + shift


def _swap_pairs(y, even):
    n = y.shape[-1]
    return jnp.where(even, pltpu.roll(y, n - 1, 1), pltpu.roll(y, 1, 1))


def _mod_kernel(c_ref, w_ref, b_ref, o_ref):
    s = _silu(c_ref[...]).astype(BF16)
    o_ref[...] = jnp.dot(s, w_ref[...].astype(BF16), preferred_element_type=F32) + b_ref[...]


def _mod_table(cond, mod_w, mod_b):
    out = pl.pallas_call(
        _mod_kernel,
        grid=(DEPTH, N_MOD),
        in_specs=[
            pl.BlockSpec((MOD_ROWS, D_MODEL), lambda l, j: (0, 0)),
            pl.BlockSpec((None, D_MODEL, D_MODEL), lambda l, j: (l, 0, j)),
            pl.BlockSpec((None, 1, D_MODEL), lambda l, j: (l, 0, j)),
        ],
        out_specs=pl.BlockSpec((None, None, MOD_ROWS, D_MODEL), lambda l, j: (l, j, 0, 0)),
        out_shape=jax.ShapeDtypeStruct((DEPTH, N_MOD, MOD_ROWS, D_MODEL), F32),
        compiler_params=_params(2),
    )(cond, mod_w, mod_b.reshape(DEPTH, 1, N_MOD * D_MODEL))
    return out.reshape(DEPTH * N_MOD * MOD_ROWS, 1, D_MODEL)


def _ffn_kernel(x_ref, sh_ref, sc_ref, gt_ref, g_ref, wgu_ref, wd_ref, o_ref, act_ref):
    x = x_ref[...]
    h = _modulate(x, g_ref[...], sh_ref[...], sc_ref[...]).astype(BF16)
    for c0 in range(0, D_FF, FF_CHUNK):
        a = jnp.dot(h, wgu_ref[:, c0:c0 + FF_CHUNK], preferred_element_type=F32)
        u = jnp.dot(h, wgu_ref[:, D_FF + c0:D_FF + c0 + FF_CHUNK], preferred_element_type=F32)
        act_ref[:, c0:c0 + FF_CHUNK] = (_silu(a) * u).astype(BF16)
    y = jnp.dot(act_ref[...], wd_ref[...], preferred_element_type=F32)
    o_ref[...] = x + (FFN_RES * gt_ref[...]) * y


def _half_ffn(x, modv, layer, part0, g, wgu, wd, row_fn):
    n_tok = x.shape[0]
    tile = pl.BlockSpec((TOKEN_TILE, D_MODEL), lambda i: (i, 0))
    return pl.pallas_call(
        _ffn_kernel,
        grid=(n_tok // TOKEN_TILE,),
        in_specs=[tile,
                  _mod_spec(layer, part0, row_fn), _mod_spec(layer, part0 + 1, row_fn),
                  _mod_spec(layer, part0 + 2, row_fn),
                  _resident((1, D_MODEL)), _resident((D_MODEL, 2 * D_FF)), _resident((D_FF, D_MODEL))],
        out_specs=tile,
        out_shape=jax.ShapeDtypeStruct((n_tok, D_MODEL), F32),
        scratch_shapes=[pltpu.VMEM((TOKEN_TILE, D_FF), BF16)],
        compiler_params=_params(1),
    )(x, modv, modv, modv, g.reshape(1, D_MODEL), wgu, wd)


def _oproj_kernel(x_ref, o_ref, gt_ref, w_ref, y_ref):
    y_ref[...] = x_ref[...] + gt_ref[...] * jnp.dot(o_ref[...], w_ref[...], preferred_element_type=F32)


def _out_proj(x, o, modv, layer, w_o, row_fn):
    n_tok = x.shape[0]
    tile = pl.BlockSpec((TOKEN_TILE, D_MODEL), lambda i: (i, 0))
    return pl.pallas_call(
        _oproj_kernel,
        grid=(n_tok // TOKEN_TILE,),
        in_specs=[tile, tile, _mod_spec(layer, 5, row_fn), _resident((D_MODEL, D_MODEL))],
        out_specs=tile,
        out_shape=jax.ShapeDtypeStruct((n_tok, D_MODEL), F32),
        compiler_params=_params(1),
    )(x, o, modv, w_o)


def _head_pair_norm(yp, gain2, lo):
    sq = yp * yp
    s_lo = jnp.sum(jnp.where(lo, sq, 0.0), axis=-1, keepdims=True)
    s_hi = jnp.sum(jnp.where(lo, 0.0, sq), axis=-1, keepdims=True)
    ms = jnp.where(lo, s_lo, s_hi) * (1.0 / HEAD_DIM)
    return (yp * lax.rsqrt(ms + RMS_EPS)) * gain2


def _gqa_proj_kernel(*refs, n_q, n_kv, rope):
    x_ref, sh_ref, sc_ref, g_ref, w_ref, qn_ref, kn_ref = refs[:7]
    if rope:
        cos_ref, sin_ref = refs[7:9]
        q_ref, k_ref, v_ref = refs[9:]
    else:
        q_ref, k_ref, v_ref = refs[7:]
    h = _modulate(x_ref[...], g_ref[...], sh_ref[...], sc_ref[...]).astype(BF16)
    lane = lax.broadcasted_iota(jnp.int32, (1, LANES), 1)
    lo = lane < HEAD_DIM
    even = (lane & 1) == 0
    q_cols = n_q * HEAD_DIM
    k_cols = n_kv * HEAD_DIM

    def normed(c0, gain_ref, out_ref, o0):
        y = jnp.dot(h, w_ref[:, c0:c0 + 2 * LANES], preferred_element_type=F32)
        for p in range(2):
            yn = _head_pair_norm(y[:, p * LANES:(p + 1) * LANES], gain_ref[...], lo)
            if rope:
                yn = yn * cos_ref[...] + _swap_pairs(yn, even) * sin_ref[...]
            out_ref[:, o0 + p * LANES:o0 + (p + 1) * LANES] = yn.astype(out_ref.dtype)

    for c0 in range(0, q_cols, 2 * LANES):
        normed(c0, qn_ref, q_ref, c0)
    for c0 in range(0, k_cols, 2 * LANES):
        normed(q_cols + c0, kn_ref, k_ref, c0)
    for c0 in range(0, k_cols, 2 * LANES):
        v = jnp.dot(h, w_ref[:, q_cols + k_cols + c0:q_cols + k_cols + c0 + 2 * LANES],
                    preferred_element_type=F32)
        v_ref[:, c0:c0 + 2 * LANES] = v.astype(v_ref.dtype)


def _gqa_project(x, modv, layer, g_mix, w_qkv, q_norm, k_norm, n_kv, row_fn, rope_tabs, kv_dtype):
    n_tok = x.shape[0]
    q_cols, k_cols = N_HEADS * HEAD_DIM, n_kv * HEAD_DIM
    rope = rope_tabs is not None
    tile = lambda w: pl.BlockSpec((TOKEN_TILE, w), lambda i: (i, 0))
    in_specs = [tile(D_MODEL), _mod_spec(layer, 3, row_fn), _mod_spec(layer, 4, row_fn),
                _resident((1, D_MODEL)), _resident((D_MODEL, q_cols + 2 * k_cols)),
                _resident((1, LANES)), _resident((1, LANES))]
    args = [x, modv, modv, g_mix.reshape(1, D_MODEL), w_qkv,
            jnp.tile(q_norm, 2).reshape(1, LANES), jnp.tile(k_norm, 2).reshape(1, LANES)]
    if rope:
        per_batch = DEC_SEQ // TOKEN_TILE
        tab = pl.BlockSpec((TOKEN_TILE, LANES), lambda i: (i % per_batch, 0))
        in_specs += [tab, tab]
        args += list(rope_tabs)
    return pl.pallas_call(
        functools.partial(_gqa_proj_kernel, n_q=N_HEADS, n_kv=n_kv, rope=rope),
        grid=(n_tok // TOKEN_TILE,),
        in_specs=in_specs,
        out_specs=[tile(q_cols), tile(k_cols), tile(k_cols)],
        out_shape=[jax.ShapeDtypeStruct((n_tok, q_cols), BF16),
                   jax.ShapeDtypeStruct((n_tok, k_cols), kv_dtype),
                   jax.ShapeDtypeStruct((n_tok, k_cols), kv_dtype)],
        compiler_params=_params(1),
    )(*args)


def _mla_group_norm(y, gain):
    ms = jnp.sum(y * y, axis=-1, keepdims=True) * (1.0 / C_QK)
    return (y * lax.rsqrt(ms + RMS_EPS)) * gain


def _mla_expand_kv(ckv, kr128, wk_ref, wv_ref, kn_ref, rope_refs, k_ref, v_ref):
    lane = lax.broadcasted_iota(jnp.int32, (1, LANES), 1)
    even = (lane & 1) == 0
    for c0 in range(0, N_HEADS * C_HEAD_PAD, 2 * LANES):
        y = jnp.dot(ckv, wk_ref[:, c0:c0 + 2 * LANES], preferred_element_type=F32)
        for p in range(2):
            kn = _mla_group_norm(y[:, p * LANES:(p + 1) * LANES] + kr128, kn_ref[...])
            if rope_refs is not None:
                kn = kn * rope_refs[0][...] + _swap_pairs(kn, even) * rope_refs[1][...]
            k_ref[:, c0 + p * LANES:c0 + (p + 1) * LANES] = kn.astype(k_ref.dtype)
    for c0 in range(0, N_HEADS * C_V, 2 * LANES):
        v = jnp.dot(ckv, wv_ref[:, c0:c0 + 2 * LANES], preferred_element_type=F32)
        v_ref[:, c0:c0 + 2 * LANES] = v.astype(v_ref.dtype)


def _mla_proj_kernel(*refs, rope):
    (x_ref, sh_ref, sc_ref, g_ref, wd_ref, qln_ref, kvln_ref, wuq_ref, wk_ref, wv_ref,
     qn_ref, kn_ref) = refs[:12]
    if rope:
        rope_refs = refs[12:14]
        q_ref, k_ref, v_ref, ckv_ref, kr_ref = refs[14:]
    else:
        rope_refs = None
        q_ref, k_ref, v_ref, ckv_ref, kr_ref = refs[12:]
    h = _modulate(x_ref[...], g_ref[...], sh_ref[...], sc_ref[...]).astype(BF16)
    y = jnp.dot(h, wd_ref[...], preferred_element_type=F32)

    def row_norm(z, gain):
        return (z * lax.rsqrt(jnp.mean(z * z, axis=-1, keepdims=True) + RMS_EPS)) * gain

    cq = row_norm(y[:, :C_Q_RANK], qln_ref[...]).astype(BF16)
    ckv = row_norm(y[:, C_Q_RANK:C_Q_RANK + C_KV_RANK], kvln_ref[...])
    kr128 = pltpu.roll(y[:, C_Q_RANK + C_KV_RANK:], C_NOPE, 1)
    ckv_ref[...] = ckv
    kr_ref[...] = kr128

    lane = lax.broadcasted_iota(jnp.int32, (1, LANES), 1)
    even = (lane & 1) == 0
    for c0 in range(0, N_HEADS * C_HEAD_PAD, 2 * LANES):
        yq = jnp.dot(cq, wuq_ref[:, c0:c0 + 2 * LANES], preferred_element_type=F32)
        for p in range(2):
            qn = _mla_group_norm(yq[:, p * LANES:(p + 1) * LANES], qn_ref[...])
            if rope:
                qn = qn * rope_refs[0][...] + _swap_pairs(qn, even) * rope_refs[1][...]
            q_ref[:, c0 + p * LANES:c0 + (p + 1) * LANES] = qn.astype(q_ref.dtype)
    _mla_expand_kv(ckv.astype(BF16), kr128, wk_ref, wv_ref, kn_ref, rope_refs, k_ref, v_ref)


def _mla_project(x, modv, layer, g_mix, w, row_fn, rope_tabs):
    n_tok = x.shape[0]
    rope = rope_tabs is not None
    tile = lambda wd: pl.BlockSpec((TOKEN_TILE, wd), lambda i: (i, 0))
    qk_cols = N_HEADS * C_HEAD_PAD
    in_specs = [tile(D_MODEL), _mod_spec(layer, 3, row_fn), _mod_spec(layer, 4, row_fn),
                _resident((1, D_MODEL)), _resident((D_MODEL, C_DOWN_PAD)),
                _resident((1, C_Q_RANK)), _resident((1, C_KV_RANK)),
                _resident((C_Q_RANK, qk_cols)), _resident((C_KV_RANK, qk_cols)),
                _resident((C_KV_RANK, N_HEADS * C_V)),
                _resident((1, LANES)), _resident((1, LANES))]
    args = [x, modv, modv, g_mix.reshape(1, D_MODEL), w["down"], w["q_lnorm"], w["kv_lnorm"],
            w["uq"], w["uk"], w["uv"], w["q_norm"], w["k_norm"]]
    if rope:
        per_batch = DEC_SEQ // TOKEN_TILE
        tab = pl.BlockSpec((TOKEN_TILE, LANES), lambda i: (i % per_batch, 0))
        in_specs += [tab, tab]
        args += list(rope_tabs)
    return pl.pallas_call(
        functools.partial(_mla_proj_kernel, rope=rope),
        grid=(n_tok // TOKEN_TILE,),
        in_specs=in_specs,
        out_specs=[tile(qk_cols), tile(qk_cols), tile(N_HEADS * C_V), tile(C_KV_RANK), tile(LANES)],
        out_shape=[jax.ShapeDtypeStruct((n_tok, qk_cols), BF16),
                   jax.ShapeDtypeStruct((n_tok, qk_cols), BF16),
                   jax.ShapeDtypeStruct((n_tok, N_HEADS * C_V), BF16),
                   jax.ShapeDtypeStruct((n_tok, C_KV_RANK), F32),
                   jax.ShapeDtypeStruct((n_tok, LANES), F32)],
        compiler_params=_params(1),
    )(*args)


def _mla_cache_kernel(ckv_ref, kr_ref, wk_ref, wv_ref, kn_ref, k_ref, v_ref):
    _mla_expand_kv(ckv_ref[...].astype(BF16), kr_ref[...], wk_ref, wv_ref, kn_ref, None, k_ref, v_ref)


def _mla_expand_cache(ckv, kr128, w):
    n_tok = ckv.shape[0]
    tile = lambda wd: pl.BlockSpec((TOKEN_TILE, wd), lambda i: (i, 0))
    qk_cols = N_HEADS * C_HEAD_PAD
    return pl.pallas_call(
        _mla_cache_kernel,
        grid=(n_tok // TOKEN_TILE,),
        in_specs=[tile(C_KV_RANK), tile(LANES), _resident((C_KV_RANK, qk_cols)),
                  _resident((C_KV_RANK, N_HEADS * C_V)), _resident((1, LANES))],
        out_specs=[tile(qk_cols), tile(N_HEADS * C_V)],
        out_shape=[jax.ShapeDtypeStruct((n_tok, qk_cols), BF16),
                   jax.ShapeDtypeStruct((n_tok, N_HEADS * C_V), BF16)],
        compiler_params=_params(1),
    )(ckv, kr128, w["uk"], w["uv"], w["k_norm"])


def _softmax_pv(q, segs, scale, sink):
    dims = (((1,), (1,)), ((), ()))
    scores = []
    for k, _, mask in segs:
        s = lax.dot_general(q, k, dims, preferred_element_type=F32) * scale
        if mask is not None:
            s = jnp.where(mask, s, NEG_INF)
        scores.append(s)
    m = functools.reduce(jnp.maximum, [jnp.max(s, axis=-1, keepdims=True) for s in scores])
    if sink is not None:
        m = jnp.maximum(m, sink)
    denom = jnp.exp(sink - m) if sink is not None else 0.0
    out = 0.0
    for s, (_, v, _) in zip(scores, segs):
        p = jnp.exp(s - m)
        denom = denom + jnp.sum(p, axis=-1, keepdims=True)
        out = out + jnp.dot(p.astype(BF16), v, preferred_element_type=F32)
    return out * (1.0 / denom)


def _attn_kernel(*refs, n_kv, dk, scale, seg_kinds, has_sink, tq):
    n_seg = len(seg_kinds)
    q_ref = refs[0]
    kv_refs = refs[1:1 + 2 * n_seg]
    sink_ref = refs[1 + 2 * n_seg] if has_sink else None
    o_ref = refs[-1]
    group = N_HEADS // n_kv
    rows = group * tq
    blk = pl.program_id(1)
    n_blk = pl.num_programs(1)

    masks = []
    for kind in seg_kinds:
        if kind == "full":
            masks.append(None)
            continue
        n_keys = kv_refs[2 * len(masks)].shape[0]
        r = lax.broadcasted_iota(jnp.int32, (rows, n_keys), 0) & (tq - 1)
        c = lax.broadcasted_iota(jnp.int32, (rows, n_keys), 1)
        if kind == "prev":
            masks.append((c >= r) & (blk > 0))
        else:
            masks.append((c <= r) & (blk < n_blk - 1))

    for hk in range(n_kv):
        heads = range(hk * group, (hk + 1) * group)
        parts = [q_ref[:, h * dk:(h + 1) * dk] for h in heads]
        q = parts[0] if group == 1 else jnp.concatenate(parts, axis=0)
        segs = []
        for s_i in range(n_seg):
            k = kv_refs[2 * s_i][:, hk * dk:(hk + 1) * dk].astype(BF16)
            v = kv_refs[2 * s_i + 1][:, hk * HEAD_DIM:(hk + 1) * HEAD_DIM].astype(BF16)
            segs.append((k, v, masks[s_i]))
        sink = None
        if has_sink:
            cols = [jnp.broadcast_to(sink_ref[h:h + 1, 0:1], (tq, 1)) for h in heads]
            sink = cols[0] if group == 1 else jnp.concatenate(cols, axis=0)
        out = _softmax_pv(q, segs, scale, sink)
        for g_i, h in enumerate(heads):
            o_ref[:, h * HEAD_DIM:(h + 1) * HEAD_DIM] = out[g_i * tq:(g_i + 1) * tq].astype(o_ref.dtype)


def _attention(q, segs, *, n_batch, n_tok, tq, n_kv, dk, scale, sink=None):
    n_blk = n_tok // tq
    in_specs = [pl.BlockSpec((tq, N_HEADS * dk), lambda b, i: (b * n_blk + i, 0))]
    args = [q]
    kinds = []
    for k, v, rows, kind in segs:
        if kind == "full":
            idx = lambda b, i: (b, 0)
        elif kind == "cur":
            idx = lambda b, i: (b * n_blk + i, 0)
        elif kind == "prev":
            idx = lambda b, i: (b * n_blk + jnp.maximum(i - 1, 0), 0)
        else:
            idx = lambda b, i: (b * n_blk + jnp.minimum(i + 1, n_blk - 1), 0)
        in_specs += [pl.BlockSpec((rows, n_kv * dk), idx), pl.BlockSpec((rows, n_kv * HEAD_DIM), idx)]
        args += [k, v]
        kinds.append("full" if kind == "cur" else kind)
    if sink is not None:
        in_specs.append(pl.BlockSpec((N_HEADS, LANES), lambda b, i: (0, 0)))
        args.append(jnp.broadcast_to(sink.reshape(N_HEADS, 1), (N_HEADS, LANES)))
    return pl.pallas_call(
        functools.partial(_attn_kernel, n_kv=n_kv, dk=dk, scale=scale, seg_kinds=tuple(kinds),
                          has_sink=sink is not None, tq=tq),
        grid=(n_batch, n_blk),
        in_specs=in_specs,
        out_specs=pl.BlockSpec((tq, N_HEADS * HEAD_DIM), lambda b, i: (b * n_blk + i, 0)),
        out_shape=jax.ShapeDtypeStruct((n_batch * n_tok, N_HEADS * HEAD_DIM), BF16),
        compiler_params=_params(2),
    )(*args)


def _na_row_window(r):
    rs = min(max(r - NA_ROWS // 2, 0), GRID_ROWS - NA_ROWS)
    return rs, r - rs


def _na_kernel(q_ref, k_ref, v_ref, kc_ref, vc_ref, bias_ref, o_ref):
    scale = HEAD_DIM ** -0.5
    dims = (((1,), (1,)), ((), ()))
    for hh in range(LANES // HEAD_DIM):
        cols = slice(hh * HEAD_DIM, (hh + 1) * HEAD_DIM)
        q = q_ref[:, cols]
        k = k_ref[:, cols]
        v = v_ref[:, cols]
        kc = kc_ref[:, cols].astype(BF16)
        vc = vc_ref[:, cols].astype(BF16)
        s_ctx = lax.dot_general(q, kc, dims, preferred_element_type=F32) * scale
        for r in range(GRID_ROWS):
            rs, slot = _na_row_window(r)
            rows = slice(r * GRID_W, (r + 1) * GRID_W)
            win = slice(rs * GRID_W, rs * GRID_W + NA_WIN_KEYS)
            s_n = lax.dot_general(q[rows], k[win], dims, preferred_element_type=F32) * scale
            s_n = s_n + bias_ref[hh, slot]
            s_c = s_ctx[rows]
            m = jnp.maximum(jnp.max(s_n, axis=-1, keepdims=True), jnp.max(s_c, axis=-1, keepdims=True))
            p_n = jnp.exp(s_n - m)
            p_c = jnp.exp(s_c - m)
            denom = jnp.sum(p_n, axis=-1, keepdims=True) + jnp.sum(p_c, axis=-1, keepdims=True)
            out = (jnp.dot(p_n.astype(BF16), v[win], preferred_element_type=F32)
                   + jnp.dot(p_c.astype(BF16), vc, preferred_element_type=F32))
            o_ref[rows, cols] = (out * (1.0 / denom)).astype(o_ref.dtype)


def _na_bias_table(rpb):
    slot = jnp.arange(NA_ROWS)[:, None, None]
    qcol = jnp.arange(GRID_W)[None, :, None]
    key = jnp.arange(NA_WIN_KEYS)[None, None, :]
    krow, kcol = key // GRID_W, key % GRID_W
    cs = jnp.clip(qcol - NA_COLS // 2, 0, GRID_W - NA_COLS)
    valid = (kcol >= cs) & (kcol < cs + NA_COLS)
    ri = jnp.clip(krow - slot + NA_ROWS - 1, 0, 2 * NA_ROWS - 2)
    ci = jnp.clip(kcol - qcol + NA_COLS - 1, 0, 2 * NA_COLS - 2)
    shape = (NA_ROWS, GRID_W, NA_WIN_KEYS)
    ri, ci, valid = (jnp.broadcast_to(a, shape) for a in (ri, ci, valid))
    return jnp.where(valid[None], rpb.astype(F32)[:, ri, ci], NEG_INF)


def _na_attention(q, k, v, kc, vc, bias):
    n_pairs = N_HEADS * HEAD_DIM // LANES
    lat = pl.BlockSpec((DEC_SEQ, LANES), lambda hp, b: (b, hp))
    ctx = pl.BlockSpec((PAST_LEN, LANES), lambda hp, b: (b, hp))
    return pl.pallas_call(
        _na_kernel,
        grid=(n_pairs, DEC_BATCH),
        in_specs=[lat, lat, lat, ctx, ctx,
                  pl.BlockSpec((LANES // HEAD_DIM, NA_ROWS, GRID_W, NA_WIN_KEYS),
                               lambda hp, b: (hp, 0, 0, 0))],
        out_specs=lat,
        out_shape=jax.ShapeDtypeStruct((DEC_BATCH * DEC_SEQ, N_HEADS * HEAD_DIM), BF16),
        compiler_params=_params(2),
    )(q, k, v, kc, vc, bias)


def _grid_angles(n, rot_dim):
    pos = jnp.arange(n, dtype=jnp.int32)
    row = (pos // GRID_W).astype(F32)
    col = (pos % GRID_W).astype(F32)
    n_ax = rot_dim // 4
    inv = ROPE_THETA ** (-jnp.arange(n_ax, dtype=F32) / n_ax)
    return jnp.concatenate([row[:, None] * inv, col[:, None] * inv], axis=-1)


def _pair_tables(ang):
    cos = jnp.repeat(jnp.cos(ang), 2, axis=-1)
    sin = jnp.stack([-jnp.sin(ang), jnp.sin(ang)], axis=-1).reshape(ang.shape[0], -1)
    return cos, sin


def _gqa_rope_tables():
    cos, sin = _pair_tables(_grid_angles(DEC_SEQ, HEAD_DIM))
    return jnp.tile(cos, (1, LANES // HEAD_DIM)), jnp.tile(sin, (1, LANES // HEAD_DIM))


def _mla_rope_tables():
    cos, sin = _pair_tables(_grid_angles(DEC_SEQ, C_ROPE))
    pad = LANES - C_QK
    cos = jnp.concatenate([jnp.ones((DEC_SEQ, C_NOPE), F32), cos, jnp.ones((DEC_SEQ, pad), F32)], axis=-1)
    sin = jnp.concatenate([jnp.zeros((DEC_SEQ, C_NOPE), F32), sin, jnp.zeros((DEC_SEQ, pad), F32)], axis=-1)
    return cos, sin


def _mla_weights(w_down, q_lnorm, kv_lnorm, w_uq, w_ukv, q_norm, k_norm):
    pad_head = lambda a: jnp.pad(a, [(0, 0)] * (a.ndim - 1) + [(0, C_HEAD_PAD - a.shape[-1])])
    ukv = w_ukv.reshape(C_KV_RANK, N_HEADS, C_NOPE + C_V)
    return {
        "down": jnp.pad(w_down, ((0, 0), (0, C_DOWN_PAD - w_down.shape[1]))).astype(BF16),
        "q_lnorm": q_lnorm.reshape(1, C_Q_RANK),
        "kv_lnorm": kv_lnorm.reshape(1, C_KV_RANK),
        "uq": pad_head(w_uq.reshape(C_Q_RANK, N_HEADS, C_QK)).reshape(C_Q_RANK, -1).astype(BF16),
        "uk": pad_head(ukv[:, :, :C_NOPE]).reshape(C_KV_RANK, -1).astype(BF16),
        "uv": ukv[:, :, C_NOPE:].reshape(C_KV_RANK, -1).astype(BF16),
        "q_norm": pad_head(q_norm).reshape(1, C_HEAD_PAD),
        "k_norm": pad_head(k_norm).reshape(1, C_HEAD_PAD),
    }


def kernel(x_prompt, x_sample, cache_a_k, cache_a_v, cache_b_k, cache_b_v, cache_c_kv, cache_c_krope, cache_d_k, cache_d_v, c, c_ctx, mod_w, mod_b, norm_ff1, norm_mix, norm_ff2, ff1_w_gu, ff1_w_down, ff2_w_gu, ff2_w_down, a_w_qkv, a_q_norm, a_k_norm, a_sink, a_w_o, b_w_qkv, b_q_norm, b_k_norm, b_w_o, c_w_down, c_q_lnorm, c_kv_lnorm, c_w_uq, c_w_ukv, c_q_norm, c_k_norm, c_w_o, d_w_qkv, d_q_norm, d_k_norm, d_rpb, d_w_o):
    n_p, n_s = BATCH * SEQ, DEC_BATCH * DEC_SEQ
    xp = x_prompt.reshape(n_p, D_MODEL)
    xs = x_sample.reshape(n_s, D_MODEL)
    cond = jnp.concatenate([c_ctx[None], c, jnp.zeros((MOD_ROWS - 1 - DEC_BATCH, D_MODEL), F32)], axis=0)
    modv = _mod_table(cond, mod_w, mod_b)
    s_row = _sample_row(TOKEN_TILE)
    gqa_tabs = _gqa_rope_tables()
    gqa_scale = HEAD_DIM ** -0.5
    flat_cache = lambda a: a.reshape(DEC_BATCH * PAST_LEN, -1)
    new = {}

    for i in range(DEPTH):
        kind, j = i % N_MIXERS, i // N_MIXERS
        wgu1, wd1 = ff1_w_gu[i].astype(BF16), ff1_w_down[i].astype(BF16)
        xp = _half_ffn(xp, modv, i, 0, norm_ff1[i], wgu1, wd1, _prompt_row)
        xs = _half_ffn(xs, modv, i, 0, norm_ff1[i], wgu1, wd1, s_row)

        if kind in (0, 1, 3):
            w_qkv, q_norm, k_norm, w_o, n_kv = {
                0: (a_w_qkv, a_q_norm, a_k_norm, a_w_o, GQA_KV_HEADS),
                1: (b_w_qkv, b_q_norm, b_k_norm, b_w_o, GQA_KV_HEADS),
                3: (d_w_qkv, d_q_norm, d_k_norm, d_w_o, N_HEADS)}[kind]
            w_qkv = w_qkv[j].astype(BF16)
            qp, kp, vp = _gqa_project(xp, modv, i, norm_mix[i], w_qkv, q_norm[j], k_norm[j], n_kv,
                                      _prompt_row, None, F32)
            qs, ks, vs = _gqa_project(xs, modv, i, norm_mix[i], w_qkv, q_norm[j], k_norm[j], n_kv,
                                      s_row, gqa_tabs if kind != 3 else None, BF16)
            op = _attention(qp, [(kp, vp, SEQ, "full")], n_batch=BATCH, n_tok=SEQ, tq=SEQ, n_kv=n_kv,
                            dk=HEAD_DIM, scale=gqa_scale, sink=a_sink[j] if kind == 0 else None)
            kv_shape = (BATCH, 1, SEQ, n_kv, HEAD_DIM)
            if kind == 0:
                new["a_k"], new["a_v"] = kp.reshape(kv_shape), vp.reshape(kv_shape)
                kc, vc = flat_cache(cache_a_k[:, j]), flat_cache(cache_a_v[:, j])
                os_ = _attention(qs, [(ks, vs, A_WINDOW, "prev"), (ks, vs, A_WINDOW, "cur"),
                                      (ks, vs, A_WINDOW, "next"), (kc, vc, PAST_LEN, "full")],
                                 n_batch=DEC_BATCH, n_tok=DEC_SEQ, tq=A_WINDOW, n_kv=n_kv, dk=HEAD_DIM,
                                 scale=gqa_scale, sink=a_sink[j])
            elif kind == 1:
                new["b_k"], new["b_v"] = kp.reshape(kv_shape), vp.reshape(kv_shape)
                kc, vc = flat_cache(cache_b_k[:, j]), flat_cache(cache_b_v[:, j])
                os_ = _attention(qs, [(ks, vs, DEC_SEQ, "full"), (kc, vc, PAST_LEN, "full")],
                                 n_batch=DEC_BATCH, n_tok=DEC_SEQ, tq=256, n_kv=n_kv, dk=HEAD_DIM,
                                 scale=gqa_scale)
            else:
                new["d_k"], new["d_v"] = kp.reshape(kv_shape), vp.reshape(kv_shape)
                kc, vc = flat_cache(cache_d_k[:, j]), flat_cache(cache_d_v[:, j])
                os_ = _na_attention(qs, ks, vs, kc, vc, _na_bias_table(d_rpb[j]))
        else:
            w = _mla_weights(c_w_down[j], c_q_lnorm[j], c_kv_lnorm[j], c_w_uq[j], c_w_ukv[j],
                             c_q_norm[j], c_k_norm[j])
            w_o = c_w_o
            qp, kp, vp, ckv_p, kr_p = _mla_project(xp, modv, i, norm_mix[i], w, _prompt_row, None)
            qs, ks, vs, _, _ = _mla_project(xs, modv, i, norm_mix[i], w, s_row, _mla_rope_tables())
            new["c_kv"] = ckv_p.reshape(BATCH, 1, SEQ, C_KV_RANK)
            new["c_krope"] = kr_p[:, C_NOPE:C_QK].reshape(BATCH, 1, SEQ, C_ROPE)
            kr_cache = jnp.pad(flat_cache(cache_c_krope[:, j]), ((0, 0), (C_NOPE, LANES - C_QK)))
            kc, vc = _mla_expand_cache(flat_cache(cache_c_kv[:, j]), kr_cache, w)
            mla_scale = C_QK ** -0.5
            op = _attention(qp, [(kp, vp, SEQ, "full")], n_batch=BATCH, n_tok=SEQ, tq=SEQ,
                            n_kv=N_HEADS, dk=C_HEAD_PAD, scale=mla_scale)
            os_ = _attention(qs, [(ks, vs, DEC_SEQ, "full"), (kc, vc, PAST_LEN, "full")],
                             n_batch=DEC_BATCH, n_tok=DEC_SEQ, tq=256, n_kv=N_HEADS, dk=C_HEAD_PAD,
                             scale=mla_scale)

        w_o = w_o[j].astype(BF16)
        xp = _out_proj(xp, op, modv, i, w_o, _prompt_row)
        xs = _out_proj(xs, os_, modv, i, w_o, s_row)
        wgu2, wd2 = ff2_w_gu[i].astype(BF16), ff2_w_down[i].astype(BF16)
        xp = _half_ffn(xp, modv, i, 6, norm_ff2[i], wgu2, wd2, _prompt_row)
        xs = _half_ffn(xs, modv, i, 6, norm_ff2[i], wgu2, wd2, s_row)

    return (xp.reshape(BATCH, SEQ, D_MODEL), xs.reshape(DEC_BATCH, DEC_SEQ, D_MODEL),
            new["a_k"], new["a_v"], new["b_k"], new["b_v"], new["c_kv"], new["c_krope"],
            new["d_k"], new["d_v"])
```

```python
import functools

import jax
import jax.numpy as jnp
from jax import lax
from jax.experimental import pallas as pl
from jax.experimental.pallas import tpu as pltpu

F32 = jnp.float32
BF16 = jnp.bfloat16

D_MODEL = 1024
BATCH = 16
SEQ = 256
DEPTH = 4
DEC_BATCH = 8
DEC_SEQ = 1024
PAST_LEN = 512
GRID_W = 64
N_MIXERS = 4
RMS_EPS = 1e-6
ROPE_THETA = 10000.0
NEG_INF = -1e30
D_FF = 2816
FFN_RES = 0.5
N_MOD = 9
HEAD_DIM = 64
N_HEADS = 16
GQA_KV_HEADS = 4
A_WINDOW = 128
C_Q_RANK = 384
C_KV_RANK = 256
C_NOPE = 64
C_ROPE = 32
C_V = 64
C_QK = C_NOPE + C_ROPE
NA_ROWS = 8
NA_COLS = 16

LANES = 128
MOD_ROWS = 16
C_DOWN_PAD = 768
C_HEAD_PAD = LANES
VMEM_LIMIT_BYTES = 56 * 1024 * 1024
TOKEN_TILE = 512
FF_CHUNK = 256
KEY_CHUNK = 512
GRID_ROWS = DEC_SEQ // GRID_W
NA_WIN_ROWS = NA_ROWS + 2
NA_WIN_KEYS = NA_WIN_ROWS * GRID_W
NA_SLOTS = NA_WIN_ROWS


def _params(n_axes):
    return pltpu.CompilerParams(dimension_semantics=("arbitrary",) * n_axes,
                                vmem_limit_bytes=VMEM_LIMIT_BYTES)


def _resident(shape):
    nd = len(shape)
    return pl.BlockSpec(shape, lambda *_: (0,) * nd, pipeline_mode=pl.Buffered(1))


def _mod_spec(layer, part, row_fn):
    base = (layer * N_MOD + part) * MOD_ROWS
    return pl.BlockSpec((None, 1, D_MODEL), lambda i: (base + row_fn(i), 0, 0))


def _prompt_row(i):
    return 0


def _sample_row(tile):
    per_batch = DEC_SEQ // tile
    return lambda i: 1 + i // per_batch


def _silu(a):
    return a * (1.0 / (1.0 + jnp.exp(-a)))


def _modulate(x, g, shift, scale):
    y = x * lax.rsqrt(jnp.mean(x * x, axis=-1, keepdims=True) + RMS_EPS)
    return (y * g) * (1.0 + scale) + shift


def _swap_pairs(y, even):
    n = y.shape[-1]
    return jnp.where(even, pltpu.roll(y, n - 1, 1), pltpu.roll(y, 1, 1))


def _mod_kernel(c_ref, w_ref, b_ref, o_ref):
    s = _silu(c_ref[...]).astype(BF16)
    o_ref[...] = jnp.dot(s, w_ref[...].astype(BF16), preferred_element_type=F32) + b_ref[...]


def _mod_table(cond, mod_w, mod_b):
    out = pl.pallas_call(
        _mod_kernel,
        grid=(DEPTH, N_MOD),
        in_specs=[
            pl.BlockSpec((MOD_ROWS, D_MODEL), lambda l, j: (0, 0)),
            pl.BlockSpec((None, D_MODEL, D_MODEL), lambda l, j: (l, 0, j)),
            pl.BlockSpec((None, 1, D_MODEL), lambda l, j: (l, 0, j)),
        ],
        out_specs=pl.BlockSpec((None, None, MOD_ROWS, D_MODEL), lambda l, j: (l, j, 0, 0)),
        out_shape=jax.ShapeDtypeStruct((DEPTH, N_MOD, MOD_ROWS, D_MODEL), F32),
        compiler_params=_params(2),
        name="mod_table",
    )(cond, mod_w, mod_b.reshape(DEPTH, 1, N_MOD * D_MODEL))
    return out.reshape(DEPTH * N_MOD * MOD_ROWS, 1, D_MODEL)


def _ffn_kernel(x_ref, sh_ref, sc_ref, gt_ref, g_ref, wgu_ref, wd_ref, o_ref, act_ref):
    x = x_ref[...]
    h = _modulate(x, g_ref[...], sh_ref[...], sc_ref[...]).astype(BF16)
    for c0 in range(0, D_FF, FF_CHUNK):
        a = jnp.dot(h, wgu_ref[:, c0:c0 + FF_CHUNK], preferred_element_type=F32)
        u = jnp.dot(h, wgu_ref[:, D_FF + c0:D_FF + c0 + FF_CHUNK], preferred_element_type=F32)
        act_ref[:, c0:c0 + FF_CHUNK] = (_silu(a) * u).astype(BF16)
    y = jnp.dot(act_ref[...], wd_ref[...], preferred_element_type=F32)
    o_ref[...] = x + (FFN_RES * gt_ref[...]) * y


def _half_ffn(x, modv, layer, part0, g, wgu, wd, row_fn):
    n_tok = x.shape[0]
    tile = pl.BlockSpec((TOKEN_TILE, D_MODEL), lambda i: (i, 0))
    return pl.pallas_call(
        _ffn_kernel,
        grid=(n_tok // TOKEN_TILE,),
        in_specs=[tile,
                  _mod_spec(layer, part0, row_fn), _mod_spec(layer, part0 + 1, row_fn),
                  _mod_spec(layer, part0 + 2, row_fn),
                  _resident((1, D_MODEL)), _resident((D_MODEL, 2 * D_FF)), _resident((D_FF, D_MODEL))],
        out_specs=tile,
        out_shape=jax.ShapeDtypeStruct((n_tok, D_MODEL), F32),
        scratch_shapes=[pltpu.VMEM((TOKEN_TILE, D_FF), BF16)],
        compiler_params=_params(1),
        name=f"ffn_{n_tok}",
    )(x, modv, modv, modv, g.reshape(1, D_MODEL), wgu, wd)


def _oproj_kernel(x_ref, o_ref, gt_ref, w_ref, y_ref):
    y_ref[...] = x_ref[...] + gt_ref[...] * jnp.dot(o_ref[...], w_ref[...], preferred_element_type=F32)


def _out_proj(x, o, modv, layer, w_o, row_fn):
    n_tok = x.shape[0]
    tile = pl.BlockSpec((TOKEN_TILE, D_MODEL), lambda i: (i, 0))
    return pl.pallas_call(
        _oproj_kernel,
        grid=(n_tok // TOKEN_TILE,),
        in_specs=[tile, tile, _mod_spec(layer, 5, row_fn), _resident((D_MODEL, D_MODEL))],
        out_specs=tile,
        out_shape=jax.ShapeDtypeStruct((n_tok, D_MODEL), F32),
        compiler_params=_params(1),
        name=f"oproj_{n_tok}",
    )(x, o, modv, w_o)


def _head_pair_norm(yp, gain2, lo):
    sq = yp * yp
    s_lo = jnp.sum(jnp.where(lo, sq, 0.0), axis=-1, keepdims=True)
    s_hi = jnp.sum(jnp.where(lo, 0.0, sq), axis=-1, keepdims=True)
    ms = jnp.where(lo, s_lo, s_hi) * (1.0 / HEAD_DIM)
    return (yp * lax.rsqrt(ms + RMS_EPS)) * gain2


def _gqa_proj_kernel(*refs, n_q, n_kv, rope):
    x_ref, sh_ref, sc_ref, g_ref, w_ref, qn_ref, kn_ref = refs[:7]
    if rope:
        cos_ref, sin_ref = refs[7:9]
        q_ref, k_ref, v_ref = refs[9:]
    else:
        q_ref, k_ref, v_ref = refs[7:]
    h = _modulate(x_ref[...], g_ref[...], sh_ref[...], sc_ref[...]).astype(BF16)
    lane = lax.broadcasted_iota(jnp.int32, (1, LANES), 1)
    lo = lane < HEAD_DIM
    even = (lane & 1) == 0
    q_cols = n_q * HEAD_DIM
    k_cols = n_kv * HEAD_DIM

    def normed(c0, gain_ref, out_ref, o0):
        y = jnp.dot(h, w_ref[:, c0:c0 + 2 * LANES], preferred_element_type=F32)
        for p in range(2):
            yn = _head_pair_norm(y[:, p * LANES:(p + 1) * LANES], gain_ref[...], lo)
            if rope:
                yn = yn * cos_ref[...] + _swap_pairs(yn, even) * sin_ref[...]
            out_ref[:, o0 + p * LANES:o0 + (p + 1) * LANES] = yn.astype(out_ref.dtype)

    for c0 in range(0, q_cols, 2 * LANES):
        normed(c0, qn_ref, q_ref, c0)
    for c0 in range(0, k_cols, 2 * LANES):
        normed(q_cols + c0, kn_ref, k_ref, c0)
    for c0 in range(0, k_cols, 2 * LANES):
        v = jnp.dot(h, w_ref[:, q_cols + k_cols + c0:q_cols + k_cols + c0 + 2 * LANES],
                    preferred_element_type=F32)
        v_ref[:, c0:c0 + 2 * LANES] = v.astype(v_ref.dtype)


def _gqa_project(x, modv, layer, g_mix, w_qkv, q_norm, k_norm, n_kv, row_fn, rope_tabs, kv_dtype):
    n_tok = x.shape[0]
    q_cols, k_cols = N_HEADS * HEAD_DIM, n_kv * HEAD_DIM
    rope = rope_tabs is not None
    tile = lambda w: pl.BlockSpec((TOKEN_TILE, w), lambda i: (i, 0))
    in_specs = [tile(D_MODEL), _mod_spec(layer, 3, row_fn), _mod_spec(layer, 4, row_fn),
                _resident((1, D_MODEL)), _resident((D_MODEL, q_cols + 2 * k_cols)),
                _resident((1, LANES)), _resident((1, LANES))]
    args = [x, modv, modv, g_mix.reshape(1, D_MODEL), w_qkv,
            jnp.tile(q_norm, 2).reshape(1, LANES), jnp.tile(k_norm, 2).reshape(1, LANES)]
    if rope:
        per_batch = DEC_SEQ // TOKEN_TILE
        tab = pl.BlockSpec((TOKEN_TILE, LANES), lambda i: (i % per_batch, 0))
        in_specs += [tab, tab]
        args += list(rope_tabs)
    return pl.pallas_call(
        functools.partial(_gqa_proj_kernel, n_q=N_HEADS, n_kv=n_kv, rope=rope),
        grid=(n_tok // TOKEN_TILE,),
        in_specs=in_specs,
        out_specs=[tile(q_cols), tile(k_cols), tile(k_cols)],
        out_shape=[jax.ShapeDtypeStruct((n_tok, q_cols), BF16),
                   jax.ShapeDtypeStruct((n_tok, k_cols), kv_dtype),
                   jax.ShapeDtypeStruct((n_tok, k_cols), kv_dtype)],
        compiler_params=_params(1),
        name=f"gqa_proj_{n_tok}_kv{n_kv}",
    )(*args)


def _mla_group_norm(y, gain):
    ms = jnp.sum(y * y, axis=-1, keepdims=True) * (1.0 / C_QK)
    return (y * lax.rsqrt(ms + RMS_EPS)) * gain


def _mla_expand_kv(ckv, kr128, wk_ref, wv_ref, kn_ref, rope_refs, k_ref, v_ref):
    lane = lax.broadcasted_iota(jnp.int32, (1, LANES), 1)
    even = (lane & 1) == 0
    for c0 in range(0, N_HEADS * C_HEAD_PAD, 2 * LANES):
        y = jnp.dot(ckv, wk_ref[:, c0:c0 + 2 * LANES], preferred_element_type=F32)
        for p in range(2):
            kn = _mla_group_norm(y[:, p * LANES:(p + 1) * LANES] + kr128, kn_ref[...])
            if rope_refs is not None:
                kn = kn * rope_refs[0][...] + _swap_pairs(kn, even) * rope_refs[1][...]
            k_ref[:, c0 + p * LANES:c0 + (p + 1) * LANES] = kn.astype(k_ref.dtype)
    for c0 in range(0, N_HEADS * C_V, 2 * LANES):
        v = jnp.dot(ckv, wv_ref[:, c0:c0 + 2 * LANES], preferred_element_type=F32)
        v_ref[:, c0:c0 + 2 * LANES] = v.astype(v_ref.dtype)


def _mla_proj_kernel(*refs, rope):
    (x_ref, sh_ref, sc_ref, g_ref, wd_ref, qln_ref, kvln_ref, wuq_ref, wk_ref, wv_ref,
     qn_ref, kn_ref) = refs[:12]
    if rope:
        rope_refs = refs[12:14]
        q_ref, k_ref, v_ref, ckv_ref, kr_ref = refs[14:]
    else:
        rope_refs = None
        q_ref, k_ref, v_ref, ckv_ref, kr_ref = refs[12:]
    h = _modulate(x_ref[...], g_ref[...], sh_ref[...], sc_ref[...]).astype(BF16)
    y = jnp.dot(h, wd_ref[...], preferred_element_type=F32)

    def row_norm(z, gain):
        return (z * lax.rsqrt(jnp.mean(z * z, axis=-1, keepdims=True) + RMS_EPS)) * gain

    cq = row_norm(y[:, :C_Q_RANK], qln_ref[...]).astype(BF16)
    ckv = row_norm(y[:, C_Q_RANK:C_Q_RANK + C_KV_RANK], kvln_ref[...])
    kr128 = pltpu.roll(y[:, C_Q_RANK + C_KV_RANK:], C_NOPE, 1)
    ckv_ref[...] = ckv
    kr_ref[...] = kr128

    lane = lax.broadcasted_iota(jnp.int32, (1, LANES), 1)
    even = (lane & 1) == 0
    for c0 in range(0, N_HEADS * C_HEAD_PAD, 2 * LANES):
        yq = jnp.dot(cq, wuq_ref[:, c0:c0 + 2 * LANES], preferred_element_type=F32)
        for p in range(2):
            qn = _mla_group_norm(yq[:, p * LANES:(p + 1) * LANES], qn_ref[...])
            if rope:
                qn = qn * rope_refs[0][...] + _swap_pairs(qn, even) * rope_refs[1][...]
            q_ref[:, c0 + p * LANES:c0 + (p + 1) * LANES] = qn.astype(q_ref.dtype)
    _mla_expand_kv(ckv.astype(BF16), kr128, wk_ref, wv_ref, kn_ref, rope_refs, k_ref, v_ref)


def _mla_project(x, modv, layer, g_mix, w, row_fn, rope_tabs):
    n_tok = x.shape[0]
    rope = rope_tabs is not None
    tile = lambda wd: pl.BlockSpec((TOKEN_TILE, wd), lambda i: (i, 0))
    qk_cols = N_HEADS * C_HEAD_PAD
    in_specs = [tile(D_MODEL), _mod_spec(layer, 3, row_fn), _mod_spec(layer, 4, row_fn),
                _resident((1, D_MODEL)), _resident((D_MODEL, C_DOWN_PAD)),
                _resident((1, C_Q_RANK)), _resident((1, C_KV_RANK)),
                _resident((C_Q_RANK, qk_cols)), _resident((C_KV_RANK, qk_cols)),
                _resident((C_KV_RANK, N_HEADS * C_V)),
                _resident((1, LANES)), _resident((1, LANES))]
    args = [x, modv, modv, g_mix.reshape(1, D_MODEL), w["down"], w["q_lnorm"], w["kv_lnorm"],
            w["uq"], w["uk"], w["uv"], w["q_norm"], w["k_norm"]]
    if rope:
        per_batch = DEC_SEQ // TOKEN_TILE
        tab = pl.BlockSpec((TOKEN_TILE, LANES), lambda i: (i % per_batch, 0))
        in_specs += [tab, tab]
        args += list(rope_tabs)
    return pl.pallas_call(
        functools.partial(_mla_proj_kernel, rope=rope),
        grid=(n_tok // TOKEN_TILE,),
        in_specs=in_specs,
        out_specs=[tile(qk_cols), tile(qk_cols), tile(N_HEADS * C_V), tile(C_KV_RANK), tile(LANES)],
        out_shape=[jax.ShapeDtypeStruct((n_tok, qk_cols), BF16),
                   jax.ShapeDtypeStruct((n_tok, qk_cols), BF16),
                   jax.ShapeDtypeStruct((n_tok, N_HEADS * C_V), BF16),
                   jax.ShapeDtypeStruct((n_tok, C_KV_RANK), F32),
                   jax.ShapeDtypeStruct((n_tok, LANES), F32)],
        compiler_params=_params(1),
        name=f"mla_proj_{n_tok}",
    )(*args)


def _mla_cache_kernel(ckv_ref, kr_ref, wk_ref, wv_ref, kn_ref, k_ref, v_ref):
    _mla_expand_kv(ckv_ref[...].astype(BF16), kr_ref[...], wk_ref, wv_ref, kn_ref, None, k_ref, v_ref)


def _mla_expand_cache(ckv, kr128, w):
    n_tok = ckv.shape[0]
    tile = lambda wd: pl.BlockSpec((TOKEN_TILE, wd), lambda i: (i, 0))
    qk_cols = N_HEADS * C_HEAD_PAD
    return pl.pallas_call(
        _mla_cache_kernel,
        grid=(n_tok // TOKEN_TILE,),
        in_specs=[tile(C_KV_RANK), tile(LANES), _resident((C_KV_RANK, qk_cols)),
                  _resident((C_KV_RANK, N_HEADS * C_V)), _resident((1, LANES))],
        out_specs=[tile(qk_cols), tile(N_HEADS * C_V)],
        out_shape=[jax.ShapeDtypeStruct((n_tok, qk_cols), BF16),
                   jax.ShapeDtypeStruct((n_tok, N_HEADS * C_V), BF16)],
        compiler_params=_params(1),
        name="mla_cache_kv",
    )(ckv, kr128, w["uk"], w["uv"], w["k_norm"])


def _softmax_pv(q, segs, scale, sink):
    dims = (((1,), (1,)), ((), ()))
    m = sink
    denom = None if sink is None else jnp.ones_like(sink)
    out = None
    for k, v, mask in segs:
        s = lax.dot_general(q, k, dims, preferred_element_type=F32) * scale
        if mask is not None:
            s = jnp.where(mask, s, NEG_INF)
        row_max = jnp.max(s, axis=-1, keepdims=True)
        m_new = row_max if m is None else jnp.maximum(m, row_max)
        p = jnp.exp(s - m_new)
        row_sum = jnp.sum(p, axis=-1, keepdims=True)
        pv = jnp.dot(p.astype(BF16), v, preferred_element_type=F32)
        if m is None:
            denom, out = row_sum, pv
        else:
            rescale = jnp.exp(m - m_new)
            denom = denom * rescale + row_sum
            out = pv if out is None else out * rescale + pv
        m = m_new
    return out * (1.0 / denom)


def _attn_kernel(*refs, kv_step, group, dk, scale, seg_kinds, has_sink, tq):
    n_piece = sum(len(kinds) for kinds in seg_kinds)
    q_ref = refs[0]
    kv_refs = refs[1:1 + 2 * n_piece]
    sink_ref = refs[1 + 2 * n_piece] if has_sink else None
    o_ref = refs[-1]
    rows = group * tq
    first_head = pl.program_id(1) * (kv_step * group)
    blk = pl.program_id(2)
    n_blk = pl.num_programs(2)

    def piece_mask(kind, n_keys):
        r = lax.broadcasted_iota(jnp.int32, (rows, n_keys), 0) & (tq - 1)
        c = lax.broadcasted_iota(jnp.int32, (rows, n_keys), 1)
        if kind == "prev":
            return (c >= r) & (blk > 0)
        if kind == "next":
            return (c <= r - (tq - A_WINDOW)) & (blk < n_blk - 1)
        if kind == "band":
            return jnp.abs(r - c) <= A_WINDOW
        return jnp.full((rows, n_keys), True)

    seg_refs, seg_masks = [], []
    p_i = 0
    for kinds in seg_kinds:
        pieces = [(kv_refs[2 * (p_i + n)], kv_refs[2 * (p_i + n) + 1]) for n in range(len(kinds))]
        p_i += len(kinds)
        seg_refs.append(pieces)
        if all(kind == "full" for kind in kinds):
            seg_masks.append(None)
        else:
            parts = [piece_mask(kind, kr.shape[0]) for kind, (kr, _) in zip(kinds, pieces)]
            seg_masks.append(parts[0] if len(parts) == 1 else jnp.concatenate(parts, axis=1))

    def cat(parts, axis):
        return parts[0] if len(parts) == 1 else jnp.concatenate(parts, axis=axis)

    for hk in range(kv_step):
        heads = range(hk * group, (hk + 1) * group)
        q = cat([q_ref[:, h * dk:(h + 1) * dk] for h in heads], 0)
        segs = []
        for pieces, mask in zip(seg_refs, seg_masks):
            k = cat([kr[:, hk * dk:(hk + 1) * dk].astype(BF16) for kr, _ in pieces], 0)
            v = cat([vr[:, hk * HEAD_DIM:(hk + 1) * HEAD_DIM].astype(BF16) for _, vr in pieces], 0)
            for c0 in range(0, k.shape[0], KEY_CHUNK):
                chunk = slice(c0, c0 + KEY_CHUNK)
                segs.append((k[chunk], v[chunk], None if mask is None else mask[:, chunk]))
        sink = cat([jnp.full((tq, 1), sink_ref[first_head + h], F32) for h in heads], 0) if has_sink else None
        out = _softmax_pv(q, segs, scale, sink)
        for g_i, h in enumerate(heads):
            o_ref[:, h * HEAD_DIM:(h + 1) * HEAD_DIM] = out[g_i * tq:(g_i + 1) * tq].astype(o_ref.dtype)


def _attention(q, segs, *, n_batch, n_tok, tq, n_kv, kv_step, dk, scale, sink=None):
    n_blk = n_tok // tq
    per_blk = tq // A_WINDOW
    n_win = n_tok // A_WINDOW
    group = N_HEADS // n_kv
    in_specs = [pl.BlockSpec((tq, kv_step * group * dk), lambda b, g, i: (b * n_blk + i, g))]
    args = [q]
    for pieces in segs:
        for k, v, rows, kind in pieces:
            if kind == "full":
                idx = lambda b, g, i: (b, g)
            elif kind == "band":
                idx = lambda b, g, i: (b * n_blk + i, g)
            elif kind == "prev":
                idx = lambda b, g, i: (b * n_win + jnp.maximum(i * per_blk - 1, 0), g)
            else:
                idx = lambda b, g, i: (b * n_win + jnp.minimum((i + 1) * per_blk, n_win - 1), g)
            in_specs += [pl.BlockSpec((rows, kv_step * dk), idx), pl.BlockSpec((rows, kv_step * HEAD_DIM), idx)]
            args += [k, v]
    seg_kinds = tuple(tuple(kind for _, _, _, kind in pieces) for pieces in segs)
    if sink is not None:
        in_specs.append(pl.BlockSpec(memory_space=pltpu.SMEM))
        args.append(sink)
    return pl.pallas_call(
        functools.partial(_attn_kernel, kv_step=kv_step, group=group, dk=dk, scale=scale,
                          seg_kinds=seg_kinds, has_sink=sink is not None, tq=tq),
        grid=(n_batch, n_kv // kv_step, n_blk),
        in_specs=in_specs,
        out_specs=pl.BlockSpec((tq, kv_step * group * HEAD_DIM), lambda b, g, i: (b * n_blk + i, g)),
        out_shape=jax.ShapeDtypeStruct((n_batch * n_tok, N_HEADS * HEAD_DIM), BF16),
        compiler_params=_params(3),
        name=f"attn_{n_batch}x{n_tok}_kv{n_kv}_dk{dk}_" + "_".join(k[0] for ks in seg_kinds for k in ks),
    )(*args)


def _na_row_window(r):
    rs = min(max(r - NA_ROWS // 2, 0), GRID_ROWS - NA_ROWS)
    ws = min(rs - rs % 2, GRID_ROWS - NA_WIN_ROWS)
    return ws, r - ws, rs - ws


def _na_kernel(q_ref, k_ref, v_ref, kc_ref, vc_ref, bias_ref, o_ref, s_ref, sc_ref, p_ref, pc_ref, l_ref):
    scale = HEAD_DIM ** -0.5
    dims = (((1,), (1,)), ((), ()))

    @pl.when((pl.program_id(0) == 0) & (pl.program_id(1) == 0))
    def _():
        p_ref[...] = jnp.zeros_like(p_ref)

    for hh in range(LANES // HEAD_DIM):
        cols = slice(hh * HEAD_DIM, (hh + 1) * HEAD_DIM)
        q = q_ref[:, cols]
        kc = kc_ref[:, cols].astype(BF16)
        vc = vc_ref[:, cols].astype(BF16)
        s_ref[hh] = lax.dot_general(q, k_ref[:, cols], dims, preferred_element_type=F32) * scale
        sc_ref[hh] = lax.dot_general(q, kc, dims, preferred_element_type=F32) * scale
        for r in range(GRID_ROWS):
            ws, slot, _ = _na_row_window(r)
            rows = slice(r * GRID_W, (r + 1) * GRID_W)
            win = slice(ws * GRID_W, (ws + NA_WIN_ROWS) * GRID_W)
            s_n = s_ref[hh, rows, win] + bias_ref[hh, slot]
            s_c = sc_ref[hh, rows, :]
            m = jnp.maximum(jnp.max(s_n, axis=-1, keepdims=True), jnp.max(s_c, axis=-1, keepdims=True))
            p_n = jnp.exp(s_n - m)
            p_c = jnp.exp(s_c - m)
            l_ref[hh, rows, :] = jnp.sum(p_n, axis=-1, keepdims=True) + jnp.sum(p_c, axis=-1, keepdims=True)
            p_ref[hh, rows, win] = p_n.astype(BF16)
            pc_ref[hh, rows, :] = p_c.astype(BF16)
        out = (jnp.dot(p_ref[hh], v_ref[:, cols], preferred_element_type=F32)
               + jnp.dot(pc_ref[hh], vc, preferred_element_type=F32))
        o_ref[:, cols] = (out * (1.0 / l_ref[hh])).astype(o_ref.dtype)


def _na_bias_table(rpb):
    n_h, n_ri, n_ci = rpb.shape
    period = 2 * GRID_W
    u = jnp.concatenate([rpb[..., NA_COLS - 1:], jnp.zeros((n_h, n_ri, period - n_ci), F32),
                         rpb[..., :NA_COLS - 1]], axis=-1)
    flat = jnp.broadcast_to(u[:, :, None, :], (n_h, n_ri, GRID_W, period)).reshape(n_h, n_ri, -1)
    toep = flat[..., :GRID_W * (period - 1)].reshape(n_h, n_ri, GRID_W, period - 1)[..., :GRID_W]
    qcol = jnp.arange(GRID_W)[:, None]
    kcol = jnp.arange(GRID_W)[None, :]
    cs = jnp.clip(qcol - NA_COLS // 2, 0, GRID_W - NA_COLS)
    valid = (kcol >= cs) & (kcol < cs + NA_COLS)
    toep = jnp.where(valid, toep, NEG_INF)
    first_visible = {slot: first for _, slot, first in map(_na_row_window, range(GRID_ROWS))}
    assert sorted(first_visible) == list(range(NA_SLOTS))
    pad = NA_WIN_ROWS
    toep = jnp.pad(toep, ((0, 0), (pad, pad), (0, 0), (0, 0)), constant_values=NEG_INF)
    slots = []
    for d in range(NA_SLOTS):
        lo = NA_ROWS - 1 - d + pad
        rows = toep[:, lo:lo + NA_WIN_ROWS]
        j = jnp.arange(NA_WIN_ROWS)[None, :, None, None]
        visible = (j >= first_visible[d]) & (j < first_visible[d] + NA_ROWS)
        slots.append(jnp.where(visible, rows, NEG_INF))
    table = jnp.stack(slots, axis=1)
    return table.transpose(0, 1, 3, 2, 4).reshape(n_h, NA_SLOTS, GRID_W, NA_WIN_KEYS)


def _na_attention(q, k, v, kc, vc, bias):
    n_pairs = N_HEADS * HEAD_DIM // LANES
    lat = pl.BlockSpec((DEC_SEQ, LANES), lambda hp, b: (b, hp))
    ctx = pl.BlockSpec((PAST_LEN, LANES), lambda hp, b: (b, hp))
    return pl.pallas_call(
        _na_kernel,
        grid=(n_pairs, DEC_BATCH),
        in_specs=[lat, lat, lat, ctx, ctx,
                  pl.BlockSpec((LANES // HEAD_DIM, NA_SLOTS, GRID_W, NA_WIN_KEYS),
                               lambda hp, b: (hp, 0, 0, 0))],
        out_specs=lat,
        out_shape=jax.ShapeDtypeStruct((DEC_BATCH * DEC_SEQ, N_HEADS * HEAD_DIM), BF16),
        scratch_shapes=[pltpu.VMEM((LANES // HEAD_DIM, DEC_SEQ, DEC_SEQ), F32),
                        pltpu.VMEM((LANES // HEAD_DIM, DEC_SEQ, PAST_LEN), F32),
                        pltpu.VMEM((LANES // HEAD_DIM, DEC_SEQ, DEC_SEQ), BF16),
                        pltpu.VMEM((LANES // HEAD_DIM, DEC_SEQ, PAST_LEN), BF16),
                        pltpu.VMEM((LANES // HEAD_DIM, DEC_SEQ, 1), F32)],
        compiler_params=_params(2),
        name="na_attn",
    )(q, k, v, kc, vc, bias)


def _grid_angles(n, rot_dim):
    pos = jnp.arange(n, dtype=jnp.int32)
    row = (pos // GRID_W).astype(F32)
    col = (pos % GRID_W).astype(F32)
    n_ax = rot_dim // 4
    inv = ROPE_THETA ** (-jnp.arange(n_ax, dtype=F32) / n_ax)
    return jnp.concatenate([row[:, None] * inv, col[:, None] * inv], axis=-1)


def _pair_tables(ang):
    cos = jnp.repeat(jnp.cos(ang), 2, axis=-1)
    sin = jnp.stack([-jnp.sin(ang), jnp.sin(ang)], axis=-1).reshape(ang.shape[0], -1)
    return cos, sin


def _gqa_rope_tables():
    cos, sin = _pair_tables(_grid_angles(DEC_SEQ, HEAD_DIM))
    return jnp.tile(cos, (1, LANES // HEAD_DIM)), jnp.tile(sin, (1, LANES // HEAD_DIM))


def _mla_rope_tables():
    cos, sin = _pair_tables(_grid_angles(DEC_SEQ, C_ROPE))
    pad = LANES - C_QK
    cos = jnp.concatenate([jnp.ones((DEC_SEQ, C_NOPE), F32), cos, jnp.ones((DEC_SEQ, pad), F32)], axis=-1)
    sin = jnp.concatenate([jnp.zeros((DEC_SEQ, C_NOPE), F32), sin, jnp.zeros((DEC_SEQ, pad), F32)], axis=-1)
    return cos, sin


def _mla_weights(w_down, q_lnorm, kv_lnorm, w_uq, w_ukv, q_norm, k_norm):
    pad_head = lambda a: jnp.pad(a, [(0, 0)] * (a.ndim - 1) + [(0, C_HEAD_PAD - a.shape[-1])])
    ukv = w_ukv.reshape(C_KV_RANK, N_HEADS, C_NOPE + C_V)
    return {
        "down": jnp.pad(w_down, ((0, 0), (0, C_DOWN_PAD - w_down.shape[1]))).astype(BF16),
        "q_lnorm": q_lnorm.reshape(1, C_Q_RANK),
        "kv_lnorm": kv_lnorm.reshape(1, C_KV_RANK),
        "uq": pad_head(w_uq.reshape(C_Q_RANK, N_HEADS, C_QK)).reshape(C_Q_RANK, -1).astype(BF16),
        "uk": pad_head(ukv[:, :, :C_NOPE]).reshape(C_KV_RANK, -1).astype(BF16),
        "uv": ukv[:, :, C_NOPE:].reshape(C_KV_RANK, -1).astype(BF16),
        "q_norm": pad_head(q_norm).reshape(1, C_HEAD_PAD),
        "k_norm": pad_head(k_norm).reshape(1, C_HEAD_PAD),
    }


def kernel(x_prompt, x_sample, cache_a_k, cache_a_v, cache_b_k, cache_b_v, cache_c_kv, cache_c_krope, cache_d_k, cache_d_v, c, c_ctx, mod_w, mod_b, norm_ff1, norm_mix, norm_ff2, ff1_w_gu, ff1_w_down, ff2_w_gu, ff2_w_down, a_w_qkv, a_q_norm, a_k_norm, a_sink, a_w_o, b_w_qkv, b_q_norm, b_k_norm, b_w_o, c_w_down, c_q_lnorm, c_kv_lnorm, c_w_uq, c_w_ukv, c_q_norm, c_k_norm, c_w_o, d_w_qkv, d_q_norm, d_k_norm, d_rpb, d_w_o):
    n_p, n_s = BATCH * SEQ, DEC_BATCH * DEC_SEQ
    xp = x_prompt.reshape(n_p, D_MODEL)
    xs = x_sample.reshape(n_s, D_MODEL)
    cond = jnp.concatenate([c_ctx[None], c, jnp.zeros((MOD_ROWS - 1 - DEC_BATCH, D_MODEL), F32)], axis=0)
    modv = _mod_table(cond, mod_w, mod_b)
    s_row = _sample_row(TOKEN_TILE)
    gqa_tabs = _gqa_rope_tables()
    gqa_scale = HEAD_DIM ** -0.5
    flat_cache = lambda a: a.reshape(DEC_BATCH * PAST_LEN, -1)
    new = {}

    for i in range(DEPTH):
        kind, j = i % N_MIXERS, i // N_MIXERS
        wgu1, wd1 = ff1_w_gu[i].astype(BF16), ff1_w_down[i].astype(BF16)
        xp = _half_ffn(xp, modv, i, 0, norm_ff1[i], wgu1, wd1, _prompt_row)
        xs = _half_ffn(xs, modv, i, 0, norm_ff1[i], wgu1, wd1, s_row)

        if kind in (0, 1, 3):
            w_qkv, q_norm, k_norm, w_o, n_kv = {
                0: (a_w_qkv, a_q_norm, a_k_norm, a_w_o, GQA_KV_HEADS),
                1: (b_w_qkv, b_q_norm, b_k_norm, b_w_o, GQA_KV_HEADS),
                3: (d_w_qkv, d_q_norm, d_k_norm, d_w_o, N_HEADS)}[kind]
            w_qkv = w_qkv[j].astype(BF16)
            qp, kp, vp = _gqa_project(xp, modv, i, norm_mix[i], w_qkv, q_norm[j], k_norm[j], n_kv,
                                      _prompt_row, None, F32)
            qs, ks, vs = _gqa_project(xs, modv, i, norm_mix[i], w_qkv, q_norm[j], k_norm[j], n_kv,
                                      s_row, gqa_tabs if kind != 3 else None, BF16)
            op = _attention(qp, [[(kp, vp, SEQ, "full")]], n_batch=BATCH, n_tok=SEQ, tq=SEQ, n_kv=n_kv,
                            kv_step=n_kv, dk=HEAD_DIM, scale=gqa_scale, sink=a_sink[j] if kind == 0 else None)
            kv_shape = (BATCH, 1, SEQ, n_kv, HEAD_DIM)
            if kind == 0:
                new["a_k"], new["a_v"] = kp.reshape(kv_shape), vp.reshape(kv_shape)
                kc, vc = flat_cache(cache_a_k[:, j]), flat_cache(cache_a_v[:, j])
                tq = 2 * A_WINDOW
                os_ = _attention(qs, [[(ks, vs, A_WINDOW, "prev"), (ks, vs, tq, "band"),
                                       (ks, vs, A_WINDOW, "next")], [(kc, vc, PAST_LEN, "full")]],
                                 n_batch=DEC_BATCH, n_tok=DEC_SEQ, tq=tq, n_kv=n_kv, kv_step=2, dk=HEAD_DIM,
                                 scale=gqa_scale, sink=a_sink[j])
            elif kind == 1:
                new["b_k"], new["b_v"] = kp.reshape(kv_shape), vp.reshape(kv_shape)
                kc, vc = flat_cache(cache_b_k[:, j]), flat_cache(cache_b_v[:, j])
                os_ = _attention(qs, [[(ks, vs, DEC_SEQ, "full")], [(kc, vc, PAST_LEN, "full")]],
                                 n_batch=DEC_BATCH, n_tok=DEC_SEQ, tq=512, n_kv=n_kv, kv_step=2, dk=HEAD_DIM,
                                 scale=gqa_scale)
            else:
                new["d_k"], new["d_v"] = kp.reshape(kv_shape), vp.reshape(kv_shape)
                kc, vc = flat_cache(cache_d_k[:, j]), flat_cache(cache_d_v[:, j])
                os_ = _na_attention(qs, ks, vs, kc, vc, _na_bias_table(d_rpb[j]))
        else:
            w = _mla_weights(c_w_down[j], c_q_lnorm[j], c_kv_lnorm[j], c_w_uq[j], c_w_ukv[j],
                             c_q_norm[j], c_k_norm[j])
            w_o = c_w_o
            qp, kp, vp, ckv_p, kr_p = _mla_project(xp, modv, i, norm_mix[i], w, _prompt_row, None)
            qs, ks, vs, _, _ = _mla_project(xs, modv, i, norm_mix[i], w, s_row, _mla_rope_tables())
            new["c_kv"] = ckv_p.reshape(BATCH, 1, SEQ, C_KV_RANK)
            new["c_krope"] = kr_p[:, C_NOPE:C_QK].reshape(BATCH, 1, SEQ, C_ROPE)
            kr_cache = jnp.pad(flat_cache(cache_c_krope[:, j]), ((0, 0), (C_NOPE, LANES - C_QK)))
            kc, vc = _mla_expand_cache(flat_cache(cache_c_kv[:, j]), kr_cache, w)
            mla_scale = C_QK ** -0.5
            op = _attention(qp, [[(kp, vp, SEQ, "full")]], n_batch=BATCH, n_tok=SEQ, tq=SEQ,
                            n_kv=N_HEADS, kv_step=N_HEADS, dk=C_HEAD_PAD, scale=mla_scale)
            os_ = _attention(qs, [[(ks, vs, DEC_SEQ, "full")], [(kc, vc, PAST_LEN, "full")]],
                             n_batch=DEC_BATCH, n_tok=DEC_SEQ, tq=DEC_SEQ, n_kv=N_HEADS, kv_step=2, dk=C_HEAD_PAD,
                             scale=mla_scale)

        w_o = w_o[j].astype(BF16)
        xp = _out_proj(xp, op, modv, i, w_o, _prompt_row)
        xs = _out_proj(xs, os_, modv, i, w_o, s_row)
        wgu2, wd2 = ff2_w_gu[i].astype(BF16), ff2_w_down[i].astype(BF16)
        xp = _half_ffn(xp, modv, i, 6, norm_ff2[i], wgu2, wd2, _prompt_row)
        xs = _half_ffn(xs, modv, i, 6, norm_ff2[i], wgu2, wd2, s_row)

    return (xp.reshape(BATCH, SEQ, D_MODEL), xs.reshape(DEC_BATCH, DEC_SEQ, D_MODEL),
            new["a_k"], new["a_v"], new["b_k"], new["b_v"], new["c_kv"], new["c_krope"],
            new["d_k"], new["d_v"])
```

```python
import functools

import jax
import jax.numpy as jnp
from jax import lax
from jax.experimental import pallas as pl
from jax.experimental.pallas import tpu as pltpu

F32 = jnp.float32
BF16 = jnp.bfloat16

D_MODEL = 1024
BATCH = 16
SEQ = 256
DEPTH = 4
DEC_BATCH = 8
DEC_SEQ = 1024
PAST_LEN = 512
GRID_W = 64
N_MIXERS = 4
RMS_EPS = 1e-6
ROPE_THETA = 10000.0
NEG_INF = -1e30
D_FF = 2816
FFN_RES = 0.5
N_MOD = 9
HEAD_DIM = 64
N_HEADS = 16
GQA_KV_HEADS = 4
A_WINDOW = 128
C_Q_RANK = 384
C_KV_RANK = 256
C_NOPE = 64
C_ROPE = 32
C_V = 64
C_QK = C_NOPE + C_ROPE
NA_ROWS = 8
NA_COLS = 16

LANES = 128
MOD_ROWS = 16
C_DOWN_PAD = 768
C_HEAD_PAD = LANES
VMEM_LIMIT_BYTES = 56 * 1024 * 1024
TOKEN_TILE = 512
FF_CHUNK = 256
KEY_CHUNK = 512
LOG2_E = 1.4426950408889634
GQA_SCALE = HEAD_DIM ** -0.5
GRID_ROWS = DEC_SEQ // GRID_W
NA_WIN_ROWS = NA_ROWS + 2
NA_WIN_KEYS = NA_WIN_ROWS * GRID_W
NA_SLOTS = NA_WIN_ROWS


def _params(n_axes):
    return pltpu.CompilerParams(dimension_semantics=("arbitrary",) * n_axes,
                                vmem_limit_bytes=VMEM_LIMIT_BYTES)


def _resident(shape):
    nd = len(shape)
    return pl.BlockSpec(shape, lambda *_: (0,) * nd, pipeline_mode=pl.Buffered(1))


def _mod_spec(layer, part, row_fn):
    base = (layer * N_MOD + part) * MOD_ROWS
    return pl.BlockSpec((None, 1, D_MODEL), lambda i: (base + row_fn(i), 0, 0))


def _prompt_row(i):
    return 0


def _sample_row(tile):
    per_batch = DEC_SEQ // tile
    return lambda i: 1 + i // per_batch


def _silu(a):
    return a * (1.0 / (1.0 + jnp.exp(-a)))


def _modulate(x, g, shift, scale):
    y = x * lax.rsqrt(jnp.mean(x * x, axis=-1, keepdims=True) + RMS_EPS)
    return (y * g) * (1.0 + scale) + shift


def _swap_pairs(y, even):
    n = y.shape[-1]
    return jnp.where(even, pltpu.roll(y, n - 1, 1), pltpu.roll(y, 1, 1))


def _mod_kernel(c_ref, w_ref, b_ref, o_ref):
    s = _silu(c_ref[...]).astype(BF16)
    o_ref[...] = jnp.dot(s, w_ref[...].astype(BF16), preferred_element_type=F32) + b_ref[...]


def _mod_table(cond, mod_w, mod_b):
    out = pl.pallas_call(
        _mod_kernel,
        grid=(DEPTH, N_MOD),
        in_specs=[
            pl.BlockSpec((MOD_ROWS, D_MODEL), lambda l, j: (0, 0)),
            pl.BlockSpec((None, D_MODEL, D_MODEL), lambda l, j: (l, 0, j)),
            pl.BlockSpec((None, 1, D_MODEL), lambda l, j: (l, 0, j)),
        ],
        out_specs=pl.BlockSpec((None, None, MOD_ROWS, D_MODEL), lambda l, j: (l, j, 0, 0)),
        out_shape=jax.ShapeDtypeStruct((DEPTH, N_MOD, MOD_ROWS, D_MODEL), F32),
        compiler_params=_params(2),
        name="mod_table",
    )(cond, mod_w, mod_b.reshape(DEPTH, 1, N_MOD * D_MODEL))
    return out.reshape(DEPTH * N_MOD * MOD_ROWS, 1, D_MODEL)


def _ffn_kernel(*refs, mixer_out):
    if mixer_out:
        x_ref, attn_ref, gm_ref, wo_ref = refs[:4]
        refs = refs[4:]
        x = x_ref[...] + gm_ref[...] * jnp.dot(attn_ref[...], wo_ref[...], preferred_element_type=F32)
    else:
        x = refs[0][...]
        refs = refs[1:]
    sh_ref, sc_ref, gt_ref, g_ref, wgu_ref, wd_ref, o_ref, act_ref = refs
    h = _modulate(x, g_ref[...], sh_ref[...], sc_ref[...]).astype(BF16)
    for c0 in range(0, D_FF, FF_CHUNK):
        a = jnp.dot(h, wgu_ref[:, c0:c0 + FF_CHUNK], preferred_element_type=F32)
        u = jnp.dot(h, wgu_ref[:, D_FF + c0:D_FF + c0 + FF_CHUNK], preferred_element_type=F32)
        act_ref[:, c0:c0 + FF_CHUNK] = (_silu(a) * u).astype(BF16)
    y = jnp.dot(act_ref[...], wd_ref[...], preferred_element_type=F32)
    o_ref[...] = x + (FFN_RES * gt_ref[...]) * y


def _half_ffn(x, modv, layer, part0, g, wgu, wd, row_fn, mixer_out=None):
    n_tok = x.shape[0]
    tile = pl.BlockSpec((TOKEN_TILE, D_MODEL), lambda i: (i, 0))
    in_specs, args = [tile], [x]
    if mixer_out is not None:
        in_specs += [tile, _mod_spec(layer, 5, row_fn), _resident((D_MODEL, D_MODEL))]
        args += [mixer_out[0], modv, mixer_out[1]]
    in_specs += [_mod_spec(layer, part0, row_fn), _mod_spec(layer, part0 + 1, row_fn),
                 _mod_spec(layer, part0 + 2, row_fn),
                 _resident((1, D_MODEL)), _resident((D_MODEL, 2 * D_FF)), _resident((D_FF, D_MODEL))]
    args += [modv, modv, modv, g.reshape(1, D_MODEL), wgu, wd]
    return pl.pallas_call(
        functools.partial(_ffn_kernel, mixer_out=mixer_out is not None),
        grid=(n_tok // TOKEN_TILE,),
        in_specs=in_specs,
        out_specs=tile,
        out_shape=jax.ShapeDtypeStruct((n_tok, D_MODEL), F32),
        scratch_shapes=[pltpu.VMEM((TOKEN_TILE, D_FF), BF16)],
        compiler_params=_params(1),
        name=f"ffn_{n_tok}" + ("_mix" if mixer_out is not None else ""),
    )(*args)


def _head_pair_norm(yp, gain2, lo):
    sq = yp * yp
    s_lo = jnp.sum(jnp.where(lo, sq, 0.0), axis=-1, keepdims=True)
    s_hi = jnp.sum(jnp.where(lo, 0.0, sq), axis=-1, keepdims=True)
    ms = jnp.where(lo, s_lo, s_hi) * (1.0 / HEAD_DIM)
    return (yp * lax.rsqrt(ms + RMS_EPS)) * gain2


def _gqa_proj_kernel(*refs, n_q, n_kv, rope):
    x_ref, sh_ref, sc_ref, g_ref, w_ref, qn_ref, kn_ref = refs[:7]
    if rope:
        cos_ref, sin_ref = refs[7:9]
        q_ref, k_ref, v_ref = refs[9:]
    else:
        q_ref, k_ref, v_ref = refs[7:]
    h = _modulate(x_ref[...], g_ref[...], sh_ref[...], sc_ref[...]).astype(BF16)
    lane = lax.broadcasted_iota(jnp.int32, (1, LANES), 1)
    lo = lane < HEAD_DIM
    even = (lane & 1) == 0
    q_cols = n_q * HEAD_DIM
    k_cols = n_kv * HEAD_DIM

    def normed(c0, gain_ref, out_ref, o0, post_scale):
        y = jnp.dot(h, w_ref[:, c0:c0 + 2 * LANES], preferred_element_type=F32)
        for p in range(2):
            yn = _head_pair_norm(y[:, p * LANES:(p + 1) * LANES], gain_ref[...], lo)
            if rope:
                yn = yn * cos_ref[...] + _swap_pairs(yn, even) * sin_ref[...]
            if post_scale is not None:
                yn = yn * post_scale
            out_ref[:, o0 + p * LANES:o0 + (p + 1) * LANES] = yn.astype(out_ref.dtype)

    for c0 in range(0, q_cols, 2 * LANES):
        normed(c0, qn_ref, q_ref, c0, GQA_SCALE)
    for c0 in range(0, k_cols, 2 * LANES):
        normed(q_cols + c0, kn_ref, k_ref, c0, None)
    for c0 in range(0, k_cols, 2 * LANES):
        v = jnp.dot(h, w_ref[:, q_cols + k_cols + c0:q_cols + k_cols + c0 + 2 * LANES],
                    preferred_element_type=F32)
        v_ref[:, c0:c0 + 2 * LANES] = v.astype(v_ref.dtype)


def _gqa_project(x, modv, layer, g_mix, w_qkv, q_norm, k_norm, n_kv, row_fn, rope_tabs, kv_dtype):
    n_tok = x.shape[0]
    q_cols, k_cols = N_HEADS * HEAD_DIM, n_kv * HEAD_DIM
    rope = rope_tabs is not None
    tile = lambda w: pl.BlockSpec((TOKEN_TILE, w), lambda i: (i, 0))
    in_specs = [tile(D_MODEL), _mod_spec(layer, 3, row_fn), _mod_spec(layer, 4, row_fn),
                _resident((1, D_MODEL)), _resident((D_MODEL, q_cols + 2 * k_cols)),
                _resident((1, LANES)), _resident((1, LANES))]
    args = [x, modv, modv, g_mix.reshape(1, D_MODEL), w_qkv,
            jnp.tile(q_norm, 2).reshape(1, LANES), jnp.tile(k_norm, 2).reshape(1, LANES)]
    if rope:
        per_batch = DEC_SEQ // TOKEN_TILE
        tab = pl.BlockSpec((TOKEN_TILE, LANES), lambda i: (i % per_batch, 0))
        in_specs += [tab, tab]
        args += list(rope_tabs)
    return pl.pallas_call(
        functools.partial(_gqa_proj_kernel, n_q=N_HEADS, n_kv=n_kv, rope=rope),
        grid=(n_tok // TOKEN_TILE,),
        in_specs=in_specs,
        out_specs=[tile(q_cols), tile(k_cols), tile(k_cols)],
        out_shape=[jax.ShapeDtypeStruct((n_tok, q_cols), BF16),
                   jax.ShapeDtypeStruct((n_tok, k_cols), kv_dtype),
                   jax.ShapeDtypeStruct((n_tok, k_cols), kv_dtype)],
        compiler_params=_params(1),
        name=f"gqa_proj_{n_tok}_kv{n_kv}",
    )(*args)


def _mla_group_norm(y, gain):
    ms = jnp.sum(y * y, axis=-1, keepdims=True) * (1.0 / C_QK)
    return (y * lax.rsqrt(ms + RMS_EPS)) * gain


def _mla_expand_kv(ckv, kr128, wk_ref, wv_ref, kn_ref, rope_refs, k_ref, v_ref):
    lane = lax.broadcasted_iota(jnp.int32, (1, LANES), 1)
    even = (lane & 1) == 0
    for c0 in range(0, N_HEADS * C_HEAD_PAD, 2 * LANES):
        y = jnp.dot(ckv, wk_ref[:, c0:c0 + 2 * LANES], preferred_element_type=F32)
        for p in range(2):
            kn = _mla_group_norm(y[:, p * LANES:(p + 1) * LANES] + kr128, kn_ref[...])
            if rope_refs is not None:
                kn = kn * rope_refs[0][...] + _swap_pairs(kn, even) * rope_refs[1][...]
            k_ref[:, c0 + p * LANES:c0 + (p + 1) * LANES] = kn.astype(k_ref.dtype)
    for c0 in range(0, N_HEADS * C_V, 2 * LANES):
        v = jnp.dot(ckv, wv_ref[:, c0:c0 + 2 * LANES], preferred_element_type=F32)
        v_ref[:, c0:c0 + 2 * LANES] = v.astype(v_ref.dtype)


def _mla_proj_kernel(*refs, rope):
    (x_ref, sh_ref, sc_ref, g_ref, wd_ref, qln_ref, kvln_ref, wuq_ref, wk_ref, wv_ref,
     qn_ref, kn_ref) = refs[:12]
    if rope:
        rope_refs = refs[12:14]
        q_ref, k_ref, v_ref, ckv_ref, kr_ref = refs[14:]
    else:
        rope_refs = None
        q_ref, k_ref, v_ref, ckv_ref, kr_ref = refs[12:]
    h = _modulate(x_ref[...], g_ref[...], sh_ref[...], sc_ref[...]).astype(BF16)
    y = jnp.dot(h, wd_ref[...], preferred_element_type=F32)

    def row_norm(z, gain):
        return (z * lax.rsqrt(jnp.mean(z * z, axis=-1, keepdims=True) + RMS_EPS)) * gain

    cq = row_norm(y[:, :C_Q_RANK], qln_ref[...]).astype(BF16)
    ckv = row_norm(y[:, C_Q_RANK:C_Q_RANK + C_KV_RANK], kvln_ref[...])
    kr128 = pltpu.roll(y[:, C_Q_RANK + C_KV_RANK:], C_NOPE, 1)
    ckv_ref[...] = ckv
    kr_ref[...] = kr128

    lane = lax.broadcasted_iota(jnp.int32, (1, LANES), 1)
    even = (lane & 1) == 0
    for c0 in range(0, N_HEADS * C_HEAD_PAD, 2 * LANES):
        yq = jnp.dot(cq, wuq_ref[:, c0:c0 + 2 * LANES], preferred_element_type=F32)
        for p in range(2):
            qn = _mla_group_norm(yq[:, p * LANES:(p + 1) * LANES], qn_ref[...])
            if rope:
                qn = qn * rope_refs[0][...] + _swap_pairs(qn, even) * rope_refs[1][...]
            q_ref[:, c0 + p * LANES:c0 + (p + 1) * LANES] = qn.astype(q_ref.dtype)
    _mla_expand_kv(ckv.astype(BF16), kr128, wk_ref, wv_ref, kn_ref, rope_refs, k_ref, v_ref)


def _mla_project(x, modv, layer, g_mix, w, row_fn, rope_tabs):
    n_tok = x.shape[0]
    rope = rope_tabs is not None
    tile = lambda wd: pl.BlockSpec((TOKEN_TILE, wd), lambda i: (i, 0))
    qk_cols = N_HEADS * C_HEAD_PAD
    in_specs = [tile(D_MODEL), _mod_spec(layer, 3, row_fn), _mod_spec(layer, 4, row_fn),
                _resident((1, D_MODEL)), _resident((D_MODEL, C_DOWN_PAD)),
                _resident((1, C_Q_RANK)), _resident((1, C_KV_RANK)),
                _resident((C_Q_RANK, qk_cols)), _resident((C_KV_RANK, qk_cols)),
                _resident((C_KV_RANK, N_HEADS * C_V)),
                _resident((1, LANES)), _resident((1, LANES))]
    args = [x, modv, modv, g_mix.reshape(1, D_MODEL), w["down"], w["q_lnorm"], w["kv_lnorm"],
            w["uq"], w["uk"], w["uv"], w["q_norm"], w["k_norm"]]
    if rope:
        per_batch = DEC_SEQ // TOKEN_TILE
        tab = pl.BlockSpec((TOKEN_TILE, LANES), lambda i: (i % per_batch, 0))
        in_specs += [tab, tab]
        args += list(rope_tabs)
    return pl.pallas_call(
        functools.partial(_mla_proj_kernel, rope=rope),
        grid=(n_tok // TOKEN_TILE,),
        in_specs=in_specs,
        out_specs=[tile(qk_cols), tile(qk_cols), tile(N_HEADS * C_V), tile(C_KV_RANK), tile(LANES)],
        out_shape=[jax.ShapeDtypeStruct((n_tok, qk_cols), BF16),
                   jax.ShapeDtypeStruct((n_tok, qk_cols), BF16),
                   jax.ShapeDtypeStruct((n_tok, N_HEADS * C_V), BF16),
                   jax.ShapeDtypeStruct((n_tok, C_KV_RANK), F32),
                   jax.ShapeDtypeStruct((n_tok, LANES), F32)],
        compiler_params=_params(1),
        name=f"mla_proj_{n_tok}",
    )(*args)


def _mla_cache_kernel(ckv_ref, kr_ref, wk_ref, wv_ref, kn_ref, k_ref, v_ref):
    _mla_expand_kv(ckv_ref[...].astype(BF16), kr_ref[...], wk_ref, wv_ref, kn_ref, None, k_ref, v_ref)


def _mla_expand_cache(ckv, kr128, w):
    n_tok = ckv.shape[0]
    tile = lambda wd: pl.BlockSpec((TOKEN_TILE, wd), lambda i: (i, 0))
    qk_cols = N_HEADS * C_HEAD_PAD
    return pl.pallas_call(
        _mla_cache_kernel,
        grid=(n_tok // TOKEN_TILE,),
        in_specs=[tile(C_KV_RANK), tile(LANES), _resident((C_KV_RANK, qk_cols)),
                  _resident((C_KV_RANK, N_HEADS * C_V)), _resident((1, LANES))],
        out_specs=[tile(qk_cols), tile(N_HEADS * C_V)],
        out_shape=[jax.ShapeDtypeStruct((n_tok, qk_cols), BF16),
                   jax.ShapeDtypeStruct((n_tok, N_HEADS * C_V), BF16)],
        compiler_params=_params(1),
        name="mla_cache_kv",
    )(ckv, kr128, w["uk"], w["uv"], w["k_norm"])


def _online_softmax(q, segs, scale, sink):
    dims = (((1,), (1,)), ((), ()))
    to_exp2 = scale * LOG2_E
    m = None
    acc = None
    for k, v1, bias in segs:
        s = lax.dot_general(q, k, dims, preferred_element_type=F32)
        if bias is not None:
            s = s + bias
        row_max = jnp.max(s, axis=-1, keepdims=True)
        m_new = row_max if m is None else jnp.maximum(m, row_max)
        p = jnp.exp2((s - m_new) * to_exp2)
        pv = jnp.dot(p.astype(BF16), v1, preferred_element_type=F32)
        acc = pv if m is None else acc * jnp.exp2((m - m_new) * to_exp2) + pv
        m = m_new
    if sink is not None:
        lane = lax.broadcasted_iota(jnp.int32, (1, 2 * HEAD_DIM), 1)
        acc = acc + jnp.where(lane >= HEAD_DIM, jnp.exp2((sink - m * scale) * LOG2_E), 0.0)
    denom = pltpu.roll(acc, HEAD_DIM, 1)
    return (acc * (1.0 / denom))[:, :HEAD_DIM]


def _attn_kernel(*refs, kv_step, group, dk, scale, seg_kinds, has_sink, tq):
    n_piece = sum(len(kinds) for kinds in seg_kinds)
    q_ref = refs[0]
    kv_refs = refs[1:1 + 2 * n_piece]
    sink_ref = refs[1 + 2 * n_piece] if has_sink else None
    o_ref = refs[-1]
    first_head = pl.program_id(1) * (kv_step * group)
    blk = pl.program_id(2)
    n_blk = pl.num_programs(2)

    def cat(parts, axis):
        return parts[0] if len(parts) == 1 else jnp.concatenate(parts, axis=axis)

    def piece_bias(kind, n_keys):
        r = lax.broadcasted_iota(jnp.int32, (tq, n_keys), 0)
        c = lax.broadcasted_iota(jnp.int32, (tq, n_keys), 1)
        if kind == "prev":
            ok = (c >= r) & (blk > 0)
        elif kind == "next":
            ok = (c <= r - (tq - A_WINDOW)) & (blk < n_blk - 1)
        elif kind == "band":
            ok = jnp.abs(r - c) <= A_WINDOW
        else:
            return jnp.zeros((tq, n_keys), F32)
        return jnp.where(ok, 0.0, NEG_INF)

    seg_refs, seg_bias = [], []
    p_i = 0
    for kinds in seg_kinds:
        pieces = [(kv_refs[2 * (p_i + n)], kv_refs[2 * (p_i + n) + 1]) for n in range(len(kinds))]
        p_i += len(kinds)
        seg_refs.append(pieces)
        if all(kind == "full" for kind in kinds):
            seg_bias.append(None)
        else:
            bias = cat([piece_bias(kind, kr.shape[0]) for kind, (kr, _) in zip(kinds, pieces)], 1)
            seg_bias.append(cat([bias] * group, 0))

    for hk in range(kv_step):
        heads = range(hk * group, (hk + 1) * group)
        q = cat([q_ref[:, h * dk:(h + 1) * dk] for h in heads], 0)
        segs = []
        for pieces, bias in zip(seg_refs, seg_bias):
            k = cat([kr[:, hk * dk:(hk + 1) * dk].astype(BF16) for kr, _ in pieces], 0)
            v = cat([vr[:, hk * HEAD_DIM:(hk + 1) * HEAD_DIM].astype(BF16) for _, vr in pieces], 0)
            v1 = jnp.concatenate([v, jnp.ones_like(v)], axis=1)
            for c0 in range(0, k.shape[0], KEY_CHUNK):
                chunk = slice(c0, c0 + KEY_CHUNK)
                segs.append((k[chunk], v1[chunk], None if bias is None else bias[:, chunk]))
        sink = (cat([jnp.full((tq, 2 * HEAD_DIM), sink_ref[first_head + h], F32) for h in heads], 0)
                if has_sink else None)
        out = _online_softmax(q, segs, scale, sink)
        for g_i, h in enumerate(heads):
            o_ref[:, h * HEAD_DIM:(h + 1) * HEAD_DIM] = out[g_i * tq:(g_i + 1) * tq].astype(o_ref.dtype)


def _attention(q, segs, *, n_batch, n_tok, tq, n_kv, kv_step, dk, scale, sink=None):
    n_blk = n_tok // tq
    per_blk = tq // A_WINDOW
    n_win = n_tok // A_WINDOW
    group = N_HEADS // n_kv
    in_specs = [pl.BlockSpec((tq, kv_step * group * dk), lambda b, g, i: (b * n_blk + i, g))]
    args = [q]
    for pieces in segs:
        for k, v, rows, kind in pieces:
            if kind == "full":
                idx = lambda b, g, i: (b, g)
            elif kind == "band":
                idx = lambda b, g, i: (b * n_blk + i, g)
            elif kind == "prev":
                idx = lambda b, g, i: (b * n_win + jnp.maximum(i * per_blk - 1, 0), g)
            else:
                idx = lambda b, g, i: (b * n_win + jnp.minimum((i + 1) * per_blk, n_win - 1), g)
            in_specs += [pl.BlockSpec((rows, kv_step * dk), idx), pl.BlockSpec((rows, kv_step * HEAD_DIM), idx)]
            args += [k, v]
    seg_kinds = tuple(tuple(kind for _, _, _, kind in pieces) for pieces in segs)
    if sink is not None:
        in_specs.append(pl.BlockSpec(memory_space=pltpu.SMEM))
        args.append(sink)
    return pl.pallas_call(
        functools.partial(_attn_kernel, kv_step=kv_step, group=group, dk=dk, scale=scale,
                          seg_kinds=seg_kinds, has_sink=sink is not None, tq=tq),
        grid=(n_batch, n_kv // kv_step, n_blk),
        in_specs=in_specs,
        out_specs=pl.BlockSpec((tq, kv_step * group * HEAD_DIM), lambda b, g, i: (b * n_blk + i, g)),
        out_shape=jax.ShapeDtypeStruct((n_batch * n_tok, N_HEADS * HEAD_DIM), BF16),
        compiler_params=_params(3),
        name=f"attn_{n_batch}x{n_tok}_kv{n_kv}_dk{dk}_" + "_".join(k[0] for ks in seg_kinds for k in ks),
    )(*args)


def _na_row_window(r):
    rs = min(max(r - NA_ROWS // 2, 0), GRID_ROWS - NA_ROWS)
    ws = min(rs - rs % 2, GRID_ROWS - NA_WIN_ROWS)
    return ws, r - ws, rs - ws


def _na_kernel(q_ref, k_ref, v_ref, kc_ref, vc_ref, bias_ref, o_ref, s_ref, sc_ref, p_ref, pc_ref, l_ref):
    dims = (((1,), (1,)), ((), ()))

    @pl.when((pl.program_id(0) == 0) & (pl.program_id(1) == 0))
    def _():
        p_ref[...] = jnp.zeros_like(p_ref)

    for hh in range(LANES // HEAD_DIM):
        cols = slice(hh * HEAD_DIM, (hh + 1) * HEAD_DIM)
        q = q_ref[:, cols]
        kc = kc_ref[:, cols].astype(BF16)
        vc = vc_ref[:, cols].astype(BF16)
        s_ref[hh] = lax.dot_general(q, k_ref[:, cols], dims, preferred_element_type=F32)
        sc_ref[hh] = lax.dot_general(q, kc, dims, preferred_element_type=F32)
        for r in range(GRID_ROWS):
            ws, slot, _ = _na_row_window(r)
            rows = slice(r * GRID_W, (r + 1) * GRID_W)
            win = slice(ws * GRID_W, (ws + NA_WIN_ROWS) * GRID_W)
            s_n = s_ref[hh, rows, win] + bias_ref[hh, slot]
            s_c = sc_ref[hh, rows, :]
            m = jnp.maximum(jnp.max(s_n, axis=-1, keepdims=True), jnp.max(s_c, axis=-1, keepdims=True))
            p_n = jnp.exp(s_n - m)
            p_c = jnp.exp(s_c - m)
            l_ref[hh, rows, :] = jnp.sum(p_n, axis=-1, keepdims=True) + jnp.sum(p_c, axis=-1, keepdims=True)
            p_ref[hh, rows, win] = p_n.astype(BF16)
            pc_ref[hh, rows, :] = p_c.astype(BF16)
        out = (jnp.dot(p_ref[hh], v_ref[:, cols], preferred_element_type=F32)
               + jnp.dot(pc_ref[hh], vc, preferred_element_type=F32))
        o_ref[:, cols] = (out * (1.0 / l_ref[hh])).astype(o_ref.dtype)


def _na_bias_table(rpb):
    n_h, n_ri, n_ci = rpb.shape
    period = 2 * GRID_W
    u = jnp.concatenate([rpb[..., NA_COLS - 1:], jnp.zeros((n_h, n_ri, period - n_ci), F32),
                         rpb[..., :NA_COLS - 1]], axis=-1)
    flat = jnp.broadcast_to(u[:, :, None, :], (n_h, n_ri, GRID_W, period)).reshape(n_h, n_ri, -1)
    toep = flat[..., :GRID_W * (period - 1)].reshape(n_h, n_ri, GRID_W, period - 1)[..., :GRID_W]
    qcol = jnp.arange(GRID_W)[:, None]
    kcol = jnp.arange(GRID_W)[None, :]
    cs = jnp.clip(qcol - NA_COLS // 2, 0, GRID_W - NA_COLS)
    valid = (kcol >= cs) & (kcol < cs + NA_COLS)
    toep = jnp.where(valid, toep, NEG_INF)
    first_visible = {slot: first for _, slot, first in map(_na_row_window, range(GRID_ROWS))}
    assert sorted(first_visible) == list(range(NA_SLOTS))
    pad = NA_WIN_ROWS
    toep = jnp.pad(toep, ((0, 0), (pad, pad), (0, 0), (0, 0)), constant_values=NEG_INF)
    slots = []
    for d in range(NA_SLOTS):
        lo = NA_ROWS - 1 - d + pad
        rows = toep[:, lo:lo + NA_WIN_ROWS]
        j = jnp.arange(NA_WIN_ROWS)[None, :, None, None]
        visible = (j >= first_visible[d]) & (j < first_visible[d] + NA_ROWS)
        slots.append(jnp.where(visible, rows, NEG_INF))
    table = jnp.stack(slots, axis=1)
    return table.transpose(0, 1, 3, 2, 4).reshape(n_h, NA_SLOTS, GRID_W, NA_WIN_KEYS)


def _na_attention(q, k, v, kc, vc, bias):
    n_pairs = N_HEADS * HEAD_DIM // LANES
    lat = pl.BlockSpec((DEC_SEQ, LANES), lambda hp, b: (b, hp))
    ctx = pl.BlockSpec((PAST_LEN, LANES), lambda hp, b: (b, hp))
    return pl.pallas_call(
        _na_kernel,
        grid=(n_pairs, DEC_BATCH),
        in_specs=[lat, lat, lat, ctx, ctx,
                  pl.BlockSpec((LANES // HEAD_DIM, NA_SLOTS, GRID_W, NA_WIN_KEYS),
                               lambda hp, b: (hp, 0, 0, 0))],
        out_specs=lat,
        out_shape=jax.ShapeDtypeStruct((DEC_BATCH * DEC_SEQ, N_HEADS * HEAD_DIM), BF16),
        scratch_shapes=[pltpu.VMEM((LANES // HEAD_DIM, DEC_SEQ, DEC_SEQ), F32),
                        pltpu.VMEM((LANES // HEAD_DIM, DEC_SEQ, PAST_LEN), F32),
                        pltpu.VMEM((LANES // HEAD_DIM, DEC_SEQ, DEC_SEQ), BF16),
                        pltpu.VMEM((LANES // HEAD_DIM, DEC_SEQ, PAST_LEN), BF16),
                        pltpu.VMEM((LANES // HEAD_DIM, DEC_SEQ, 1), F32)],
        compiler_params=_params(2),
        name="na_attn",
    )(q, k, v, kc, vc, bias)


def _grid_angles(n, rot_dim):
    pos = jnp.arange(n, dtype=jnp.int32)
    row = (pos // GRID_W).astype(F32)
    col = (pos % GRID_W).astype(F32)
    n_ax = rot_dim // 4
    inv = ROPE_THETA ** (-jnp.arange(n_ax, dtype=F32) / n_ax)
    return jnp.concatenate([row[:, None] * inv, col[:, None] * inv], axis=-1)


def _pair_tables(ang):
    cos = jnp.repeat(jnp.cos(ang), 2, axis=-1)
    sin = jnp.stack([-jnp.sin(ang), jnp.sin(ang)], axis=-1).reshape(ang.shape[0], -1)
    return cos, sin


def _gqa_rope_tables():
    cos, sin = _pair_tables(_grid_angles(DEC_SEQ, HEAD_DIM))
    return jnp.tile(cos, (1, LANES // HEAD_DIM)), jnp.tile(sin, (1, LANES // HEAD_DIM))


def _mla_rope_tables():
    cos, sin = _pair_tables(_grid_angles(DEC_SEQ, C_ROPE))
    pad = LANES - C_QK
    cos = jnp.concatenate([jnp.ones((DEC_SEQ, C_NOPE), F32), cos, jnp.ones((DEC_SEQ, pad), F32)], axis=-1)
    sin = jnp.concatenate([jnp.zeros((DEC_SEQ, C_NOPE), F32), sin, jnp.zeros((DEC_SEQ, pad), F32)], axis=-1)
    return cos, sin


def _mla_weights(w_down, q_lnorm, kv_lnorm, w_uq, w_ukv, q_norm, k_norm):
    pad_head = lambda a: jnp.pad(a, [(0, 0)] * (a.ndim - 1) + [(0, C_HEAD_PAD - a.shape[-1])])
    ukv = w_ukv.reshape(C_KV_RANK, N_HEADS, C_NOPE + C_V)
    return {
        "down": jnp.pad(w_down, ((0, 0), (0, C_DOWN_PAD - w_down.shape[1]))).astype(BF16),
        "q_lnorm": q_lnorm.reshape(1, C_Q_RANK),
        "kv_lnorm": kv_lnorm.reshape(1, C_KV_RANK),
        "uq": pad_head(w_uq.reshape(C_Q_RANK, N_HEADS, C_QK)).reshape(C_Q_RANK, -1).astype(BF16),
        "uk": pad_head(ukv[:, :, :C_NOPE]).reshape(C_KV_RANK, -1).astype(BF16),
        "uv": ukv[:, :, C_NOPE:].reshape(C_KV_RANK, -1).astype(BF16),
        "q_norm": pad_head(q_norm).reshape(1, C_HEAD_PAD),
        "k_norm": pad_head(k_norm).reshape(1, C_HEAD_PAD),
    }


def kernel(x_prompt, x_sample, cache_a_k, cache_a_v, cache_b_k, cache_b_v, cache_c_kv, cache_c_krope, cache_d_k, cache_d_v, c, c_ctx, mod_w, mod_b, norm_ff1, norm_mix, norm_ff2, ff1_w_gu, ff1_w_down, ff2_w_gu, ff2_w_down, a_w_qkv, a_q_norm, a_k_norm, a_sink, a_w_o, b_w_qkv, b_q_norm, b_k_norm, b_w_o, c_w_down, c_q_lnorm, c_kv_lnorm, c_w_uq, c_w_ukv, c_q_norm, c_k_norm, c_w_o, d_w_qkv, d_q_norm, d_k_norm, d_rpb, d_w_o):
    n_p, n_s = BATCH * SEQ, DEC_BATCH * DEC_SEQ
    xp = x_prompt.reshape(n_p, D_MODEL)
    xs = x_sample.reshape(n_s, D_MODEL)
    cond = jnp.concatenate([c_ctx[None], c, jnp.zeros((MOD_ROWS - 1 - DEC_BATCH, D_MODEL), F32)], axis=0)
    modv = _mod_table(cond, mod_w, mod_b)
    s_row = _sample_row(TOKEN_TILE)
    gqa_tabs = _gqa_rope_tables()
    gqa_scale = 1.0
    flat_cache = lambda a: a.reshape(DEC_BATCH * PAST_LEN, -1)
    new = {}

    for i in range(DEPTH):
        kind, j = i % N_MIXERS, i // N_MIXERS
        wgu1, wd1 = ff1_w_gu[i].astype(BF16), ff1_w_down[i].astype(BF16)
        xp = _half_ffn(xp, modv, i, 0, norm_ff1[i], wgu1, wd1, _prompt_row)
        xs = _half_ffn(xs, modv, i, 0, norm_ff1[i], wgu1, wd1, s_row)

        if kind in (0, 1, 3):
            w_qkv, q_norm, k_norm, w_o, n_kv = {
                0: (a_w_qkv, a_q_norm, a_k_norm, a_w_o, GQA_KV_HEADS),
                1: (b_w_qkv, b_q_norm, b_k_norm, b_w_o, GQA_KV_HEADS),
                3: (d_w_qkv, d_q_norm, d_k_norm, d_w_o, N_HEADS)}[kind]
            w_qkv = w_qkv[j].astype(BF16)
            qp, kp, vp = _gqa_project(xp, modv, i, norm_mix[i], w_qkv, q_norm[j], k_norm[j], n_kv,
                                      _prompt_row, None, F32)
            qs, ks, vs = _gqa_project(xs, modv, i, norm_mix[i], w_qkv, q_norm[j], k_norm[j], n_kv,
                                      s_row, gqa_tabs if kind != 3 else None, BF16)
            op = _attention(qp, [[(kp, vp, SEQ, "full")]], n_batch=BATCH, n_tok=SEQ, tq=SEQ, n_kv=n_kv,
                            kv_step=n_kv, dk=HEAD_DIM, scale=gqa_scale, sink=a_sink[j] if kind == 0 else None)
            kv_shape = (BATCH, 1, SEQ, n_kv, HEAD_DIM)
            if kind == 0:
                new["a_k"], new["a_v"] = kp.reshape(kv_shape), vp.reshape(kv_shape)
                kc, vc = flat_cache(cache_a_k[:, j]), flat_cache(cache_a_v[:, j])
                tq = 2 * A_WINDOW
                os_ = _attention(qs, [[(ks, vs, A_WINDOW, "prev"), (ks, vs, tq, "band"),
                                       (ks, vs, A_WINDOW, "next")], [(kc, vc, PAST_LEN, "full")]],
                                 n_batch=DEC_BATCH, n_tok=DEC_SEQ, tq=tq, n_kv=n_kv, kv_step=2, dk=HEAD_DIM,
                                 scale=gqa_scale, sink=a_sink[j])
            elif kind == 1:
                new["b_k"], new["b_v"] = kp.reshape(kv_shape), vp.reshape(kv_shape)
                kc, vc = flat_cache(cache_b_k[:, j]), flat_cache(cache_b_v[:, j])
                os_ = _attention(qs, [[(ks, vs, DEC_SEQ, "full")], [(kc, vc, PAST_LEN, "full")]],
                                 n_batch=DEC_BATCH, n_tok=DEC_SEQ, tq=512, n_kv=n_kv, kv_step=2, dk=HEAD_DIM,
                                 scale=gqa_scale)
            else:
                new["d_k"], new["d_v"] = kp.reshape(kv_shape), vp.reshape(kv_shape)
                kc, vc = flat_cache(cache_d_k[:, j]), flat_cache(cache_d_v[:, j])
                os_ = _na_attention(qs, ks, vs, kc, vc, _na_bias_table(d_rpb[j]))
        else:
            w = _mla_weights(c_w_down[j], c_q_lnorm[j], c_kv_lnorm[j], c_w_uq[j], c_w_ukv[j],
                             c_q_norm[j], c_k_norm[j])
            w_o = c_w_o
            qp, kp, vp, ckv_p, kr_p = _mla_project(xp, modv, i, norm_mix[i], w, _prompt_row, None)
            qs, ks, vs, _, _ = _mla_project(xs, modv, i, norm_mix[i], w, s_row, _mla_rope_tables())
            new["c_kv"] = ckv_p.reshape(BATCH, 1, SEQ, C_KV_RANK)
            new["c_krope"] = kr_p[:, C_NOPE:C_QK].reshape(BATCH, 1, SEQ, C_ROPE)
            kr_cache = jnp.pad(flat_cache(cache_c_krope[:, j]), ((0, 0), (C_NOPE, LANES - C_QK)))
            kc, vc = _mla_expand_cache(flat_cache(cache_c_kv[:, j]), kr_cache, w)
            mla_scale = C_QK ** -0.5
            op = _attention(qp, [[(kp, vp, SEQ, "full")]], n_batch=BATCH, n_tok=SEQ, tq=SEQ,
                            n_kv=N_HEADS, kv_step=N_HEADS, dk=C_HEAD_PAD, scale=mla_scale)
            os_ = _attention(qs, [[(ks, vs, DEC_SEQ, "full")], [(kc, vc, PAST_LEN, "full")]],
                             n_batch=DEC_BATCH, n_tok=DEC_SEQ, tq=DEC_SEQ, n_kv=N_HEADS, kv_step=2, dk=C_HEAD_PAD,
                             scale=mla_scale)

        w_o = w_o[j].astype(BF16)
        wgu2, wd2 = ff2_w_gu[i].astype(BF16), ff2_w_down[i].astype(BF16)
        xp = _half_ffn(xp, modv, i, 6, norm_ff2[i], wgu2, wd2, _prompt_row, mixer_out=(op, w_o))
        xs = _half_ffn(xs, modv, i, 6, norm_ff2[i], wgu2, wd2, s_row, mixer_out=(os_, w_o))

    return (xp.reshape(BATCH, SEQ, D_MODEL), xs.reshape(DEC_BATCH, DEC_SEQ, D_MODEL),
            new["a_k"], new["a_v"], new["b_k"], new["b_v"], new["c_kv"], new["c_krope"],
            new["d_k"], new["d_v"])
```

```python
import functools

import jax
import jax.numpy as jnp
from jax import lax
from jax.experimental import pallas as pl
from jax.experimental.pallas import tpu as pltpu

F32 = jnp.float32
BF16 = jnp.bfloat16

D_MODEL = 1024
BATCH = 16
SEQ = 256
DEPTH = 4
DEC_BATCH = 8
DEC_SEQ = 1024
PAST_LEN = 512
GRID_W = 64
N_MIXERS = 4
RMS_EPS = 1e-6
ROPE_THETA = 10000.0
NEG_INF = -1e30
D_FF = 2816
FFN_RES = 0.5
N_MOD = 9
HEAD_DIM = 64
N_HEADS = 16
GQA_KV_HEADS = 4
A_WINDOW = 128
C_Q_RANK = 384
C_KV_RANK = 256
C_NOPE = 64
C_ROPE = 32
C_V = 64
C_QK = C_NOPE + C_ROPE
NA_ROWS = 8
NA_COLS = 16

LANES = 128
MOD_ROWS = 16
C_DOWN_PAD = 768
C_HEAD_PAD = LANES
VMEM_LIMIT_BYTES = 56 * 1024 * 1024
TOKEN_TILE = 512
FF_CHUNK = 256
KEY_CHUNK = 512
LOG2_E = 1.4426950408889634
GQA_SCALE = HEAD_DIM ** -0.5
GRID_ROWS = DEC_SEQ // GRID_W
NA_WIN_ROWS = NA_ROWS + 2
NA_WIN_KEYS = NA_WIN_ROWS * GRID_W
NA_SLOTS = NA_WIN_ROWS


def _params(n_axes):
    return pltpu.CompilerParams(dimension_semantics=("arbitrary",) * n_axes,
                                vmem_limit_bytes=VMEM_LIMIT_BYTES)


def _resident(shape):
    nd = len(shape)
    return pl.BlockSpec(shape, lambda *_: (0,) * nd, pipeline_mode=pl.Buffered(1))


def _resident_layer(shape, layer):
    return pl.BlockSpec((None,) + shape, lambda *_: (layer,) + (0,) * len(shape), pipeline_mode=pl.Buffered(1))


def _mod_spec(layer, part, row_fn):
    base = (layer * N_MOD + part) * MOD_ROWS
    return pl.BlockSpec((None, 1, D_MODEL), lambda i: (base + row_fn(i), 0, 0))


def _prompt_row(i):
    return 0


def _sample_row(tile):
    per_batch = DEC_SEQ // tile
    return lambda i: 1 + i // per_batch


def _silu(a):
    return a * (1.0 / (1.0 + jnp.exp(-a)))


def _modulate(x, g, shift, scale):
    y = x * lax.rsqrt(jnp.mean(x * x, axis=-1, keepdims=True) + RMS_EPS)
    return (y * g) * (1.0 + scale) + shift


def _swap_pairs(y, even):
    n = y.shape[-1]
    return jnp.where(even, pltpu.roll(y, n - 1, 1), pltpu.roll(y, 1, 1))


def _mod_kernel(c_ref, w_ref, b_ref, o_ref):
    s = _silu(c_ref[...]).astype(BF16)
    o_ref[...] = jnp.dot(s, w_ref[...].astype(BF16), preferred_element_type=F32) + b_ref[...]


def _mod_table(cond, mod_w, mod_b):
    out = pl.pallas_call(
        _mod_kernel,
        grid=(DEPTH, N_MOD),
        in_specs=[
            pl.BlockSpec((MOD_ROWS, D_MODEL), lambda l, j: (0, 0)),
            pl.BlockSpec((None, D_MODEL, D_MODEL), lambda l, j: (l, 0, j)),
            pl.BlockSpec((None, 1, D_MODEL), lambda l, j: (l, 0, j)),
        ],
        out_specs=pl.BlockSpec((None, None, MOD_ROWS, D_MODEL), lambda l, j: (l, j, 0, 0)),
        out_shape=jax.ShapeDtypeStruct((DEPTH, N_MOD, MOD_ROWS, D_MODEL), F32),
        compiler_params=_params(2),
        name="mod_table",
    )(cond, mod_w, mod_b.reshape(DEPTH, 1, N_MOD * D_MODEL))
    return out.reshape(DEPTH * N_MOD * MOD_ROWS, 1, D_MODEL)


def _ffn_kernel(*refs, mixer_out):
    if mixer_out:
        x_ref, attn_ref, gm_ref, wo_ref = refs[:4]
        refs = refs[4:]
        x = x_ref[...] + gm_ref[...] * jnp.dot(attn_ref[...], wo_ref[...], preferred_element_type=F32)
    else:
        x = refs[0][...]
        refs = refs[1:]
    sh_ref, sc_ref, gt_ref, g_ref, wgu_ref, wd_ref, o_ref, act_ref = refs
    h = _modulate(x, g_ref[...], sh_ref[...], sc_ref[...]).astype(BF16)
    for c0 in range(0, D_FF, FF_CHUNK):
        a = jnp.dot(h, wgu_ref[:, c0:c0 + FF_CHUNK], preferred_element_type=F32)
        u = jnp.dot(h, wgu_ref[:, D_FF + c0:D_FF + c0 + FF_CHUNK], preferred_element_type=F32)
        act_ref[:, c0:c0 + FF_CHUNK] = (_silu(a) * u).astype(BF16)
    y = jnp.dot(act_ref[...], wd_ref[...], preferred_element_type=F32)
    o_ref[...] = x + (FFN_RES * gt_ref[...]) * y


def _half_ffn(x, modv, layer, part0, g, wgu, wd, row_fn, mixer_out=None):
    n_tok = x.shape[0]
    tile = pl.BlockSpec((TOKEN_TILE, D_MODEL), lambda i: (i, 0))
    in_specs, args = [tile], [x]
    if mixer_out is not None:
        in_specs += [tile, _mod_spec(layer, 5, row_fn), _resident((D_MODEL, D_MODEL))]
        args += [mixer_out[0], modv, mixer_out[1]]
    in_specs += [_mod_spec(layer, part0, row_fn), _mod_spec(layer, part0 + 1, row_fn),
                 _mod_spec(layer, part0 + 2, row_fn),
                 _resident((1, D_MODEL)), _resident_layer((D_MODEL, 2 * D_FF), layer),
                 _resident_layer((D_FF, D_MODEL), layer)]
    args += [modv, modv, modv, g.reshape(1, D_MODEL), wgu, wd]
    return pl.pallas_call(
        functools.partial(_ffn_kernel, mixer_out=mixer_out is not None),
        grid=(n_tok // TOKEN_TILE,),
        in_specs=in_specs,
        out_specs=tile,
        out_shape=jax.ShapeDtypeStruct((n_tok, D_MODEL), F32),
        scratch_shapes=[pltpu.VMEM((TOKEN_TILE, D_FF), BF16)],
        compiler_params=_params(1),
        name=f"ffn_{n_tok}" + ("_mix" if mixer_out is not None else ""),
    )(*args)


def _head_pair_norm(yp, gain2, lo):
    sq = yp * yp
    s_lo = jnp.sum(jnp.where(lo, sq, 0.0), axis=-1, keepdims=True)
    s_hi = jnp.sum(jnp.where(lo, 0.0, sq), axis=-1, keepdims=True)
    ms = jnp.where(lo, s_lo, s_hi) * (1.0 / HEAD_DIM)
    return (yp * lax.rsqrt(ms + RMS_EPS)) * gain2


def _gqa_proj_kernel(*refs, n_q, n_kv, rope, kv_transposed):
    x_ref, sh_ref, sc_ref, g_ref, w_ref, qn_ref, kn_ref = refs[:7]
    if rope:
        cos_ref, sin_ref = refs[7:9]
        q_ref, k_ref, v_ref = refs[9:]
    else:
        q_ref, k_ref, v_ref = refs[7:]
    h = _modulate(x_ref[...], g_ref[...], sh_ref[...], sc_ref[...]).astype(BF16)
    lane = lax.broadcasted_iota(jnp.int32, (1, LANES), 1)
    lo = lane < HEAD_DIM
    even = (lane & 1) == 0
    q_cols = n_q * HEAD_DIM
    k_cols = n_kv * HEAD_DIM

    def store(out_ref, o0, y, transposed):
        if not transposed:
            out_ref[:, o0:o0 + LANES] = y.astype(out_ref.dtype)
            return
        yt = y.T
        for b_i in range(y.shape[0] // SEQ):
            for h_i in range(LANES // HEAD_DIM):
                out_ref[b_i, o0 // HEAD_DIM + h_i] = yt[h_i * HEAD_DIM:(h_i + 1) * HEAD_DIM,
                                                        b_i * SEQ:(b_i + 1) * SEQ]

    def normed(c0, gain_ref, out_ref, o0, post_scale, transposed):
        y = jnp.dot(h, w_ref[:, c0:c0 + 2 * LANES], preferred_element_type=F32)
        for p in range(2):
            yn = _head_pair_norm(y[:, p * LANES:(p + 1) * LANES], gain_ref[...], lo)
            if rope:
                yn = yn * cos_ref[...] + _swap_pairs(yn, even) * sin_ref[...]
            if post_scale is not None:
                yn = yn * post_scale
            store(out_ref, o0 + p * LANES, yn, transposed)

    for c0 in range(0, q_cols, 2 * LANES):
        normed(c0, qn_ref, q_ref, c0, GQA_SCALE, False)
    for c0 in range(0, k_cols, 2 * LANES):
        normed(q_cols + c0, kn_ref, k_ref, c0, None, kv_transposed)
    for c0 in range(0, k_cols, 2 * LANES):
        v = jnp.dot(h, w_ref[:, q_cols + k_cols + c0:q_cols + k_cols + c0 + 2 * LANES],
                    preferred_element_type=F32)
        for p in range(2):
            store(v_ref, c0 + p * LANES, v[:, p * LANES:(p + 1) * LANES], kv_transposed)


def _gqa_project(x, modv, layer, g_mix, w_qkv, q_norm, k_norm, n_kv, row_fn, rope_tabs, kv_transposed):
    n_tok = x.shape[0]
    q_cols, k_cols = N_HEADS * HEAD_DIM, n_kv * HEAD_DIM
    rope = rope_tabs is not None
    if kv_transposed:
        per_tile = TOKEN_TILE // SEQ
        kv_spec = pl.BlockSpec((per_tile, n_kv, HEAD_DIM, SEQ), lambda i: (i, 0, 0, 0))
        kv_shape = jax.ShapeDtypeStruct((n_tok // SEQ, n_kv, HEAD_DIM, SEQ), F32)
    else:
        kv_spec = pl.BlockSpec((TOKEN_TILE, k_cols), lambda i: (i, 0))
        kv_shape = jax.ShapeDtypeStruct((n_tok, k_cols), BF16)
    tile = lambda w: pl.BlockSpec((TOKEN_TILE, w), lambda i: (i, 0))
    in_specs = [tile(D_MODEL), _mod_spec(layer, 3, row_fn), _mod_spec(layer, 4, row_fn),
                _resident((1, D_MODEL)), _resident((D_MODEL, q_cols + 2 * k_cols)),
                _resident((1, LANES)), _resident((1, LANES))]
    args = [x, modv, modv, g_mix.reshape(1, D_MODEL), w_qkv,
            jnp.tile(q_norm, 2).reshape(1, LANES), jnp.tile(k_norm, 2).reshape(1, LANES)]
    if rope:
        per_batch = DEC_SEQ // TOKEN_TILE
        tab = pl.BlockSpec((TOKEN_TILE, LANES), lambda i: (i % per_batch, 0))
        in_specs += [tab, tab]
        args += list(rope_tabs)
    return pl.pallas_call(
        functools.partial(_gqa_proj_kernel, n_q=N_HEADS, n_kv=n_kv, rope=rope, kv_transposed=kv_transposed),
        grid=(n_tok // TOKEN_TILE,),
        in_specs=in_specs,
        out_specs=[tile(q_cols), kv_spec, kv_spec],
        out_shape=[jax.ShapeDtypeStruct((n_tok, q_cols), BF16), kv_shape, kv_shape],
        compiler_params=_params(1),
        name=f"gqa_proj_{n_tok}_kv{n_kv}",
    )(*args)


def _mla_group_norm(y, gain):
    ms = jnp.sum(y * y, axis=-1, keepdims=True) * (1.0 / C_QK)
    return (y * lax.rsqrt(ms + RMS_EPS)) * gain


def _mla_expand_kv(ckv, kr128, wk_ref, wv_ref, kn_ref, rope_refs, k_ref, v_ref):
    lane = lax.broadcasted_iota(jnp.int32, (1, LANES), 1)
    even = (lane & 1) == 0
    for c0 in range(0, N_HEADS * C_HEAD_PAD, 2 * LANES):
        y = jnp.dot(ckv, wk_ref[:, c0:c0 + 2 * LANES], preferred_element_type=F32)
        for p in range(2):
            kn = _mla_group_norm(y[:, p * LANES:(p + 1) * LANES] + kr128, kn_ref[...])
            if rope_refs is not None:
                kn = kn * rope_refs[0][...] + _swap_pairs(kn, even) * rope_refs[1][...]
            k_ref[:, c0 + p * LANES:c0 + (p + 1) * LANES] = kn.astype(k_ref.dtype)
    for c0 in range(0, N_HEADS * C_V, 2 * LANES):
        v = jnp.dot(ckv, wv_ref[:, c0:c0 + 2 * LANES], preferred_element_type=F32)
        v_ref[:, c0:c0 + 2 * LANES] = v.astype(v_ref.dtype)


def _mla_proj_kernel(*refs, rope):
    (x_ref, sh_ref, sc_ref, g_ref, wd_ref, qln_ref, kvln_ref, wuq_ref, wk_ref, wv_ref,
     qn_ref, kn_ref) = refs[:12]
    if rope:
        rope_refs = refs[12:14]
        q_ref, k_ref, v_ref, ckv_ref, kr_ref = refs[14:]
    else:
        rope_refs = None
        q_ref, k_ref, v_ref, ckv_ref, kr_ref = refs[12:]
    h = _modulate(x_ref[...], g_ref[...], sh_ref[...], sc_ref[...]).astype(BF16)
    y = jnp.dot(h, wd_ref[...], preferred_element_type=F32)

    def row_norm(z, gain):
        return (z * lax.rsqrt(jnp.mean(z * z, axis=-1, keepdims=True) + RMS_EPS)) * gain

    cq = row_norm(y[:, :C_Q_RANK], qln_ref[...]).astype(BF16)
    ckv = row_norm(y[:, C_Q_RANK:C_Q_RANK + C_KV_RANK], kvln_ref[...])
    kr128 = pltpu.roll(y[:, C_Q_RANK + C_KV_RANK:], C_NOPE, 1)
    ckv_ref[...] = ckv
    kr_ref[...] = kr128

    lane = lax.broadcasted_iota(jnp.int32, (1, LANES), 1)
    even = (lane & 1) == 0
    for c0 in range(0, N_HEADS * C_HEAD_PAD, 2 * LANES):
        yq = jnp.dot(cq, wuq_ref[:, c0:c0 + 2 * LANES], preferred_element_type=F32)
        for p in range(2):
            qn = _mla_group_norm(yq[:, p * LANES:(p + 1) * LANES], qn_ref[...])
            if rope:
                qn = qn * rope_refs[0][...] + _swap_pairs(qn, even) * rope_refs[1][...]
            q_ref[:, c0 + p * LANES:c0 + (p + 1) * LANES] = qn.astype(q_ref.dtype)
    _mla_expand_kv(ckv.astype(BF16), kr128, wk_ref, wv_ref, kn_ref, rope_refs, k_ref, v_ref)


def _mla_project(x, modv, layer, g_mix, w, row_fn, rope_tabs):
    n_tok = x.shape[0]
    rope = rope_tabs is not None
    tile = lambda wd: pl.BlockSpec((TOKEN_TILE, wd), lambda i: (i, 0))
    qk_cols = N_HEADS * C_HEAD_PAD
    in_specs = [tile(D_MODEL), _mod_spec(layer, 3, row_fn), _mod_spec(layer, 4, row_fn),
                _resident((1, D_MODEL)), _resident((D_MODEL, C_DOWN_PAD)),
                _resident((1, C_Q_RANK)), _resident((1, C_KV_RANK)),
                _resident((C_Q_RANK, qk_cols)), _resident((C_KV_RANK, qk_cols)),
                _resident((C_KV_RANK, N_HEADS * C_V)),
                _resident((1, LANES)), _resident((1, LANES))]
    args = [x, modv, modv, g_mix.reshape(1, D_MODEL), w["down"], w["q_lnorm"], w["kv_lnorm"],
            w["uq"], w["uk"], w["uv"], w["q_norm"], w["k_norm"]]
    if rope:
        per_batch = DEC_SEQ // TOKEN_TILE
        tab = pl.BlockSpec((TOKEN_TILE, LANES), lambda i: (i % per_batch, 0))
        in_specs += [tab, tab]
        args += list(rope_tabs)
    return pl.pallas_call(
        functools.partial(_mla_proj_kernel, rope=rope),
        grid=(n_tok // TOKEN_TILE,),
        in_specs=in_specs,
        out_specs=[tile(qk_cols), tile(qk_cols), tile(N_HEADS * C_V), tile(C_KV_RANK), tile(LANES)],
        out_shape=[jax.ShapeDtypeStruct((n_tok, qk_cols), BF16),
                   jax.ShapeDtypeStruct((n_tok, qk_cols), BF16),
                   jax.ShapeDtypeStruct((n_tok, N_HEADS * C_V), BF16),
                   jax.ShapeDtypeStruct((n_tok, C_KV_RANK), F32),
                   jax.ShapeDtypeStruct((n_tok, LANES), F32)],
        compiler_params=_params(1),
        name=f"mla_proj_{n_tok}",
    )(*args)


def _mla_cache_kernel(ckv_ref, kr_ref, wk_ref, wv_ref, kn_ref, k_ref, v_ref):
    _mla_expand_kv(ckv_ref[...].astype(BF16), kr_ref[...], wk_ref, wv_ref, kn_ref, None, k_ref, v_ref)


def _mla_expand_cache(ckv, kr128, w):
    n_tok = ckv.shape[0]
    tile = lambda wd: pl.BlockSpec((TOKEN_TILE, wd), lambda i: (i, 0))
    qk_cols = N_HEADS * C_HEAD_PAD
    return pl.pallas_call(
        _mla_cache_kernel,
        grid=(n_tok // TOKEN_TILE,),
        in_specs=[tile(C_KV_RANK), tile(LANES), _resident((C_KV_RANK, qk_cols)),
                  _resident((C_KV_RANK, N_HEADS * C_V)), _resident((1, LANES))],
        out_specs=[tile(qk_cols), tile(N_HEADS * C_V)],
        out_shape=[jax.ShapeDtypeStruct((n_tok, qk_cols), BF16),
                   jax.ShapeDtypeStruct((n_tok, N_HEADS * C_V), BF16)],
        compiler_params=_params(1),
        name="mla_cache_kv",
    )(ckv, kr128, w["uk"], w["uv"], w["k_norm"])


def _online_softmax(q, segs, scale, sink):
    dims = (((1,), (1,)), ((), ()))
    to_exp2 = scale * LOG2_E
    m = None
    acc = None
    for k, v1, bias, transposed in segs:
        if transposed:
            s = jnp.dot(q, k, preferred_element_type=F32)
        else:
            s = lax.dot_general(q, k, dims, preferred_element_type=F32)
        if bias is not None:
            s = s + bias
        row_max = jnp.max(s, axis=-1, keepdims=True)
        m_new = row_max if m is None else jnp.maximum(m, row_max)
        p = jnp.exp2((s - m_new) * to_exp2).astype(BF16)
        if transposed:
            pv = lax.dot_general(p, v1, dims, preferred_element_type=F32)
        else:
            pv = jnp.dot(p, v1, preferred_element_type=F32)
        acc = pv if m is None else acc * jnp.exp2((m - m_new) * to_exp2) + pv
        m = m_new
    if sink is not None:
        lane = lax.broadcasted_iota(jnp.int32, (1, 2 * HEAD_DIM), 1)
        acc = acc + jnp.where(lane >= HEAD_DIM, jnp.exp2((sink - m * scale) * LOG2_E), 0.0)
    denom = pltpu.roll(acc, HEAD_DIM, 1)
    return (acc * (1.0 / denom))[:, :HEAD_DIM]


def _attn_kernel(*refs, kv_step, group, dk, scale, seg_kinds, has_sink, tq):
    n_piece = sum(len(kinds) for kinds in seg_kinds)
    q_ref = refs[0]
    kv_refs = refs[1:1 + 2 * n_piece]
    sink_ref = refs[1 + 2 * n_piece] if has_sink else None
    o_ref = refs[-1]
    first_head = pl.program_id(1) * (kv_step * group)
    blk = pl.program_id(2)
    n_blk = pl.num_programs(2)

    def cat(parts, axis):
        return parts[0] if len(parts) == 1 else jnp.concatenate(parts, axis=axis)

    def piece_bias(kind, n_keys):
        r = lax.broadcasted_iota(jnp.int32, (tq, n_keys), 0)
        c = lax.broadcasted_iota(jnp.int32, (tq, n_keys), 1)
        if kind == "prev":
            ok = (c >= r) & (blk > 0)
        elif kind == "next":
            ok = (c <= r - (tq - A_WINDOW)) & (blk < n_blk - 1)
        elif kind == "band":
            ok = jnp.abs(r - c) <= A_WINDOW
        else:
            return jnp.zeros((tq, n_keys), F32)
        return jnp.where(ok, 0.0, NEG_INF)

    seg_refs, seg_bias = [], []
    p_i = 0
    for kinds in seg_kinds:
        pieces = [(kv_refs[2 * (p_i + n)], kv_refs[2 * (p_i + n) + 1]) for n in range(len(kinds))]
        p_i += len(kinds)
        seg_refs.append(pieces)
        if all(kind in ("full", "fullT") for kind in kinds):
            seg_bias.append(None)
        else:
            bias = cat([piece_bias(kind, kr.shape[0]) for kind, (kr, _) in zip(kinds, pieces)], 1)
            seg_bias.append(cat([bias] * group, 0))

    for hk in range(kv_step):
        heads = range(hk * group, (hk + 1) * group)
        q = cat([q_ref[:, h * dk:(h + 1) * dk] for h in heads], 0)
        segs = []
        for kinds, pieces, bias in zip(seg_kinds, seg_refs, seg_bias):
            if kinds == ("fullT",):
                k = pieces[0][0][hk].astype(BF16)
                v = pieces[0][1][hk].astype(BF16)
                v1 = jnp.concatenate([v, jnp.ones_like(v)], axis=0)
                for c0 in range(0, k.shape[1], KEY_CHUNK):
                    segs.append((k[:, c0:c0 + KEY_CHUNK], v1[:, c0:c0 + KEY_CHUNK], None, True))
                continue
            k = cat([kr[:, hk * dk:(hk + 1) * dk].astype(BF16) for kr, _ in pieces], 0)
            v = cat([vr[:, hk * HEAD_DIM:(hk + 1) * HEAD_DIM].astype(BF16) for _, vr in pieces], 0)
            v1 = jnp.concatenate([v, jnp.ones_like(v)], axis=1)
            for c0 in range(0, k.shape[0], KEY_CHUNK):
                chunk = slice(c0, c0 + KEY_CHUNK)
                segs.append((k[chunk], v1[chunk], None if bias is None else bias[:, chunk], False))
        sink = (cat([jnp.full((tq, 2 * HEAD_DIM), sink_ref[first_head + h], F32) for h in heads], 0)
                if has_sink else None)
        out = _online_softmax(q, segs, scale, sink)
        for g_i, h in enumerate(heads):
            o_ref[:, h * HEAD_DIM:(h + 1) * HEAD_DIM] = out[g_i * tq:(g_i + 1) * tq].astype(o_ref.dtype)


def _attention(q, segs, *, n_batch, n_tok, tq, n_kv, kv_step, dk, scale, sink=None):
    n_blk = n_tok // tq
    per_blk = tq // A_WINDOW
    n_win = n_tok // A_WINDOW
    group = N_HEADS // n_kv
    in_specs = [pl.BlockSpec((tq, kv_step * group * dk), lambda b, g, i: (b * n_blk + i, g))]
    args = [q]
    for pieces in segs:
        for k, v, rows, kind in pieces:
            if kind == "fullT":
                idx = lambda b, g, i: (b, g, 0, 0)
                in_specs += [pl.BlockSpec((None, kv_step, dk, rows), idx),
                             pl.BlockSpec((None, kv_step, HEAD_DIM, rows), idx)]
                args += [k, v]
                continue
            if kind == "full":
                idx = lambda b, g, i: (b, g)
            elif kind == "band":
                idx = lambda b, g, i: (b * n_blk + i, g)
            elif kind == "prev":
                idx = lambda b, g, i: (b * n_win + jnp.maximum(i * per_blk - 1, 0), g)
            else:
                idx = lambda b, g, i: (b * n_win + jnp.minimum((i + 1) * per_blk, n_win - 1), g)
            in_specs += [pl.BlockSpec((rows, kv_step * dk), idx), pl.BlockSpec((rows, kv_step * HEAD_DIM), idx)]
            args += [k, v]
    seg_kinds = tuple(tuple(kind for _, _, _, kind in pieces) for pieces in segs)
    if sink is not None:
        in_specs.append(pl.BlockSpec(memory_space=pltpu.SMEM))
        args.append(sink)
    return pl.pallas_call(
        functools.partial(_attn_kernel, kv_step=kv_step, group=group, dk=dk, scale=scale,
                          seg_kinds=seg_kinds, has_sink=sink is not None, tq=tq),
        grid=(n_batch, n_kv // kv_step, n_blk),
        in_specs=in_specs,
        out_specs=pl.BlockSpec((tq, kv_step * group * HEAD_DIM), lambda b, g, i: (b * n_blk + i, g)),
        out_shape=jax.ShapeDtypeStruct((n_batch * n_tok, N_HEADS * HEAD_DIM), BF16),
        compiler_params=_params(3),
        name=f"attn_{n_batch}x{n_tok}_kv{n_kv}_dk{dk}_" + "_".join(k[0] for ks in seg_kinds for k in ks),
    )(*args)


def _na_row_window(r):
    rs = min(max(r - NA_ROWS // 2, 0), GRID_ROWS - NA_ROWS)
    ws = min(rs - rs % 2, GRID_ROWS - NA_WIN_ROWS)
    return ws, r - ws, rs - ws


def _na_kernel(q_ref, k_ref, v_ref, kc_ref, vc_ref, bias_ref, o_ref, s_ref, sc_ref, p_ref, pc_ref, l_ref):
    dims = (((1,), (1,)), ((), ()))

    @pl.when((pl.program_id(0) == 0) & (pl.program_id(1) == 0))
    def _():
        p_ref[...] = jnp.zeros_like(p_ref)

    for hh in range(LANES // HEAD_DIM):
        cols = slice(hh * HEAD_DIM, (hh + 1) * HEAD_DIM)
        q = q_ref[:, cols]
        s_ref[hh] = lax.dot_general(q, k_ref[:, cols], dims, preferred_element_type=F32)
        sc_ref[hh] = jnp.dot(q, kc_ref[hh].astype(BF16), preferred_element_type=F32)
        for r in range(GRID_ROWS):
            ws, slot, _ = _na_row_window(r)
            rows = slice(r * GRID_W, (r + 1) * GRID_W)
            win = slice(ws * GRID_W, (ws + NA_WIN_ROWS) * GRID_W)
            s_n = s_ref[hh, rows, win] + bias_ref[hh, slot]
            s_c = sc_ref[hh, rows, :]
            m = jnp.maximum(jnp.max(s_n, axis=-1, keepdims=True), jnp.max(s_c, axis=-1, keepdims=True))
            p_n = jnp.exp(s_n - m)
            p_c = jnp.exp(s_c - m)
            l_ref[hh, rows, :] = jnp.sum(p_n, axis=-1, keepdims=True) + jnp.sum(p_c, axis=-1, keepdims=True)
            p_ref[hh, rows, win] = p_n.astype(BF16)
            pc_ref[hh, rows, :] = p_c.astype(BF16)
        out = (jnp.dot(p_ref[hh], v_ref[:, cols], preferred_element_type=F32)
               + lax.dot_general(pc_ref[hh], vc_ref[hh].astype(BF16), dims, preferred_element_type=F32))
        o_ref[:, cols] = (out * (1.0 / l_ref[hh])).astype(o_ref.dtype)


def _na_bias_table(rpb):
    n_h, n_ri, n_ci = rpb.shape
    period = 2 * GRID_W
    u = jnp.concatenate([rpb[..., NA_COLS - 1:], jnp.zeros((n_h, n_ri, period - n_ci), F32),
                         rpb[..., :NA_COLS - 1]], axis=-1)
    flat = jnp.broadcast_to(u[:, :, None, :], (n_h, n_ri, GRID_W, period)).reshape(n_h, n_ri, -1)
    toep = flat[..., :GRID_W * (period - 1)].reshape(n_h, n_ri, GRID_W, period - 1)[..., :GRID_W]
    qcol = jnp.arange(GRID_W)[:, None]
    kcol = jnp.arange(GRID_W)[None, :]
    cs = jnp.clip(qcol - NA_COLS // 2, 0, GRID_W - NA_COLS)
    valid = (kcol >= cs) & (kcol < cs + NA_COLS)
    toep = jnp.where(valid, toep, NEG_INF)
    first_visible = {slot: first for _, slot, first in map(_na_row_window, range(GRID_ROWS))}
    assert sorted(first_visible) == list(range(NA_SLOTS))
    pad = NA_WIN_ROWS
    toep = jnp.pad(toep, ((0, 0), (pad, pad), (0, 0), (0, 0)), constant_values=NEG_INF)
    slots = []
    for d in range(NA_SLOTS):
        lo = NA_ROWS - 1 - d + pad
        rows = toep[:, lo:lo + NA_WIN_ROWS]
        j = jnp.arange(NA_WIN_ROWS)[None, :, None, None]
        visible = (j >= first_visible[d]) & (j < first_visible[d] + NA_ROWS)
        slots.append(jnp.where(visible, rows, NEG_INF))
    table = jnp.stack(slots, axis=1)
    return table.transpose(0, 1, 3, 2, 4).reshape(n_h, NA_SLOTS, GRID_W, NA_WIN_KEYS)


def _na_attention(q, k, v, kc, vc, bias):
    n_pairs = N_HEADS * HEAD_DIM // LANES
    lat = pl.BlockSpec((DEC_SEQ, LANES), lambda hp, b: (b, hp))
    ctx = pl.BlockSpec((None, LANES // HEAD_DIM, HEAD_DIM, PAST_LEN), lambda hp, b: (b, hp, 0, 0))
    return pl.pallas_call(
        _na_kernel,
        grid=(n_pairs, DEC_BATCH),
        in_specs=[lat, lat, lat, ctx, ctx,
                  pl.BlockSpec((LANES // HEAD_DIM, NA_SLOTS, GRID_W, NA_WIN_KEYS),
                               lambda hp, b: (hp, 0, 0, 0))],
        out_specs=lat,
        out_shape=jax.ShapeDtypeStruct((DEC_BATCH * DEC_SEQ, N_HEADS * HEAD_DIM), BF16),
        scratch_shapes=[pltpu.VMEM((LANES // HEAD_DIM, DEC_SEQ, DEC_SEQ), F32),
                        pltpu.VMEM((LANES // HEAD_DIM, DEC_SEQ, PAST_LEN), F32),
                        pltpu.VMEM((LANES // HEAD_DIM, DEC_SEQ, DEC_SEQ), BF16),
                        pltpu.VMEM((LANES // HEAD_DIM, DEC_SEQ, PAST_LEN), BF16),
                        pltpu.VMEM((LANES // HEAD_DIM, DEC_SEQ, 1), F32)],
        compiler_params=_params(2),
        name="na_attn",
    )(q, k, v, kc, vc, bias)


def _grid_angles(n, rot_dim):
    pos = jnp.arange(n, dtype=jnp.int32)
    row = (pos // GRID_W).astype(F32)
    col = (pos % GRID_W).astype(F32)
    n_ax = rot_dim // 4
    inv = ROPE_THETA ** (-jnp.arange(n_ax, dtype=F32) / n_ax)
    return jnp.concatenate([row[:, None] * inv, col[:, None] * inv], axis=-1)


def _pair_tables(ang):
    cos = jnp.repeat(jnp.cos(ang), 2, axis=-1)
    sin = jnp.stack([-jnp.sin(ang), jnp.sin(ang)], axis=-1).reshape(ang.shape[0], -1)
    return cos, sin


def _gqa_rope_tables():
    cos, sin = _pair_tables(_grid_angles(DEC_SEQ, HEAD_DIM))
    return jnp.tile(cos, (1, LANES // HEAD_DIM)), jnp.tile(sin, (1, LANES // HEAD_DIM))


def _mla_rope_tables():
    cos, sin = _pair_tables(_grid_angles(DEC_SEQ, C_ROPE))
    pad = LANES - C_QK
    cos = jnp.concatenate([jnp.ones((DEC_SEQ, C_NOPE), F32), cos, jnp.ones((DEC_SEQ, pad), F32)], axis=-1)
    sin = jnp.concatenate([jnp.zeros((DEC_SEQ, C_NOPE), F32), sin, jnp.zeros((DEC_SEQ, pad), F32)], axis=-1)
    return cos, sin


def _mla_weights(w_down, q_lnorm, kv_lnorm, w_uq, w_ukv, q_norm, k_norm):
    pad_head = lambda a: jnp.pad(a, [(0, 0)] * (a.ndim - 1) + [(0, C_HEAD_PAD - a.shape[-1])])
    ukv = w_ukv.reshape(C_KV_RANK, N_HEADS, C_NOPE + C_V)
    return {
        "down": jnp.pad(w_down, ((0, 0), (0, C_DOWN_PAD - w_down.shape[1]))).astype(BF16),
        "q_lnorm": q_lnorm.reshape(1, C_Q_RANK),
        "kv_lnorm": kv_lnorm.reshape(1, C_KV_RANK),
        "uq": pad_head(w_uq.reshape(C_Q_RANK, N_HEADS, C_QK)).reshape(C_Q_RANK, -1).astype(BF16),
        "uk": pad_head(ukv[:, :, :C_NOPE]).reshape(C_KV_RANK, -1).astype(BF16),
        "uv": ukv[:, :, C_NOPE:].reshape(C_KV_RANK, -1).astype(BF16),
        "q_norm": pad_head(q_norm).reshape(1, C_HEAD_PAD),
        "k_norm": pad_head(k_norm).reshape(1, C_HEAD_PAD),
    }


def kernel(x_prompt, x_sample, cache_a_k, cache_a_v, cache_b_k, cache_b_v, cache_c_kv, cache_c_krope, cache_d_k, cache_d_v, c, c_ctx, mod_w, mod_b, norm_ff1, norm_mix, norm_ff2, ff1_w_gu, ff1_w_down, ff2_w_gu, ff2_w_down, a_w_qkv, a_q_norm, a_k_norm, a_sink, a_w_o, b_w_qkv, b_q_norm, b_k_norm, b_w_o, c_w_down, c_q_lnorm, c_kv_lnorm, c_w_uq, c_w_ukv, c_q_norm, c_k_norm, c_w_o, d_w_qkv, d_q_norm, d_k_norm, d_rpb, d_w_o):
    n_p, n_s = BATCH * SEQ, DEC_BATCH * DEC_SEQ
    xp = x_prompt.reshape(n_p, D_MODEL)
    xs = x_sample.reshape(n_s, D_MODEL)
    cond = jnp.concatenate([c_ctx[None], c, jnp.zeros((MOD_ROWS - 1 - DEC_BATCH, D_MODEL), F32)], axis=0)
    modv = _mod_table(cond, mod_w, mod_b)
    s_row = _sample_row(TOKEN_TILE)
    gqa_tabs = _gqa_rope_tables()
    gqa_scale = 1.0
    flat_cache = lambda a: a.reshape(DEC_BATCH * PAST_LEN, -1)
    keys_last = lambda a: jnp.transpose(a, (0, 2, 3, 1))
    keys_first = lambda a: jnp.transpose(a, (0, 3, 1, 2))[:, None]
    ffn_w = [(ff1_w_gu.astype(BF16), ff1_w_down.astype(BF16)), (ff2_w_gu.astype(BF16), ff2_w_down.astype(BF16))]
    new = {}

    for i in range(DEPTH):
        kind, j = i % N_MIXERS, i // N_MIXERS
        xp = _half_ffn(xp, modv, i, 0, norm_ff1[i], *ffn_w[0], _prompt_row)
        xs = _half_ffn(xs, modv, i, 0, norm_ff1[i], *ffn_w[0], s_row)

        if kind in (0, 1, 3):
            w_qkv, q_norm, k_norm, w_o, n_kv, cache_k, cache_v = {
                0: (a_w_qkv, a_q_norm, a_k_norm, a_w_o, GQA_KV_HEADS, cache_a_k, cache_a_v),
                1: (b_w_qkv, b_q_norm, b_k_norm, b_w_o, GQA_KV_HEADS, cache_b_k, cache_b_v),
                3: (d_w_qkv, d_q_norm, d_k_norm, d_w_o, N_HEADS, cache_d_k, cache_d_v)}[kind]
            w_qkv = w_qkv[j].astype(BF16)
            qp, kp, vp = _gqa_project(xp, modv, i, norm_mix[i], w_qkv, q_norm[j], k_norm[j], n_kv,
                                      _prompt_row, None, True)
            qs, ks, vs = _gqa_project(xs, modv, i, norm_mix[i], w_qkv, q_norm[j], k_norm[j], n_kv,
                                      s_row, gqa_tabs if kind != 3 else None, False)
            op = _attention(qp, [[(kp, vp, SEQ, "fullT")]], n_batch=BATCH, n_tok=SEQ, tq=SEQ, n_kv=n_kv,
                            kv_step=n_kv, dk=HEAD_DIM, scale=gqa_scale, sink=a_sink[j] if kind == 0 else None)
            name = "abcd"[kind]
            new[name + "_k"], new[name + "_v"] = keys_first(kp), keys_first(vp)
            kc, vc = keys_last(cache_k[:, j]), keys_last(cache_v[:, j])
            if kind == 0:
                tq = 2 * A_WINDOW
                os_ = _attention(qs, [[(ks, vs, A_WINDOW, "prev"), (ks, vs, tq, "band"),
                                       (ks, vs, A_WINDOW, "next")], [(kc, vc, PAST_LEN, "fullT")]],
                                 n_batch=DEC_BATCH, n_tok=DEC_SEQ, tq=tq, n_kv=n_kv, kv_step=2, dk=HEAD_DIM,
                                 scale=gqa_scale, sink=a_sink[j])
            elif kind == 1:
                os_ = _attention(qs, [[(ks, vs, DEC_SEQ, "full")], [(kc, vc, PAST_LEN, "fullT")]],
                                 n_batch=DEC_BATCH, n_tok=DEC_SEQ, tq=512, n_kv=n_kv, kv_step=2, dk=HEAD_DIM,
                                 scale=gqa_scale)
            else:
                os_ = _na_attention(qs, ks, vs, kc, vc, _na_bias_table(d_rpb[j]))
        else:
            w = _mla_weights(c_w_down[j], c_q_lnorm[j], c_kv_lnorm[j], c_w_uq[j], c_w_ukv[j],
                             c_q_norm[j], c_k_norm[j])
            w_o = c_w_o
            qp, kp, vp, ckv_p, kr_p = _mla_project(xp, modv, i, norm_mix[i], w, _prompt_row, None)
            qs, ks, vs, _, _ = _mla_project(xs, modv, i, norm_mix[i], w, s_row, _mla_rope_tables())
            new["c_kv"] = ckv_p.reshape(BATCH, 1, SEQ, C_KV_RANK)
            new["c_krope"] = kr_p[:, C_NOPE:C_QK].reshape(BATCH, 1, SEQ, C_ROPE)
            kr_cache = jnp.pad(flat_cache(cache_c_krope[:, j]), ((0, 0), (C_NOPE, LANES - C_QK)))
            kc, vc = _mla_expand_cache(flat_cache(cache_c_kv[:, j]), kr_cache, w)
            mla_scale = C_QK ** -0.5
            op = _attention(qp, [[(kp, vp, SEQ, "full")]], n_batch=BATCH, n_tok=SEQ, tq=SEQ,
                            n_kv=N_HEADS, kv_step=N_HEADS, dk=C_HEAD_PAD, scale=mla_scale)
            os_ = _attention(qs, [[(ks, vs, DEC_SEQ, "full")], [(kc, vc, PAST_LEN, "full")]],
                             n_batch=DEC_BATCH, n_tok=DEC_SEQ, tq=DEC_SEQ, n_kv=N_HEADS, kv_step=2, dk=C_HEAD_PAD,
                             scale=mla_scale)

        w_o = w_o[j].astype(BF16)
        xp = _half_ffn(xp, modv, i, 6, norm_ff2[i], *ffn_w[1], _prompt_row, mixer_out=(op, w_o))
        xs = _half_ffn(xs, modv, i, 6, norm_ff2[i], *ffn_w[1], s_row, mixer_out=(os_, w_o))

    return (xp.reshape(BATCH, SEQ, D_MODEL), xs.reshape(DEC_BATCH, DEC_SEQ, D_MODEL),
            new["a_k"], new["a_v"], new["b_k"], new["b_v"], new["c_kv"], new["c_krope"],
            new["d_k"], new["d_v"])
```

```python
import functools

import jax
import jax.numpy as jnp
from jax import lax
from jax.experimental import pallas as pl
from jax.experimental.pallas import tpu as pltpu

F32 = jnp.float32
BF16 = jnp.bfloat16

D_MODEL = 1024
BATCH = 16
SEQ = 256
DEPTH = 4
DEC_BATCH = 8
DEC_SEQ = 1024
PAST_LEN = 512
GRID_W = 64
N_MIXERS = 4
RMS_EPS = 1e-6
ROPE_THETA = 10000.0
NEG_INF = -1e30
D_FF = 2816
FFN_RES = 0.5
N_MOD = 9
HEAD_DIM = 64
N_HEADS = 16
GQA_KV_HEADS = 4
A_WINDOW = 128
C_Q_RANK = 384
C_KV_RANK = 256
C_NOPE = 64
C_ROPE = 32
C_V = 64
C_QK = C_NOPE + C_ROPE
NA_ROWS = 8
NA_COLS = 16

LANES = 128
MOD_ROWS = 16
C_DOWN_PAD = 768
C_HEAD_PAD = LANES
VMEM_LIMIT_BYTES = 56 * 1024 * 1024
TOKEN_TILE = 512
FF_CHUNK = 256
KEY_CHUNK = 512
LOG2_E = 1.4426950408889634
GQA_SCALE = HEAD_DIM ** -0.5
GRID_ROWS = DEC_SEQ // GRID_W
NA_WIN_ROWS = NA_ROWS + 2
NA_WIN_KEYS = NA_WIN_ROWS * GRID_W
NA_SLOTS = NA_WIN_ROWS


def _params(n_axes):
    return pltpu.CompilerParams(dimension_semantics=("arbitrary",) * n_axes,
                                vmem_limit_bytes=VMEM_LIMIT_BYTES)


def _resident(shape):
    nd = len(shape)
    return pl.BlockSpec(shape, lambda *_: (0,) * nd, pipeline_mode=pl.Buffered(1))


def _resident_layer(shape, layer):
    return pl.BlockSpec((None,) + shape, lambda *_: (layer,) + (0,) * len(shape), pipeline_mode=pl.Buffered(1))


def _mod_spec(layer, part, row_fn):
    base = (layer * N_MOD + part) * MOD_ROWS
    return pl.BlockSpec((None, 1, D_MODEL), lambda i: (base + row_fn(i), 0, 0))


def _prompt_row(i):
    return 0


def _sample_row(tile):
    per_batch = DEC_SEQ // tile
    return lambda i: 1 + i // per_batch


def _silu(a):
    return a * (1.0 / (1.0 + jnp.exp(-a)))


def _modulate(x, g, shift, scale):
    y = x * lax.rsqrt(jnp.mean(x * x, axis=-1, keepdims=True) + RMS_EPS)
    return (y * g) * (1.0 + scale) + shift


def _swap_pair_columns(a):
    return a.reshape(a.shape[:-1] + (a.shape[-1] // 2, 2))[..., ::-1].reshape(a.shape)


def _mod_kernel(c_ref, w_ref, b_ref, o_ref):
    s = _silu(c_ref[...]).astype(BF16)
    o_ref[...] = jnp.dot(s, w_ref[...].astype(BF16), preferred_element_type=F32) + b_ref[...]


def _mod_table(cond, mod_w, mod_b):
    out = pl.pallas_call(
        _mod_kernel,
        grid=(DEPTH, N_MOD),
        in_specs=[
            pl.BlockSpec((MOD_ROWS, D_MODEL), lambda l, j: (0, 0)),
            pl.BlockSpec((None, D_MODEL, D_MODEL), lambda l, j: (l, 0, j)),
            pl.BlockSpec((None, 1, D_MODEL), lambda l, j: (l, 0, j)),
        ],
        out_specs=pl.BlockSpec((None, None, MOD_ROWS, D_MODEL), lambda l, j: (l, j, 0, 0)),
        out_shape=jax.ShapeDtypeStruct((DEPTH, N_MOD, MOD_ROWS, D_MODEL), F32),
        compiler_params=_params(2),
        name="mod_table",
    )(cond, mod_w, mod_b.reshape(DEPTH, 1, N_MOD * D_MODEL))
    return out.reshape(DEPTH * N_MOD * MOD_ROWS, 1, D_MODEL)


def _ffn_kernel(*refs, mixer_out):
    if mixer_out:
        x_ref, attn_ref, gm_ref, wo_ref = refs[:4]
        refs = refs[4:]
        x = x_ref[...] + gm_ref[...] * jnp.dot(attn_ref[...], wo_ref[...], preferred_element_type=F32)
    else:
        x = refs[0][...]
        refs = refs[1:]
    sh_ref, sc_ref, gt_ref, g_ref, wgu_ref, wd_ref, o_ref, act_ref = refs
    h = _modulate(x, g_ref[...], sh_ref[...], sc_ref[...]).astype(BF16)
    for c0 in range(0, D_FF, FF_CHUNK):
        a = jnp.dot(h, wgu_ref[:, c0:c0 + FF_CHUNK], preferred_element_type=F32)
        u = jnp.dot(h, wgu_ref[:, D_FF + c0:D_FF + c0 + FF_CHUNK], preferred_element_type=F32)
        act_ref[:, c0:c0 + FF_CHUNK] = (_silu(a) * u).astype(BF16)
    y = jnp.dot(act_ref[...], wd_ref[...], preferred_element_type=F32)
    o_ref[...] = x + (FFN_RES * gt_ref[...]) * y


def _half_ffn(x, modv, layer, part0, g, wgu, wd, row_fn, mixer_out=None):
    n_tok = x.shape[0]
    tile = pl.BlockSpec((TOKEN_TILE, D_MODEL), lambda i: (i, 0))
    in_specs, args = [tile], [x]
    if mixer_out is not None:
        in_specs += [tile, _mod_spec(layer, 5, row_fn), _resident((D_MODEL, D_MODEL))]
        args += [mixer_out[0], modv, mixer_out[1]]
    in_specs += [_mod_spec(layer, part0, row_fn), _mod_spec(layer, part0 + 1, row_fn),
                 _mod_spec(layer, part0 + 2, row_fn),
                 _resident((1, D_MODEL)), _resident_layer((D_MODEL, 2 * D_FF), layer),
                 _resident_layer((D_FF, D_MODEL), layer)]
    args += [modv, modv, modv, g.reshape(1, D_MODEL), wgu, wd]
    return pl.pallas_call(
        functools.partial(_ffn_kernel, mixer_out=mixer_out is not None),
        grid=(n_tok // TOKEN_TILE,),
        in_specs=in_specs,
        out_specs=tile,
        out_shape=jax.ShapeDtypeStruct((n_tok, D_MODEL), F32),
        scratch_shapes=[pltpu.VMEM((TOKEN_TILE, D_FF), BF16)],
        compiler_params=_params(1),
        name=f"ffn_{n_tok}" + ("_mix" if mixer_out is not None else ""),
    )(*args)


def _head_pair_rstd(yp, lo):
    sq = yp * yp
    s_lo = jnp.sum(jnp.where(lo, sq, 0.0), axis=-1, keepdims=True)
    s_hi = jnp.sum(jnp.where(lo, 0.0, sq), axis=-1, keepdims=True)
    ms = jnp.where(lo, s_lo, s_hi) * (1.0 / HEAD_DIM)
    return lax.rsqrt(ms + RMS_EPS)


def _gqa_proj_kernel(*refs, n_q, n_kv, rope, kv_transposed):
    x_ref, sh_ref, sc_ref, g_ref, w_ref, qn_ref, kn_ref = refs[:7]
    if rope:
        wsw_ref, qc_ref, qs_ref, kc_ref, ks_ref = refs[7:12]
        q_ref, k_ref, v_ref = refs[12:]
    else:
        q_ref, k_ref, v_ref = refs[7:]
    h = _modulate(x_ref[...], g_ref[...], sh_ref[...], sc_ref[...]).astype(BF16)
    lane = lax.broadcasted_iota(jnp.int32, (1, LANES), 1)
    lo = lane < HEAD_DIM
    q_cols = n_q * HEAD_DIM
    k_cols = n_kv * HEAD_DIM

    def store(out_ref, o0, y, transposed):
        if not transposed:
            out_ref[:, o0:o0 + LANES] = y.astype(out_ref.dtype)
            return
        yt = y.T
        for b_i in range(y.shape[0] // SEQ):
            for h_i in range(LANES // HEAD_DIM):
                out_ref[b_i, o0 // HEAD_DIM + h_i] = yt[h_i * HEAD_DIM:(h_i + 1) * HEAD_DIM,
                                                        b_i * SEQ:(b_i + 1) * SEQ]

    def normed(c0, gain_ref, tabs, out_ref, o0, post_scale, transposed):
        y = jnp.dot(h, w_ref[:, c0:c0 + 2 * LANES], preferred_element_type=F32)
        if rope:
            y_sw = jnp.dot(h, wsw_ref[:, c0:c0 + 2 * LANES], preferred_element_type=F32)
        for p in range(2):
            part = slice(p * LANES, (p + 1) * LANES)
            rstd = _head_pair_rstd(y[:, part], lo)
            if rope:
                yn = (y[:, part] * tabs[0][...] + y_sw[:, part] * tabs[1][...]) * rstd
            else:
                yn = (y[:, part] * rstd) * gain_ref[...]
            if post_scale is not None:
                yn = yn * post_scale
            store(out_ref, o0 + p * LANES, yn, transposed)

    for c0 in range(0, q_cols, 2 * LANES):
        normed(c0, qn_ref, (qc_ref, qs_ref) if rope else None, q_ref, c0, GQA_SCALE, False)
    for c0 in range(0, k_cols, 2 * LANES):
        normed(q_cols + c0, kn_ref, (kc_ref, ks_ref) if rope else None, k_ref, c0, None, kv_transposed)
    for c0 in range(0, k_cols, 2 * LANES):
        v = jnp.dot(h, w_ref[:, q_cols + k_cols + c0:q_cols + k_cols + c0 + 2 * LANES],
                    preferred_element_type=F32)
        for p in range(2):
            store(v_ref, c0 + p * LANES, v[:, p * LANES:(p + 1) * LANES], kv_transposed)


def _gqa_project(x, modv, layer, g_mix, w_qkv, q_norm, k_norm, n_kv, row_fn, rope_tabs, kv_transposed):
    n_tok = x.shape[0]
    q_cols, k_cols = N_HEADS * HEAD_DIM, n_kv * HEAD_DIM
    rope = rope_tabs is not None
    if kv_transposed:
        per_tile = TOKEN_TILE // SEQ
        kv_spec = pl.BlockSpec((per_tile, n_kv, HEAD_DIM, SEQ), lambda i: (i, 0, 0, 0))
        kv_shape = jax.ShapeDtypeStruct((n_tok // SEQ, n_kv, HEAD_DIM, SEQ), F32)
    else:
        kv_spec = pl.BlockSpec((TOKEN_TILE, k_cols), lambda i: (i, 0))
        kv_shape = jax.ShapeDtypeStruct((n_tok, k_cols), BF16)
    tile = lambda w: pl.BlockSpec((TOKEN_TILE, w), lambda i: (i, 0))
    in_specs = [tile(D_MODEL), _mod_spec(layer, 3, row_fn), _mod_spec(layer, 4, row_fn),
                _resident((1, D_MODEL)), _resident((D_MODEL, q_cols + 2 * k_cols)),
                _resident((1, LANES)), _resident((1, LANES))]
    pair_tile = lambda g: jnp.tile(g, LANES // HEAD_DIM).reshape(1, LANES)
    args = [x, modv, modv, g_mix.reshape(1, D_MODEL), w_qkv, pair_tile(q_norm), pair_tile(k_norm)]
    if rope:
        per_batch = DEC_SEQ // TOKEN_TILE
        tab = pl.BlockSpec((TOKEN_TILE, LANES), lambda i: (i % per_batch, 0))
        cos, sin = rope_tabs
        in_specs += [_resident((D_MODEL, q_cols + k_cols)), tab, tab, tab, tab]
        args += [_swap_pair_columns(w_qkv[:, :q_cols + k_cols]),
                 cos * pair_tile(q_norm), sin * pair_tile(_swap_pair_columns(q_norm)),
                 cos * pair_tile(k_norm), sin * pair_tile(_swap_pair_columns(k_norm))]
    return pl.pallas_call(
        functools.partial(_gqa_proj_kernel, n_q=N_HEADS, n_kv=n_kv, rope=rope, kv_transposed=kv_transposed),
        grid=(n_tok // TOKEN_TILE,),
        in_specs=in_specs,
        out_specs=[tile(q_cols), kv_spec, kv_spec],
        out_shape=[jax.ShapeDtypeStruct((n_tok, q_cols), BF16), kv_shape, kv_shape],
        compiler_params=_params(1),
        name=f"gqa_proj_{n_tok}_kv{n_kv}",
    )(*args)


def _mla_group_rstd(y):
    ms = jnp.sum(y * y, axis=-1, keepdims=True) * (1.0 / C_QK)
    return lax.rsqrt(ms + RMS_EPS)


def _mla_expand_kv(ckv, kr128, wk_ref, wv_ref, kn_ref, rope, k_ref, v_ref):
    for c0 in range(0, N_HEADS * C_HEAD_PAD, 2 * LANES):
        y = jnp.dot(ckv, wk_ref[:, c0:c0 + 2 * LANES], preferred_element_type=F32)
        for p in range(2):
            kp = y[:, p * LANES:(p + 1) * LANES] + kr128
            rstd = _mla_group_rstd(kp)
            if rope is None:
                kn = (kp * rstd) * kn_ref[...]
            else:
                kn = (kp * rope[0][...] + rope[2] * rope[1][...]) * rstd
            k_ref[:, c0 + p * LANES:c0 + (p + 1) * LANES] = kn.astype(k_ref.dtype)
    for c0 in range(0, N_HEADS * C_V, 2 * LANES):
        v = jnp.dot(ckv, wv_ref[:, c0:c0 + 2 * LANES], preferred_element_type=F32)
        v_ref[:, c0:c0 + 2 * LANES] = v.astype(v_ref.dtype)


def _mla_proj_kernel(*refs, rope):
    (x_ref, sh_ref, sc_ref, g_ref, wd_ref, qln_ref, kvln_ref, wuq_ref, wk_ref, wv_ref,
     qn_ref, kn_ref) = refs[:12]
    if rope:
        wuq_sw_ref, qc_ref, qs_ref, kc_ref, ks_ref = refs[12:17]
        q_ref, k_ref, v_ref, ckv_ref, kr_ref = refs[17:]
    else:
        q_ref, k_ref, v_ref, ckv_ref, kr_ref = refs[12:]
    h = _modulate(x_ref[...], g_ref[...], sh_ref[...], sc_ref[...]).astype(BF16)
    y = jnp.dot(h, wd_ref[...], preferred_element_type=F32)

    def row_norm(z, gain):
        return (z * lax.rsqrt(jnp.mean(z * z, axis=-1, keepdims=True) + RMS_EPS)) * gain

    cq = row_norm(y[:, :C_Q_RANK], qln_ref[...]).astype(BF16)
    ckv = row_norm(y[:, C_Q_RANK:C_Q_RANK + C_KV_RANK], kvln_ref[...])
    kr128 = pltpu.roll(y[:, C_Q_RANK + C_KV_RANK:C_DOWN_PAD], C_NOPE, 1)
    ckv_ref[...] = ckv
    kr_ref[...] = kr128

    for c0 in range(0, N_HEADS * C_HEAD_PAD, 2 * LANES):
        yq = jnp.dot(cq, wuq_ref[:, c0:c0 + 2 * LANES], preferred_element_type=F32)
        if rope:
            yq_sw = jnp.dot(cq, wuq_sw_ref[:, c0:c0 + 2 * LANES], preferred_element_type=F32)
        for p in range(2):
            part = slice(p * LANES, (p + 1) * LANES)
            rstd = _mla_group_rstd(yq[:, part])
            if rope:
                qn = (yq[:, part] * qc_ref[...] + yq_sw[:, part] * qs_ref[...]) * rstd
            else:
                qn = (yq[:, part] * rstd) * qn_ref[...]
            q_ref[:, c0 + p * LANES:c0 + (p + 1) * LANES] = qn.astype(q_ref.dtype)
    k_rope = (kc_ref, ks_ref, pltpu.roll(y[:, C_DOWN_PAD:], C_NOPE, 1)) if rope else None
    _mla_expand_kv(ckv.astype(BF16), kr128, wk_ref, wv_ref, kn_ref, k_rope, k_ref, v_ref)


def _mla_project(x, modv, layer, g_mix, w, row_fn, rope_tabs):
    n_tok = x.shape[0]
    rope = rope_tabs is not None
    tile = lambda wd: pl.BlockSpec((TOKEN_TILE, wd), lambda i: (i, 0))
    qk_cols = N_HEADS * C_HEAD_PAD
    w_down = w["down_rope"] if rope else w["down"]
    in_specs = [tile(D_MODEL), _mod_spec(layer, 3, row_fn), _mod_spec(layer, 4, row_fn),
                _resident((1, D_MODEL)), _resident(w_down.shape),
                _resident((1, C_Q_RANK)), _resident((1, C_KV_RANK)),
                _resident((C_Q_RANK, qk_cols)), _resident((C_KV_RANK, qk_cols)),
                _resident((C_KV_RANK, N_HEADS * C_V)),
                _resident((1, LANES)), _resident((1, LANES))]
    args = [x, modv, modv, g_mix.reshape(1, D_MODEL), w_down, w["q_lnorm"], w["kv_lnorm"],
            w["uq"], w["uk"], w["uv"], w["q_norm"], w["k_norm"]]
    if rope:
        per_batch = DEC_SEQ // TOKEN_TILE
        tab = pl.BlockSpec((TOKEN_TILE, LANES), lambda i: (i % per_batch, 0))
        cos, sin = rope_tabs
        in_specs += [_resident((C_Q_RANK, qk_cols)), tab, tab, tab, tab]
        args += [w["uq_sw"], cos * w["q_norm"], sin * _swap_pair_columns(w["q_norm"]),
                 cos * w["k_norm"], sin * _swap_pair_columns(w["k_norm"])]
    return pl.pallas_call(
        functools.partial(_mla_proj_kernel, rope=rope),
        grid=(n_tok // TOKEN_TILE,),
        in_specs=in_specs,
        out_specs=[tile(qk_cols), tile(qk_cols), tile(N_HEADS * C_V), tile(C_KV_RANK), tile(LANES)],
        out_shape=[jax.ShapeDtypeStruct((n_tok, qk_cols), BF16),
                   jax.ShapeDtypeStruct((n_tok, qk_cols), BF16),
                   jax.ShapeDtypeStruct((n_tok, N_HEADS * C_V), BF16),
                   jax.ShapeDtypeStruct((n_tok, C_KV_RANK), F32),
                   jax.ShapeDtypeStruct((n_tok, LANES), F32)],
        compiler_params=_params(1),
        name=f"mla_proj_{n_tok}",
    )(*args)


def _mla_cache_kernel(ckv_ref, kr_ref, wk_ref, wv_ref, kn_ref, k_ref, v_ref):
    _mla_expand_kv(ckv_ref[...].astype(BF16), kr_ref[...], wk_ref, wv_ref, kn_ref, None, k_ref, v_ref)


def _mla_expand_cache(ckv, kr128, w):
    n_tok = ckv.shape[0]
    tile = lambda wd: pl.BlockSpec((TOKEN_TILE, wd), lambda i: (i, 0))
    qk_cols = N_HEADS * C_HEAD_PAD
    return pl.pallas_call(
        _mla_cache_kernel,
        grid=(n_tok // TOKEN_TILE,),
        in_specs=[tile(C_KV_RANK), tile(LANES), _resident((C_KV_RANK, qk_cols)),
                  _resident((C_KV_RANK, N_HEADS * C_V)), _resident((1, LANES))],
        out_specs=[tile(qk_cols), tile(N_HEADS * C_V)],
        out_shape=[jax.ShapeDtypeStruct((n_tok, qk_cols), BF16),
                   jax.ShapeDtypeStruct((n_tok, N_HEADS * C_V), BF16)],
        compiler_params=_params(1),
        name="mla_cache_kv",
    )(ckv, kr128, w["uk"], w["uv"], w["k_norm"])


def _online_softmax(q, segs, scale, sink):
    dims = (((1,), (1,)), ((), ()))
    to_exp2 = scale * LOG2_E
    m = None
    acc = None
    for k, v1, bias, transposed in segs:
        if transposed:
            s = jnp.dot(q, k, preferred_element_type=F32)
        else:
            s = lax.dot_general(q, k, dims, preferred_element_type=F32)
        if bias is not None:
            s = s + bias
        row_max = jnp.max(s, axis=-1, keepdims=True)
        m_new = row_max if m is None else jnp.maximum(m, row_max)
        p = jnp.exp2((s - m_new) * to_exp2).astype(BF16)
        if transposed:
            pv = lax.dot_general(p, v1, dims, preferred_element_type=F32)
        else:
            pv = jnp.dot(p, v1, preferred_element_type=F32)
        acc = pv if m is None else acc * jnp.exp2((m - m_new) * to_exp2) + pv
        m = m_new
    if sink is not None:
        lane = lax.broadcasted_iota(jnp.int32, (1, 2 * HEAD_DIM), 1)
        acc = acc + jnp.where(lane >= HEAD_DIM, jnp.exp2((sink - m * scale) * LOG2_E), 0.0)
    denom = pltpu.roll(acc, HEAD_DIM, 1)
    return (acc * (1.0 / denom))[:, :HEAD_DIM]


def _attn_kernel(*refs, kv_step, group, dk, scale, seg_kinds, has_sink, tq):
    n_piece = sum(len(kinds) for kinds in seg_kinds)
    q_ref = refs[0]
    kv_refs = refs[1:1 + 2 * n_piece]
    sink_ref = refs[1 + 2 * n_piece] if has_sink else None
    o_ref = refs[-1]
    first_head = pl.program_id(1) * (kv_step * group)
    blk = pl.program_id(2)
    n_blk = pl.num_programs(2)

    def cat(parts, axis):
        return parts[0] if len(parts) == 1 else jnp.concatenate(parts, axis=axis)

    def piece_bias(kind, n_keys):
        r = lax.broadcasted_iota(jnp.int32, (tq, n_keys), 0)
        c = lax.broadcasted_iota(jnp.int32, (tq, n_keys), 1)
        if kind == "prev":
            ok = (c >= r) & (blk > 0)
        elif kind == "next":
            ok = (c <= r - (tq - A_WINDOW)) & (blk < n_blk - 1)
        elif kind == "band":
            ok = jnp.abs(r - c) <= A_WINDOW
        else:
            return jnp.zeros((tq, n_keys), F32)
        return jnp.where(ok, 0.0, NEG_INF)

    seg_refs, seg_bias = [], []
    p_i = 0
    for kinds in seg_kinds:
        pieces = [(kv_refs[2 * (p_i + n)], kv_refs[2 * (p_i + n) + 1]) for n in range(len(kinds))]
        p_i += len(kinds)
        seg_refs.append(pieces)
        if all(kind in ("full", "fullT") for kind in kinds):
            seg_bias.append(None)
        else:
            bias = cat([piece_bias(kind, kr.shape[0]) for kind, (kr, _) in zip(kinds, pieces)], 1)
            seg_bias.append(cat([bias] * group, 0))

    for hk in range(kv_step):
        heads = range(hk * group, (hk + 1) * group)
        q = cat([q_ref[:, h * dk:(h + 1) * dk] for h in heads], 0)
        segs = []
        for kinds, pieces, bias in zip(seg_kinds, seg_refs, seg_bias):
            if kinds == ("fullT",):
                k = pieces[0][0][hk].astype(BF16)
                v = pieces[0][1][hk].astype(BF16)
                v1 = jnp.concatenate([v, jnp.ones_like(v)], axis=0)
                for c0 in range(0, k.shape[1], KEY_CHUNK):
                    segs.append((k[:, c0:c0 + KEY_CHUNK], v1[:, c0:c0 + KEY_CHUNK], None, True))
                continue
            k = cat([kr[:, hk * dk:(hk + 1) * dk].astype(BF16) for kr, _ in pieces], 0)
            v = cat([vr[:, hk * HEAD_DIM:(hk + 1) * HEAD_DIM].astype(BF16) for _, vr in pieces], 0)
            v1 = jnp.concatenate([v, jnp.ones_like(v)], axis=1)
            for c0 in range(0, k.shape[0], KEY_CHUNK):
                chunk = slice(c0, c0 + KEY_CHUNK)
                segs.append((k[chunk], v1[chunk], None if bias is None else bias[:, chunk], False))
        sink = (cat([jnp.full((tq, 2 * HEAD_DIM), sink_ref[first_head + h], F32) for h in heads], 0)
                if has_sink else None)
        out = _online_softmax(q, segs, scale, sink)
        for g_i, h in enumerate(heads):
            o_ref[:, h * HEAD_DIM:(h + 1) * HEAD_DIM] = out[g_i * tq:(g_i + 1) * tq].astype(o_ref.dtype)


def _attention(q, segs, *, n_batch, n_tok, tq, n_kv, kv_step, dk, scale, sink=None):
    n_blk = n_tok // tq
    per_blk = tq // A_WINDOW
    n_win = n_tok // A_WINDOW
    group = N_HEADS // n_kv
    in_specs = [pl.BlockSpec((tq, kv_step * group * dk), lambda b, g, i: (b * n_blk + i, g))]
    args = [q]
    for pieces in segs:
        for k, v, rows, kind in pieces:
            if kind == "fullT":
                idx = lambda b, g, i: (b, g, 0, 0)
                in_specs += [pl.BlockSpec((None, kv_step, dk, rows), idx),
                             pl.BlockSpec((None, kv_step, HEAD_DIM, rows), idx)]
                args += [k, v]
                continue
            if kind == "full":
                idx = lambda b, g, i: (b, g)
            elif kind == "band":
                idx = lambda b, g, i: (b * n_blk + i, g)
            elif kind == "prev":
                idx = lambda b, g, i: (b * n_win + jnp.maximum(i * per_blk - 1, 0), g)
            else:
                idx = lambda b, g, i: (b * n_win + jnp.minimum((i + 1) * per_blk, n_win - 1), g)
            in_specs += [pl.BlockSpec((rows, kv_step * dk), idx), pl.BlockSpec((rows, kv_step * HEAD_DIM), idx)]
            args += [k, v]
    seg_kinds = tuple(tuple(kind for _, _, _, kind in pieces) for pieces in segs)
    if sink is not None:
        in_specs.append(pl.BlockSpec(memory_space=pltpu.SMEM))
        args.append(sink)
    return pl.pallas_call(
        functools.partial(_attn_kernel, kv_step=kv_step, group=group, dk=dk, scale=scale,
                          seg_kinds=seg_kinds, has_sink=sink is not None, tq=tq),
        grid=(n_batch, n_kv // kv_step, n_blk),
        in_specs=in_specs,
        out_specs=pl.BlockSpec((tq, kv_step * group * HEAD_DIM), lambda b, g, i: (b * n_blk + i, g)),
        out_shape=jax.ShapeDtypeStruct((n_batch * n_tok, N_HEADS * HEAD_DIM), BF16),
        compiler_params=_params(3),
        name=f"attn_{n_batch}x{n_tok}_kv{n_kv}_dk{dk}_" + "_".join(k[0] for ks in seg_kinds for k in ks),
    )(*args)


def _na_row_window(r):
    rs = min(max(r - NA_ROWS // 2, 0), GRID_ROWS - NA_ROWS)
    ws = min(rs - rs % 2, GRID_ROWS - NA_WIN_ROWS)
    return ws, r - ws, rs - ws


def _na_kernel(q_ref, k_ref, v_ref, kc_ref, vc_ref, rows_ref, o_ref,
               s_ref, sc_ref, p_ref, pc_ref, l_ref, bias_ref):
    dims = (((1,), (1,)), ((), ()))

    @pl.when((pl.program_id(0) == 0) & (pl.program_id(1) == 0))
    def _():
        p_ref[...] = jnp.zeros_like(p_ref)

    @pl.when(pl.program_id(1) == 0)
    def _():
        _na_build_bias(rows_ref, bias_ref)

    for hh in range(LANES // HEAD_DIM):
        cols = slice(hh * HEAD_DIM, (hh + 1) * HEAD_DIM)
        q = q_ref[:, cols]
        s_ref[hh] = lax.dot_general(q, k_ref[:, cols], dims, preferred_element_type=F32)
        sc_ref[hh] = jnp.dot(q, kc_ref[hh].astype(BF16), preferred_element_type=F32)
        for r in range(GRID_ROWS):
            ws, slot, _ = _na_row_window(r)
            rows = slice(r * GRID_W, (r + 1) * GRID_W)
            win = slice(ws * GRID_W, (ws + NA_WIN_ROWS) * GRID_W)
            s_n = s_ref[hh, rows, win] + bias_ref[hh, slot]
            s_c = sc_ref[hh, rows, :]
            m = jnp.maximum(jnp.max(s_n, axis=-1, keepdims=True), jnp.max(s_c, axis=-1, keepdims=True))
            p_n = jnp.exp(s_n - m)
            p_c = jnp.exp(s_c - m)
            l_ref[hh, rows, :] = jnp.sum(p_n, axis=-1, keepdims=True) + jnp.sum(p_c, axis=-1, keepdims=True)
            p_ref[hh, rows, win] = p_n.astype(BF16)
            pc_ref[hh, rows, :] = p_c.astype(BF16)
        out = (jnp.dot(p_ref[hh], v_ref[:, cols], preferred_element_type=F32)
               + lax.dot_general(pc_ref[hh], vc_ref[hh].astype(BF16), dims, preferred_element_type=F32))
        o_ref[:, cols] = (out * (1.0 / l_ref[hh])).astype(o_ref.dtype)


def _na_build_bias(rows_ref, bias_ref):
    first_visible = {slot: first for _, slot, first in map(_na_row_window, range(GRID_ROWS))}
    assert sorted(first_visible) == list(range(NA_SLOTS))
    qcol = lax.broadcasted_iota(jnp.int32, (GRID_W, GRID_W), 0)
    kcol = lax.broadcasted_iota(jnp.int32, (GRID_W, GRID_W), 1)
    cs = jnp.clip(qcol - NA_COLS // 2, 0, GRID_W - NA_COLS)
    valid = (kcol >= cs) & (kcol < cs + NA_COLS)
    masked = jnp.full((GRID_W, GRID_W), NEG_INF, F32)
    for hh in range(LANES // HEAD_DIM):
        toep = []
        for ri in range(2 * NA_ROWS - 1):
            rows = jnp.broadcast_to(rows_ref[hh, ri:ri + 1, :], (GRID_W, LANES))
            rolled = pltpu.roll(rows, 0, 1, stride=1, stride_axis=0)
            toep.append(jnp.where(valid, rolled[:, :GRID_W], NEG_INF))
        for d in range(NA_SLOTS):
            for j0 in range(0, NA_WIN_ROWS, LANES // GRID_W):
                pieces = []
                for j in range(j0, j0 + LANES // GRID_W):
                    visible = first_visible[d] <= j < first_visible[d] + NA_ROWS
                    pieces.append(toep[j - d + NA_ROWS - 1] if visible else masked)
                bias_ref[hh, d, :, j0 * GRID_W:j0 * GRID_W + LANES] = jnp.concatenate(pieces, axis=1)


def _na_bias_rows(rpb):
    n_h, n_ri, n_ci = rpb.shape
    return jnp.concatenate([rpb[..., NA_COLS - 1:], jnp.zeros((n_h, n_ri, LANES - n_ci), F32),
                            rpb[..., :NA_COLS - 1]], axis=-1)


def _na_attention(q, k, v, kc, vc, rpb):
    n_pairs = N_HEADS * HEAD_DIM // LANES
    lat = pl.BlockSpec((DEC_SEQ, LANES), lambda hp, b: (b, hp))
    ctx = pl.BlockSpec((None, LANES // HEAD_DIM, HEAD_DIM, PAST_LEN), lambda hp, b: (b, hp, 0, 0))
    return pl.pallas_call(
        _na_kernel,
        grid=(n_pairs, DEC_BATCH),
        in_specs=[lat, lat, lat, ctx, ctx,
                  pl.BlockSpec((LANES // HEAD_DIM, 2 * NA_ROWS - 1, LANES), lambda hp, b: (hp, 0, 0))],
        out_specs=lat,
        out_shape=jax.ShapeDtypeStruct((DEC_BATCH * DEC_SEQ, N_HEADS * HEAD_DIM), BF16),
        scratch_shapes=[pltpu.VMEM((LANES // HEAD_DIM, DEC_SEQ, DEC_SEQ), F32),
                        pltpu.VMEM((LANES // HEAD_DIM, DEC_SEQ, PAST_LEN), F32),
                        pltpu.VMEM((LANES // HEAD_DIM, DEC_SEQ, DEC_SEQ), BF16),
                        pltpu.VMEM((LANES // HEAD_DIM, DEC_SEQ, PAST_LEN), BF16),
                        pltpu.VMEM((LANES // HEAD_DIM, DEC_SEQ, 1), F32),
                        pltpu.VMEM((LANES // HEAD_DIM, NA_SLOTS, GRID_W, NA_WIN_KEYS), F32)],
        compiler_params=_params(2),
        name="na_attn",
    )(q, k, v, kc, vc, _na_bias_rows(rpb))


def _grid_angles(n, rot_dim):
    pos = jnp.arange(n, dtype=jnp.int32)
    row = (pos // GRID_W).astype(F32)
    col = (pos % GRID_W).astype(F32)
    n_ax = rot_dim // 4
    inv = ROPE_THETA ** (-jnp.arange(n_ax, dtype=F32) / n_ax)
    return jnp.concatenate([row[:, None] * inv, col[:, None] * inv], axis=-1)


def _pair_tables(ang):
    cos = jnp.repeat(jnp.cos(ang), 2, axis=-1)
    sin = jnp.stack([-jnp.sin(ang), jnp.sin(ang)], axis=-1).reshape(ang.shape[0], -1)
    return cos, sin


def _gqa_rope_tables():
    cos, sin = _pair_tables(_grid_angles(DEC_SEQ, HEAD_DIM))
    return jnp.tile(cos, (1, LANES // HEAD_DIM)), jnp.tile(sin, (1, LANES // HEAD_DIM))


def _mla_rope_tables():
    cos, sin = _pair_tables(_grid_angles(DEC_SEQ, C_ROPE))
    pad = LANES - C_QK
    cos = jnp.concatenate([jnp.ones((DEC_SEQ, C_NOPE), F32), cos, jnp.ones((DEC_SEQ, pad), F32)], axis=-1)
    sin = jnp.concatenate([jnp.zeros((DEC_SEQ, C_NOPE), F32), sin, jnp.zeros((DEC_SEQ, pad), F32)], axis=-1)
    return cos, sin


def _mla_weights(w_down, q_lnorm, kv_lnorm, w_uq, w_ukv, q_norm, k_norm):
    pad_head = lambda a: jnp.pad(a, [(0, 0)] * (a.ndim - 1) + [(0, C_HEAD_PAD - a.shape[-1])])
    ukv = w_ukv.reshape(C_KV_RANK, N_HEADS, C_NOPE + C_V)
    down = jnp.pad(w_down, ((0, 0), (0, C_DOWN_PAD - w_down.shape[1]))).astype(BF16)
    uq = pad_head(w_uq.reshape(C_Q_RANK, N_HEADS, C_QK)).reshape(C_Q_RANK, -1).astype(BF16)
    return {
        "down": down,
        "down_rope": jnp.concatenate([down, _swap_pair_columns(down[:, C_DOWN_PAD - LANES:])], axis=1),
        "q_lnorm": q_lnorm.reshape(1, C_Q_RANK),
        "kv_lnorm": kv_lnorm.reshape(1, C_KV_RANK),
        "uq": uq,
        "uq_sw": _swap_pair_columns(uq),
        "uk": pad_head(ukv[:, :, :C_NOPE]).reshape(C_KV_RANK, -1).astype(BF16),
        "uv": ukv[:, :, C_NOPE:].reshape(C_KV_RANK, -1).astype(BF16),
        "q_norm": pad_head(q_norm).reshape(1, C_HEAD_PAD),
        "k_norm": pad_head(k_norm).reshape(1, C_HEAD_PAD),
    }


def kernel(x_prompt, x_sample, cache_a_k, cache_a_v, cache_b_k, cache_b_v, cache_c_kv, cache_c_krope, cache_d_k, cache_d_v, c, c_ctx, mod_w, mod_b, norm_ff1, norm_mix, norm_ff2, ff1_w_gu, ff1_w_down, ff2_w_gu, ff2_w_down, a_w_qkv, a_q_norm, a_k_norm, a_sink, a_w_o, b_w_qkv, b_q_norm, b_k_norm, b_w_o, c_w_down, c_q_lnorm, c_kv_lnorm, c_w_uq, c_w_ukv, c_q_norm, c_k_norm, c_w_o, d_w_qkv, d_q_norm, d_k_norm, d_rpb, d_w_o):
    n_p, n_s = BATCH * SEQ, DEC_BATCH * DEC_SEQ
    xp = x_prompt.reshape(n_p, D_MODEL)
    xs = x_sample.reshape(n_s, D_MODEL)
    cond = jnp.concatenate([c_ctx[None], c, jnp.zeros((MOD_ROWS - 1 - DEC_BATCH, D_MODEL), F32)], axis=0)
    modv = _mod_table(cond, mod_w, mod_b)
    s_row = _sample_row(TOKEN_TILE)
    gqa_tabs = _gqa_rope_tables()
    gqa_scale = 1.0
    flat_cache = lambda a: a.reshape(DEC_BATCH * PAST_LEN, -1)
    keys_last = lambda a: jnp.transpose(a, (0, 2, 3, 1))
    keys_first = lambda a: jnp.transpose(a, (0, 3, 1, 2))[:, None]
    ffn_w = [(ff1_w_gu.astype(BF16), ff1_w_down.astype(BF16)), (ff2_w_gu.astype(BF16), ff2_w_down.astype(BF16))]
    new = {}

    for i in range(DEPTH):
        kind, j = i % N_MIXERS, i // N_MIXERS
        xp = _half_ffn(xp, modv, i, 0, norm_ff1[i], *ffn_w[0], _prompt_row)
        xs = _half_ffn(xs, modv, i, 0, norm_ff1[i], *ffn_w[0], s_row)

        if kind in (0, 1, 3):
            w_qkv, q_norm, k_norm, w_o, n_kv, cache_k, cache_v = {
                0: (a_w_qkv, a_q_norm, a_k_norm, a_w_o, GQA_KV_HEADS, cache_a_k, cache_a_v),
                1: (b_w_qkv, b_q_norm, b_k_norm, b_w_o, GQA_KV_HEADS, cache_b_k, cache_b_v),
                3: (d_w_qkv, d_q_norm, d_k_norm, d_w_o, N_HEADS, cache_d_k, cache_d_v)}[kind]
            w_qkv = w_qkv[j].astype(BF16)
            qp, kp, vp = _gqa_project(xp, modv, i, norm_mix[i], w_qkv, q_norm[j], k_norm[j], n_kv,
                                      _prompt_row, None, True)
            qs, ks, vs = _gqa_project(xs, modv, i, norm_mix[i], w_qkv, q_norm[j], k_norm[j], n_kv,
                                      s_row, gqa_tabs if kind != 3 else None, False)
            op = _attention(qp, [[(kp, vp, SEQ, "fullT")]], n_batch=BATCH, n_tok=SEQ, tq=SEQ, n_kv=n_kv,
                            kv_step=n_kv, dk=HEAD_DIM, scale=gqa_scale, sink=a_sink[j] if kind == 0 else None)
            name = "abcd"[kind]
            new[name + "_k"], new[name + "_v"] = keys_first(kp), keys_first(vp)
            kc, vc = keys_last(cache_k[:, j]), keys_last(cache_v[:, j])
            if kind == 0:
                tq = 2 * A_WINDOW
                os_ = _attention(qs, [[(ks, vs, A_WINDOW, "prev"), (ks, vs, tq, "band"),
                                       (ks, vs, A_WINDOW, "next")], [(kc, vc, PAST_LEN, "fullT")]],
                                 n_batch=DEC_BATCH, n_tok=DEC_SEQ, tq=tq, n_kv=n_kv, kv_step=2, dk=HEAD_DIM,
                                 scale=gqa_scale, sink=a_sink[j])
            elif kind == 1:
                os_ = _attention(qs, [[(ks, vs, DEC_SEQ, "full")], [(kc, vc, PAST_LEN, "fullT")]],
                                 n_batch=DEC_BATCH, n_tok=DEC_SEQ, tq=512, n_kv=n_kv, kv_step=2, dk=HEAD_DIM,
                                 scale=gqa_scale)
            else:
                os_ = _na_attention(qs, ks, vs, kc, vc, d_rpb[j])
        else:
            w = _mla_weights(c_w_down[j], c_q_lnorm[j], c_kv_lnorm[j], c_w_uq[j], c_w_ukv[j],
                             c_q_norm[j], c_k_norm[j])
            w_o = c_w_o
            qp, kp, vp, ckv_p, kr_p = _mla_project(xp, modv, i, norm_mix[i], w, _prompt_row, None)
            qs, ks, vs, _, _ = _mla_project(xs, modv, i, norm_mix[i], w, s_row, _mla_rope_tables())
            new["c_kv"] = ckv_p.reshape(BATCH, 1, SEQ, C_KV_RANK)
            new["c_krope"] = kr_p[:, C_NOPE:C_QK].reshape(BATCH, 1, SEQ, C_ROPE)
            kr_cache = jnp.pad(flat_cache(cache_c_krope[:, j]), ((0, 0), (C_NOPE, LANES - C_QK)))
            kc, vc = _mla_expand_cache(flat_cache(cache_c_kv[:, j]), kr_cache, w)
            mla_scale = C_QK ** -0.5
            op = _attention(qp, [[(kp, vp, SEQ, "full")]], n_batch=BATCH, n_tok=SEQ, tq=SEQ,
                            n_kv=N_HEADS, kv_step=N_HEADS, dk=C_HEAD_PAD, scale=mla_scale)
            os_ = _attention(qs, [[(ks, vs, DEC_SEQ, "full")], [(kc, vc, PAST_LEN, "full")]],
                             n_batch=DEC_BATCH, n_tok=DEC_SEQ, tq=DEC_SEQ, n_kv=N_HEADS, kv_step=2, dk=C_HEAD_PAD,
                             scale=mla_scale)

        w_o = w_o[j].astype(BF16)
        xp = _half_ffn(xp, modv, i, 6, norm_ff2[i], *ffn_w[1], _prompt_row, mixer_out=(op, w_o))
        xs = _half_ffn(xs, modv, i, 6, norm_ff2[i], *ffn_w[1], s_row, mixer_out=(os_, w_o))

    return (xp.reshape(BATCH, SEQ, D_MODEL), xs.reshape(DEC_BATCH, DEC_SEQ, D_MODEL),
            new["a_k"], new["a_v"], new["b_k"], new["b_v"], new["c_kv"], new["c_krope"],
            new["d_k"], new["d_v"])
```

```python
import functools

import jax
import jax.numpy as jnp
from jax import lax
from jax.experimental import pallas as pl
from jax.experimental.pallas import tpu as pltpu

F32 = jnp.float32
BF16 = jnp.bfloat16

D_MODEL = 1024
BATCH = 16
SEQ = 256
DEPTH = 4
DEC_BATCH = 8
DEC_SEQ = 1024
PAST_LEN = 512
GRID_W = 64
N_MIXERS = 4
RMS_EPS = 1e-6
ROPE_THETA = 10000.0
NEG_INF = -1e30
D_FF = 2816
FFN_RES = 0.5
N_MOD = 9
HEAD_DIM = 64
N_HEADS = 16
GQA_KV_HEADS = 4
A_WINDOW = 128
C_Q_RANK = 384
C_KV_RANK = 256
C_NOPE = 64
C_ROPE = 32
C_V = 64
C_QK = C_NOPE + C_ROPE
NA_ROWS = 8
NA_COLS = 16

LANES = 128
MOD_ROWS = 16
C_DOWN_PAD = 768
C_HEAD_PAD = LANES
VMEM_LIMIT_BYTES = 56 * 1024 * 1024
TOKEN_TILE = 512
FF_CHUNK = 256
KEY_CHUNK = 512
LOG2_E = 1.4426950408889634
GQA_SCALE = HEAD_DIM ** -0.5
GRID_ROWS = DEC_SEQ // GRID_W
NA_WIN_ROWS = NA_ROWS + 2
NA_WIN_KEYS = NA_WIN_ROWS * GRID_W
NA_SLOTS = NA_WIN_ROWS


def _params(n_axes):
    return pltpu.CompilerParams(dimension_semantics=("arbitrary",) * n_axes,
                                vmem_limit_bytes=VMEM_LIMIT_BYTES)


def _resident(shape):
    nd = len(shape)
    return pl.BlockSpec(shape, lambda *_: (0,) * nd, pipeline_mode=pl.Buffered(1))


def _resident_layer(shape, layer):
    return pl.BlockSpec((None,) + shape, lambda *_: (layer,) + (0,) * len(shape), pipeline_mode=pl.Buffered(1))


def _mod_spec(layer, part, row_fn):
    base = (layer * N_MOD + part) * MOD_ROWS
    return pl.BlockSpec((None, 1, D_MODEL), lambda i: (base + row_fn(i), 0, 0))


def _prompt_row(i):
    return 0


def _sample_row(tile):
    per_batch = DEC_SEQ // tile
    return lambda i: 1 + i // per_batch


def _silu(a):
    return a * (1.0 / (1.0 + jnp.exp(-a)))


def _modulate(x, g, shift, scale):
    y = x * lax.rsqrt(jnp.mean(x * x, axis=-1, keepdims=True) + RMS_EPS)
    return (y * g) * (1.0 + scale) + shift


def _swap_pair_columns(a):
    even = jnp.arange(a.shape[-1]) % 2 == 0
    return jnp.where(even, jnp.roll(a, -1, axis=-1), jnp.roll(a, 1, axis=-1))


def _mod_kernel(c_ref, w_ref, b_ref, o_ref):
    s = _silu(c_ref[...]).astype(BF16)
    o_ref[...] = jnp.dot(s, w_ref[...].astype(BF16), preferred_element_type=F32) + b_ref[...]


def _mod_table(cond, mod_w, mod_b):
    out = pl.pallas_call(
        _mod_kernel,
        grid=(DEPTH, N_MOD),
        in_specs=[
            pl.BlockSpec((MOD_ROWS, D_MODEL), lambda l, j: (0, 0)),
            pl.BlockSpec((None, D_MODEL, D_MODEL), lambda l, j: (l, 0, j)),
            pl.BlockSpec((None, 1, D_MODEL), lambda l, j: (l, 0, j)),
        ],
        out_specs=pl.BlockSpec((None, None, MOD_ROWS, D_MODEL), lambda l, j: (l, j, 0, 0)),
        out_shape=jax.ShapeDtypeStruct((DEPTH, N_MOD, MOD_ROWS, D_MODEL), F32),
        compiler_params=_params(2),
        name="mod_table",
    )(cond, mod_w, mod_b.reshape(DEPTH, 1, N_MOD * D_MODEL))
    return out.reshape(DEPTH * N_MOD * MOD_ROWS, 1, D_MODEL)


def _ffn_kernel(*refs, mixer_out):
    if mixer_out:
        x_ref, attn_ref, gm_ref, wo_ref = refs[:4]
        refs = refs[4:]
        x = x_ref[...] + gm_ref[...] * jnp.dot(attn_ref[...], wo_ref[...], preferred_element_type=F32)
    else:
        x = refs[0][...]
        refs = refs[1:]
    sh_ref, sc_ref, gt_ref, g_ref, wgu_ref, wd_ref, o_ref, act_ref = refs
    h = _modulate(x, g_ref[...], sh_ref[...], sc_ref[...]).astype(BF16)
    for c0 in range(0, D_FF, FF_CHUNK):
        a = jnp.dot(h, wgu_ref[:, c0:c0 + FF_CHUNK], preferred_element_type=F32)
        u = jnp.dot(h, wgu_ref[:, D_FF + c0:D_FF + c0 + FF_CHUNK], preferred_element_type=F32)
        act_ref[:, c0:c0 + FF_CHUNK] = (_silu(a) * u).astype(BF16)
    y = jnp.dot(act_ref[...], wd_ref[...], preferred_element_type=F32)
    o_ref[...] = x + (FFN_RES * gt_ref[...]) * y


def _half_ffn(x, modv, layer, part0, g, wgu, wd, row_fn, mixer_out=None):
    n_tok = x.shape[0]
    tile = pl.BlockSpec((TOKEN_TILE, D_MODEL), lambda i: (i, 0))
    in_specs, args = [tile], [x]
    if mixer_out is not None:
        in_specs += [tile, _mod_spec(layer, 5, row_fn), _resident((D_MODEL, D_MODEL))]
        args += [mixer_out[0], modv, mixer_out[1]]
    in_specs += [_mod_spec(layer, part0, row_fn), _mod_spec(layer, part0 + 1, row_fn),
                 _mod_spec(layer, part0 + 2, row_fn),
                 _resident((1, D_MODEL)), _resident_layer((D_MODEL, 2 * D_FF), layer),
                 _resident_layer((D_FF, D_MODEL), layer)]
    args += [modv, modv, modv, g.reshape(1, D_MODEL), wgu, wd]
    return pl.pallas_call(
        functools.partial(_ffn_kernel, mixer_out=mixer_out is not None),
        grid=(n_tok // TOKEN_TILE,),
        in_specs=in_specs,
        out_specs=tile,
        out_shape=jax.ShapeDtypeStruct((n_tok, D_MODEL), F32),
        scratch_shapes=[pltpu.VMEM((TOKEN_TILE, D_FF), BF16)],
        compiler_params=_params(1),
        name=f"ffn_{n_tok}" + ("_mix" if mixer_out is not None else ""),
    )(*args)


def _head_pair_rstd(yp, lo):
    sq = yp * yp
    s_lo = jnp.sum(jnp.where(lo, sq, 0.0), axis=-1, keepdims=True)
    s_hi = jnp.sum(jnp.where(lo, 0.0, sq), axis=-1, keepdims=True)
    ms = jnp.where(lo, s_lo, s_hi) * (1.0 / HEAD_DIM)
    return lax.rsqrt(ms + RMS_EPS)


def _gqa_proj_kernel(*refs, n_q, n_kv, rope, kv_transposed):
    x_ref, sh_ref, sc_ref, g_ref, w_ref, qn_ref, kn_ref = refs[:7]
    if rope:
        wsw_ref, qc_ref, qs_ref, kc_ref, ks_ref = refs[7:12]
        q_ref, k_ref, v_ref = refs[12:]
    else:
        q_ref, k_ref, v_ref = refs[7:]
    h = _modulate(x_ref[...], g_ref[...], sh_ref[...], sc_ref[...]).astype(BF16)
    lane = lax.broadcasted_iota(jnp.int32, (1, LANES), 1)
    lo = lane < HEAD_DIM
    q_cols = n_q * HEAD_DIM
    k_cols = n_kv * HEAD_DIM

    def store(out_ref, o0, y, transposed):
        if not transposed:
            out_ref[:, o0:o0 + LANES] = y.astype(out_ref.dtype)
            return
        yt = y.T
        for b_i in range(y.shape[0] // SEQ):
            for h_i in range(LANES // HEAD_DIM):
                out_ref[b_i, o0 // HEAD_DIM + h_i] = yt[h_i * HEAD_DIM:(h_i + 1) * HEAD_DIM,
                                                        b_i * SEQ:(b_i + 1) * SEQ]

    def normed(c0, gain_ref, tabs, out_ref, o0, post_scale, transposed):
        y = jnp.dot(h, w_ref[:, c0:c0 + 2 * LANES], preferred_element_type=F32)
        if rope:
            y_sw = jnp.dot(h, wsw_ref[:, c0:c0 + 2 * LANES], preferred_element_type=F32)
        for p in range(2):
            part = slice(p * LANES, (p + 1) * LANES)
            rstd = _head_pair_rstd(y[:, part], lo)
            if rope:
                yn = (y[:, part] * tabs[0][...] + y_sw[:, part] * tabs[1][...]) * rstd
            else:
                yn = (y[:, part] * rstd) * gain_ref[...]
            if post_scale is not None:
                yn = yn * post_scale
            store(out_ref, o0 + p * LANES, yn, transposed)

    for c0 in range(0, q_cols, 2 * LANES):
        normed(c0, qn_ref, (qc_ref, qs_ref) if rope else None, q_ref, c0, GQA_SCALE, False)
    for c0 in range(0, k_cols, 2 * LANES):
        normed(q_cols + c0, kn_ref, (kc_ref, ks_ref) if rope else None, k_ref, c0, None, kv_transposed)
    for c0 in range(0, k_cols, 2 * LANES):
        v = jnp.dot(h, w_ref[:, q_cols + k_cols + c0:q_cols + k_cols + c0 + 2 * LANES],
                    preferred_element_type=F32)
        for p in range(2):
            store(v_ref, c0 + p * LANES, v[:, p * LANES:(p + 1) * LANES], kv_transposed)


def _gqa_project(x, modv, layer, g_mix, w_qkv, q_norm, k_norm, n_kv, row_fn, rope_tabs, kv_transposed):
    n_tok = x.shape[0]
    q_cols, k_cols = N_HEADS * HEAD_DIM, n_kv * HEAD_DIM
    rope = rope_tabs is not None
    if kv_transposed:
        per_tile = TOKEN_TILE // SEQ
        kv_spec = pl.BlockSpec((per_tile, n_kv, HEAD_DIM, SEQ), lambda i: (i, 0, 0, 0))
        kv_shape = jax.ShapeDtypeStruct((n_tok // SEQ, n_kv, HEAD_DIM, SEQ), F32)
    else:
        kv_spec = pl.BlockSpec((TOKEN_TILE, k_cols), lambda i: (i, 0))
        kv_shape = jax.ShapeDtypeStruct((n_tok, k_cols), BF16)
    tile = lambda w: pl.BlockSpec((TOKEN_TILE, w), lambda i: (i, 0))
    in_specs = [tile(D_MODEL), _mod_spec(layer, 3, row_fn), _mod_spec(layer, 4, row_fn),
                _resident((1, D_MODEL)), _resident((D_MODEL, q_cols + 2 * k_cols)),
                _resident((1, LANES)), _resident((1, LANES))]
    pair_tile = lambda g: jnp.tile(g, LANES // HEAD_DIM).reshape(1, LANES)
    args = [x, modv, modv, g_mix.reshape(1, D_MODEL), w_qkv, pair_tile(q_norm), pair_tile(k_norm)]
    if rope:
        per_batch = DEC_SEQ // TOKEN_TILE
        tab = pl.BlockSpec((TOKEN_TILE, LANES), lambda i: (i % per_batch, 0))
        cos, sin = rope_tabs
        in_specs += [_resident((D_MODEL, q_cols + k_cols)), tab, tab, tab, tab]
        args += [_swap_pair_columns(w_qkv[:, :q_cols + k_cols]),
                 cos * pair_tile(q_norm), sin * pair_tile(_swap_pair_columns(q_norm)),
                 cos * pair_tile(k_norm), sin * pair_tile(_swap_pair_columns(k_norm))]
    return pl.pallas_call(
        functools.partial(_gqa_proj_kernel, n_q=N_HEADS, n_kv=n_kv, rope=rope, kv_transposed=kv_transposed),
        grid=(n_tok // TOKEN_TILE,),
        in_specs=in_specs,
        out_specs=[tile(q_cols), kv_spec, kv_spec],
        out_shape=[jax.ShapeDtypeStruct((n_tok, q_cols), BF16), kv_shape, kv_shape],
        compiler_params=_params(1),
        name=f"gqa_proj_{n_tok}_kv{n_kv}",
    )(*args)


def _mla_group_rstd(y):
    ms = jnp.sum(y * y, axis=-1, keepdims=True) * (1.0 / C_QK)
    return lax.rsqrt(ms + RMS_EPS)


def _mla_expand_kv(ckv, kr128, wk_ref, wv_ref, kn_ref, rope, k_ref, v_ref):
    for c0 in range(0, N_HEADS * C_HEAD_PAD, 2 * LANES):
        y = jnp.dot(ckv, wk_ref[:, c0:c0 + 2 * LANES], preferred_element_type=F32)
        for p in range(2):
            kp = y[:, p * LANES:(p + 1) * LANES] + kr128
            rstd = _mla_group_rstd(kp)
            if rope is None:
                kn = (kp * rstd) * kn_ref[...]
            else:
                kn = (kp * rope[0][...] + rope[2] * rope[1][...]) * rstd
            k_ref[:, c0 + p * LANES:c0 + (p + 1) * LANES] = kn.astype(k_ref.dtype)
    for c0 in range(0, N_HEADS * C_V, 2 * LANES):
        v = jnp.dot(ckv, wv_ref[:, c0:c0 + 2 * LANES], preferred_element_type=F32)
        v_ref[:, c0:c0 + 2 * LANES] = v.astype(v_ref.dtype)


def _mla_proj_kernel(*refs, rope):
    (x_ref, sh_ref, sc_ref, g_ref, wd_ref, qln_ref, kvln_ref, wuq_ref, wk_ref, wv_ref,
     qn_ref, kn_ref) = refs[:12]
    if rope:
        wuq_sw_ref, qc_ref, qs_ref, kc_ref, ks_ref = refs[12:17]
        q_ref, k_ref, v_ref, ckv_ref, kr_ref = refs[17:]
    else:
        q_ref, k_ref, v_ref, ckv_ref, kr_ref = refs[12:]
    h = _modulate(x_ref[...], g_ref[...], sh_ref[...], sc_ref[...]).astype(BF16)
    y = jnp.dot(h, wd_ref[...], preferred_element_type=F32)

    def row_norm(z, gain):
        return (z * lax.rsqrt(jnp.mean(z * z, axis=-1, keepdims=True) + RMS_EPS)) * gain

    cq = row_norm(y[:, :C_Q_RANK], qln_ref[...]).astype(BF16)
    ckv = row_norm(y[:, C_Q_RANK:C_Q_RANK + C_KV_RANK], kvln_ref[...])
    kr128 = pltpu.roll(y[:, C_Q_RANK + C_KV_RANK:C_DOWN_PAD], C_NOPE, 1)
    ckv_ref[...] = ckv
    kr_ref[...] = kr128

    for c0 in range(0, N_HEADS * C_HEAD_PAD, 2 * LANES):
        yq = jnp.dot(cq, wuq_ref[:, c0:c0 + 2 * LANES], preferred_element_type=F32)
        if rope:
            yq_sw = jnp.dot(cq, wuq_sw_ref[:, c0:c0 + 2 * LANES], preferred_element_type=F32)
        for p in range(2):
            part = slice(p * LANES, (p + 1) * LANES)
            rstd = _mla_group_rstd(yq[:, part])
            if rope:
                qn = (yq[:, part] * qc_ref[...] + yq_sw[:, part] * qs_ref[...]) * rstd
            else:
                qn = (yq[:, part] * rstd) * qn_ref[...]
            q_ref[:, c0 + p * LANES:c0 + (p + 1) * LANES] = qn.astype(q_ref.dtype)
    k_rope = (kc_ref, ks_ref, pltpu.roll(y[:, C_DOWN_PAD:], C_NOPE, 1)) if rope else None
    _mla_expand_kv(ckv.astype(BF16), kr128, wk_ref, wv_ref, kn_ref, k_rope, k_ref, v_ref)


def _mla_project(x, modv, layer, g_mix, w, row_fn, rope_tabs):
    n_tok = x.shape[0]
    rope = rope_tabs is not None
    tile = lambda wd: pl.BlockSpec((TOKEN_TILE, wd), lambda i: (i, 0))
    qk_cols = N_HEADS * C_HEAD_PAD
    w_down = w["down_rope"] if rope else w["down"]
    in_specs = [tile(D_MODEL), _mod_spec(layer, 3, row_fn), _mod_spec(layer, 4, row_fn),
                _resident((1, D_MODEL)), _resident(w_down.shape),
                _resident((1, C_Q_RANK)), _resident((1, C_KV_RANK)),
                _resident((C_Q_RANK, qk_cols)), _resident((C_KV_RANK, qk_cols)),
                _resident((C_KV_RANK, N_HEADS * C_V)),
                _resident((1, LANES)), _resident((1, LANES))]
    args = [x, modv, modv, g_mix.reshape(1, D_MODEL), w_down, w["q_lnorm"], w["kv_lnorm"],
            w["uq"], w["uk"], w["uv"], w["q_norm"], w["k_norm"]]
    if rope:
        per_batch = DEC_SEQ // TOKEN_TILE
        tab = pl.BlockSpec((TOKEN_TILE, LANES), lambda i: (i % per_batch, 0))
        cos, sin = rope_tabs
        in_specs += [_resident((C_Q_RANK, qk_cols)), tab, tab, tab, tab]
        args += [w["uq_sw"], cos * w["q_norm"], sin * _swap_pair_columns(w["q_norm"]),
                 cos * w["k_norm"], sin * _swap_pair_columns(w["k_norm"])]
    return pl.pallas_call(
        functools.partial(_mla_proj_kernel, rope=rope),
        grid=(n_tok // TOKEN_TILE,),
        in_specs=in_specs,
        out_specs=[tile(qk_cols), tile(qk_cols), tile(N_HEADS * C_V), tile(C_KV_RANK), tile(LANES)],
        out_shape=[jax.ShapeDtypeStruct((n_tok, qk_cols), BF16),
                   jax.ShapeDtypeStruct((n_tok, qk_cols), BF16),
                   jax.ShapeDtypeStruct((n_tok, N_HEADS * C_V), BF16),
                   jax.ShapeDtypeStruct((n_tok, C_KV_RANK), F32),
                   jax.ShapeDtypeStruct((n_tok, LANES), F32)],
        compiler_params=_params(1),
        name=f"mla_proj_{n_tok}",
    )(*args)


def _mla_cache_kernel(ckv_ref, kr_ref, wk_ref, wv_ref, kn_ref, k_ref, v_ref):
    _mla_expand_kv(ckv_ref[...].astype(BF16), kr_ref[...], wk_ref, wv_ref, kn_ref, None, k_ref, v_ref)


def _mla_expand_cache(ckv, kr128, w):
    n_tok = ckv.shape[0]
    tile = lambda wd: pl.BlockSpec((TOKEN_TILE, wd), lambda i: (i, 0))
    qk_cols = N_HEADS * C_HEAD_PAD
    return pl.pallas_call(
        _mla_cache_kernel,
        grid=(n_tok // TOKEN_TILE,),
        in_specs=[tile(C_KV_RANK), tile(LANES), _resident((C_KV_RANK, qk_cols)),
                  _resident((C_KV_RANK, N_HEADS * C_V)), _resident((1, LANES))],
        out_specs=[tile(qk_cols), tile(N_HEADS * C_V)],
        out_shape=[jax.ShapeDtypeStruct((n_tok, qk_cols), BF16),
                   jax.ShapeDtypeStruct((n_tok, N_HEADS * C_V), BF16)],
        compiler_params=_params(1),
        name="mla_cache_kv",
    )(ckv, kr128, w["uk"], w["uv"], w["k_norm"])


def _online_softmax(q, segs, scale, sink):
    dims = (((1,), (1,)), ((), ()))
    to_exp2 = scale * LOG2_E
    m = None
    acc = None
    for k, v1, bias, transposed in segs:
        if transposed:
            s = jnp.dot(q, k, preferred_element_type=F32)
        else:
            s = lax.dot_general(q, k, dims, preferred_element_type=F32)
        if bias is not None:
            s = s + bias
        row_max = jnp.max(s, axis=-1, keepdims=True)
        m_new = row_max if m is None else jnp.maximum(m, row_max)
        p = jnp.exp2((s - m_new) * to_exp2).astype(BF16)
        if transposed:
            pv = lax.dot_general(p, v1, dims, preferred_element_type=F32)
        else:
            pv = jnp.dot(p, v1, preferred_element_type=F32)
        acc = pv if m is None else acc * jnp.exp2((m - m_new) * to_exp2) + pv
        m = m_new
    if sink is not None:
        lane = lax.broadcasted_iota(jnp.int32, (1, 2 * HEAD_DIM), 1)
        acc = acc + jnp.where(lane >= HEAD_DIM, jnp.exp2((sink - m * scale) * LOG2_E), 0.0)
    denom = pltpu.roll(acc, HEAD_DIM, 1)
    return (acc * (1.0 / denom))[:, :HEAD_DIM]


def _attn_kernel(*refs, kv_step, group, dk, scale, seg_kinds, has_sink, tq):
    n_piece = sum(len(kinds) for kinds in seg_kinds)
    q_ref = refs[0]
    kv_refs = refs[1:1 + 2 * n_piece]
    sink_ref = refs[1 + 2 * n_piece] if has_sink else None
    o_ref = refs[-1]
    first_head = pl.program_id(1) * (kv_step * group)
    blk = pl.program_id(2)
    n_blk = pl.num_programs(2)

    def cat(parts, axis):
        return parts[0] if len(parts) == 1 else jnp.concatenate(parts, axis=axis)

    def piece_bias(kind, n_keys):
        r = lax.broadcasted_iota(jnp.int32, (tq, n_keys), 0)
        c = lax.broadcasted_iota(jnp.int32, (tq, n_keys), 1)
        if kind == "prev":
            ok = (c >= r) & (blk > 0)
        elif kind == "next":
            ok = (c <= r - (tq - A_WINDOW)) & (blk < n_blk - 1)
        elif kind == "band":
            ok = jnp.abs(r - c) <= A_WINDOW
        else:
            return jnp.zeros((tq, n_keys), F32)
        return jnp.where(ok, 0.0, NEG_INF)

    seg_refs, seg_bias = [], []
    p_i = 0
    for kinds in seg_kinds:
        pieces = [(kv_refs[2 * (p_i + n)], kv_refs[2 * (p_i + n) + 1]) for n in range(len(kinds))]
        p_i += len(kinds)
        seg_refs.append(pieces)
        if all(kind in ("full", "fullT") for kind in kinds):
            seg_bias.append(None)
        else:
            bias = cat([piece_bias(kind, kr.shape[0]) for kind, (kr, _) in zip(kinds, pieces)], 1)
            seg_bias.append(cat([bias] * group, 0))

    if seg_kinds in ((("full",),), (("fullT",),)):
        transposed = seg_kinds[0][0] == "fullT"
        kr, vr = seg_refs[0][0]
        contract = lambda a_dim, b_dim: (((a_dim,), (b_dim,)), ((), ()))
        logits = []
        for hk in range(kv_step):
            q = cat([q_ref[:, h * dk:(h + 1) * dk] for h in range(hk * group, (hk + 1) * group)], 0)
            k = kr[hk].astype(BF16) if transposed else kr[:, hk * dk:(hk + 1) * dk].astype(BF16)
            logits.append(lax.dot_general(k, q, contract(0 if transposed else 1, 1),
                                          preferred_element_type=F32))
        probs, denoms = [], []
        for hk, s_t in enumerate(logits):
            m = jnp.max(s_t, axis=0, keepdims=True)
            p_t = jnp.exp2((s_t - m) * (scale * LOG2_E))
            denom = jnp.sum(p_t, axis=0, keepdims=True)
            if has_sink:
                sink = cat([jnp.full((1, tq), sink_ref[first_head + h], F32)
                            for h in range(hk * group, (hk + 1) * group)], 1)
                denom = denom + jnp.exp2((sink - m * scale) * LOG2_E)
            probs.append(p_t.astype(BF16))
            denoms.append(denom)
        for hk in range(kv_step):
            v = vr[hk].astype(BF16) if transposed else vr[:, hk * HEAD_DIM:(hk + 1) * HEAD_DIM].astype(BF16)
            o_t = lax.dot_general(v, probs[hk], contract(1 if transposed else 0, 0),
                                  preferred_element_type=F32) * (1.0 / denoms[hk])
            for g_i in range(group):
                h = hk * group + g_i
                o_ref[:, h * HEAD_DIM:(h + 1) * HEAD_DIM] = o_t[:, g_i * tq:(g_i + 1) * tq].T.astype(o_ref.dtype)
        return

    for hk in range(kv_step):
        heads = range(hk * group, (hk + 1) * group)
        q = cat([q_ref[:, h * dk:(h + 1) * dk] for h in heads], 0)
        segs = []
        for kinds, pieces, bias in zip(seg_kinds, seg_refs, seg_bias):
            if kinds == ("fullT",):
                k = pieces[0][0][hk].astype(BF16)
                v = pieces[0][1][hk].astype(BF16)
                v1 = jnp.concatenate([v, jnp.ones_like(v)], axis=0)
                for c0 in range(0, k.shape[1], KEY_CHUNK):
                    segs.append((k[:, c0:c0 + KEY_CHUNK], v1[:, c0:c0 + KEY_CHUNK], None, True))
                continue
            k = cat([kr[:, hk * dk:(hk + 1) * dk].astype(BF16) for kr, _ in pieces], 0)
            v = cat([vr[:, hk * HEAD_DIM:(hk + 1) * HEAD_DIM].astype(BF16) for _, vr in pieces], 0)
            v1 = jnp.concatenate([v, jnp.ones_like(v)], axis=1)
            for c0 in range(0, k.shape[0], KEY_CHUNK):
                chunk = slice(c0, c0 + KEY_CHUNK)
                segs.append((k[chunk], v1[chunk], None if bias is None else bias[:, chunk], False))
        sink = (cat([jnp.full((tq, 2 * HEAD_DIM), sink_ref[first_head + h], F32) for h in heads], 0)
                if has_sink else None)
        out = _online_softmax(q, segs, scale, sink)
        for g_i, h in enumerate(heads):
            o_ref[:, h * HEAD_DIM:(h + 1) * HEAD_DIM] = out[g_i * tq:(g_i + 1) * tq].astype(o_ref.dtype)


def _attention(q, segs, *, n_batch, n_tok, tq, n_kv, kv_step, dk, scale, sink=None):
    n_blk = n_tok // tq
    per_blk = tq // A_WINDOW
    n_win = n_tok // A_WINDOW
    group = N_HEADS // n_kv
    in_specs = [pl.BlockSpec((tq, kv_step * group * dk), lambda b, g, i: (b * n_blk + i, g))]
    args = [q]
    for pieces in segs:
        for k, v, rows, kind in pieces:
            if kind == "fullT":
                idx = lambda b, g, i: (b, g, 0, 0)
                in_specs += [pl.BlockSpec((None, kv_step, dk, rows), idx),
                             pl.BlockSpec((None, kv_step, HEAD_DIM, rows), idx)]
                args += [k, v]
                continue
            if kind == "full":
                idx = lambda b, g, i: (b, g)
            elif kind == "band":
                idx = lambda b, g, i: (b * n_blk + i, g)
            elif kind == "prev":
                idx = lambda b, g, i: (b * n_win + jnp.maximum(i * per_blk - 1, 0), g)
            else:
                idx = lambda b, g, i: (b * n_win + jnp.minimum((i + 1) * per_blk, n_win - 1), g)
            in_specs += [pl.BlockSpec((rows, kv_step * dk), idx), pl.BlockSpec((rows, kv_step * HEAD_DIM), idx)]
            args += [k, v]
    seg_kinds = tuple(tuple(kind for _, _, _, kind in pieces) for pieces in segs)
    if sink is not None:
        in_specs.append(pl.BlockSpec(memory_space=pltpu.SMEM))
        args.append(sink)
    return pl.pallas_call(
        functools.partial(_attn_kernel, kv_step=kv_step, group=group, dk=dk, scale=scale,
                          seg_kinds=seg_kinds, has_sink=sink is not None, tq=tq),
        grid=(n_batch, n_kv // kv_step, n_blk),
        in_specs=in_specs,
        out_specs=pl.BlockSpec((tq, kv_step * group * HEAD_DIM), lambda b, g, i: (b * n_blk + i, g)),
        out_shape=jax.ShapeDtypeStruct((n_batch * n_tok, N_HEADS * HEAD_DIM), BF16),
        compiler_params=_params(3),
        name=f"attn_{n_batch}x{n_tok}_kv{n_kv}_dk{dk}_" + "_".join(k[0] for ks in seg_kinds for k in ks),
    )(*args)


def _na_row_window(r):
    rs = min(max(r - NA_ROWS // 2, 0), GRID_ROWS - NA_ROWS)
    ws = min(rs - rs % 2, GRID_ROWS - NA_WIN_ROWS)
    return ws, r - ws, rs - ws


def _na_kernel(q_ref, k_ref, v_ref, kc_ref, vc_ref, rows_ref, o_ref,
               s_ref, sc_ref, p_ref, pc_ref, l_ref, bias_ref):
    dims = (((1,), (1,)), ((), ()))

    @pl.when((pl.program_id(0) == 0) & (pl.program_id(1) == 0))
    def _():
        p_ref[...] = jnp.zeros_like(p_ref)

    @pl.when(pl.program_id(1) == 0)
    def _():
        _na_build_bias(rows_ref, bias_ref)

    for hh in range(LANES // HEAD_DIM):
        cols = slice(hh * HEAD_DIM, (hh + 1) * HEAD_DIM)
        q = q_ref[:, cols]
        s_ref[hh] = lax.dot_general(q, k_ref[:, cols], dims, preferred_element_type=F32)
        sc_ref[hh] = jnp.dot(q, kc_ref[hh].astype(BF16), preferred_element_type=F32)
        for r in range(GRID_ROWS):
            ws, slot, _ = _na_row_window(r)
            rows = slice(r * GRID_W, (r + 1) * GRID_W)
            win = slice(ws * GRID_W, (ws + NA_WIN_ROWS) * GRID_W)
            s_n = s_ref[hh, rows, win] + bias_ref[hh, slot]
            s_c = sc_ref[hh, rows, :]
            m = jnp.maximum(jnp.max(s_n, axis=-1, keepdims=True), jnp.max(s_c, axis=-1, keepdims=True))
            p_n = jnp.exp(s_n - m)
            p_c = jnp.exp(s_c - m)
            l_ref[hh, rows, :] = jnp.sum(p_n, axis=-1, keepdims=True) + jnp.sum(p_c, axis=-1, keepdims=True)
            p_ref[hh, rows, win] = p_n.astype(BF16)
            pc_ref[hh, rows, :] = p_c.astype(BF16)
        out = (jnp.dot(p_ref[hh], v_ref[:, cols], preferred_element_type=F32)
               + lax.dot_general(pc_ref[hh], vc_ref[hh].astype(BF16), dims, preferred_element_type=F32))
        o_ref[:, cols] = (out * (1.0 / l_ref[hh])).astype(o_ref.dtype)


def _na_build_bias(rows_ref, bias_ref):
    first_visible = {slot: first for _, slot, first in map(_na_row_window, range(GRID_ROWS))}
    assert sorted(first_visible) == list(range(NA_SLOTS))
    qcol = lax.broadcasted_iota(jnp.int32, (GRID_W, GRID_W), 0)
    kcol = lax.broadcasted_iota(jnp.int32, (GRID_W, GRID_W), 1)
    cs = jnp.clip(qcol - NA_COLS // 2, 0, GRID_W - NA_COLS)
    valid = (kcol >= cs) & (kcol < cs + NA_COLS)
    masked = jnp.full((GRID_W, GRID_W), NEG_INF, F32)
    for hh in range(LANES // HEAD_DIM):
        toep = []
        for ri in range(2 * NA_ROWS - 1):
            rows = jnp.broadcast_to(rows_ref[hh, ri:ri + 1, :], (GRID_W, LANES))
            rolled = pltpu.roll(rows, 0, 1, stride=1, stride_axis=0)
            toep.append(jnp.where(valid, rolled[:, :GRID_W], NEG_INF))
        for d in range(NA_SLOTS):
            for j0 in range(0, NA_WIN_ROWS, LANES // GRID_W):
                pieces = []
                for j in range(j0, j0 + LANES // GRID_W):
                    visible = first_visible[d] <= j < first_visible[d] + NA_ROWS
                    pieces.append(toep[j - d + NA_ROWS - 1] if visible else masked)
                bias_ref[hh, d, :, j0 * GRID_W:j0 * GRID_W + LANES] = jnp.concatenate(pieces, axis=1)


def _na_bias_rows(rpb):
    n_h, n_ri, n_ci = rpb.shape
    return jnp.concatenate([rpb[..., NA_COLS - 1:], jnp.zeros((n_h, n_ri, LANES - n_ci), F32),
                            rpb[..., :NA_COLS - 1]], axis=-1)


def _na_attention(q, k, v, kc, vc, rpb):
    n_pairs = N_HEADS * HEAD_DIM // LANES
    lat = pl.BlockSpec((DEC_SEQ, LANES), lambda hp, b: (b, hp))
    ctx = pl.BlockSpec((None, LANES // HEAD_DIM, HEAD_DIM, PAST_LEN), lambda hp, b: (b, hp, 0, 0))
    return pl.pallas_call(
        _na_kernel,
        grid=(n_pairs, DEC_BATCH),
        in_specs=[lat, lat, lat, ctx, ctx,
                  pl.BlockSpec((LANES // HEAD_DIM, 2 * NA_ROWS - 1, LANES), lambda hp, b: (hp, 0, 0))],
        out_specs=lat,
        out_shape=jax.ShapeDtypeStruct((DEC_BATCH * DEC_SEQ, N_HEADS * HEAD_DIM), BF16),
        scratch_shapes=[pltpu.VMEM((LANES // HEAD_DIM, DEC_SEQ, DEC_SEQ), F32),
                        pltpu.VMEM((LANES // HEAD_DIM, DEC_SEQ, PAST_LEN), F32),
                        pltpu.VMEM((LANES // HEAD_DIM, DEC_SEQ, DEC_SEQ), BF16),
                        pltpu.VMEM((LANES // HEAD_DIM, DEC_SEQ, PAST_LEN), BF16),
                        pltpu.VMEM((LANES // HEAD_DIM, DEC_SEQ, 1), F32),
                        pltpu.VMEM((LANES // HEAD_DIM, NA_SLOTS, GRID_W, NA_WIN_KEYS), F32)],
        compiler_params=_params(2),
        name="na_attn",
    )(q, k, v, kc, vc, _na_bias_rows(rpb))


def _grid_angles(n, rot_dim):
    pos = jnp.arange(n, dtype=jnp.int32)
    row = (pos // GRID_W).astype(F32)
    col = (pos % GRID_W).astype(F32)
    n_ax = rot_dim // 4
    inv = ROPE_THETA ** (-jnp.arange(n_ax, dtype=F32) / n_ax)
    return jnp.concatenate([row[:, None] * inv, col[:, None] * inv], axis=-1)


def _pair_tables(ang):
    cos = jnp.repeat(jnp.cos(ang), 2, axis=-1)
    sin = jnp.stack([-jnp.sin(ang), jnp.sin(ang)], axis=-1).reshape(ang.shape[0], -1)
    return cos, sin


def _gqa_rope_tables():
    cos, sin = _pair_tables(_grid_angles(DEC_SEQ, HEAD_DIM))
    return jnp.tile(cos, (1, LANES // HEAD_DIM)), jnp.tile(sin, (1, LANES // HEAD_DIM))


def _mla_rope_tables():
    cos, sin = _pair_tables(_grid_angles(DEC_SEQ, C_ROPE))
    pad = LANES - C_QK
    cos = jnp.concatenate([jnp.ones((DEC_SEQ, C_NOPE), F32), cos, jnp.ones((DEC_SEQ, pad), F32)], axis=-1)
    sin = jnp.concatenate([jnp.zeros((DEC_SEQ, C_NOPE), F32), sin, jnp.zeros((DEC_SEQ, pad), F32)], axis=-1)
    return cos, sin


def _mla_weights(w_down, q_lnorm, kv_lnorm, w_uq, w_ukv, q_norm, k_norm):
    pad_head = lambda a: jnp.pad(a, [(0, 0)] * (a.ndim - 1) + [(0, C_HEAD_PAD - a.shape[-1])])
    ukv = w_ukv.reshape(C_KV_RANK, N_HEADS, C_NOPE + C_V)
    down = jnp.pad(w_down, ((0, 0), (0, C_DOWN_PAD - w_down.shape[1]))).astype(BF16)
    uq = pad_head(w_uq.reshape(C_Q_RANK, N_HEADS, C_QK)).reshape(C_Q_RANK, -1).astype(BF16)
    return {
        "down": down,
        "down_rope": jnp.concatenate([down, _swap_pair_columns(down[:, C_DOWN_PAD - LANES:])], axis=1),
        "q_lnorm": q_lnorm.reshape(1, C_Q_RANK),
        "kv_lnorm": kv_lnorm.reshape(1, C_KV_RANK),
        "uq": uq,
        "uq_sw": _swap_pair_columns(uq),
        "uk": pad_head(ukv[:, :, :C_NOPE]).reshape(C_KV_RANK, -1).astype(BF16),
        "uv": ukv[:, :, C_NOPE:].reshape(C_KV_RANK, -1).astype(BF16),
        "q_norm": pad_head(q_norm).reshape(1, C_HEAD_PAD),
        "k_norm": pad_head(k_norm).reshape(1, C_HEAD_PAD),
    }


def kernel(x_prompt, x_sample, cache_a_k, cache_a_v, cache_b_k, cache_b_v, cache_c_kv, cache_c_krope, cache_d_k, cache_d_v, c, c_ctx, mod_w, mod_b, norm_ff1, norm_mix, norm_ff2, ff1_w_gu, ff1_w_down, ff2_w_gu, ff2_w_down, a_w_qkv, a_q_norm, a_k_norm, a_sink, a_w_o, b_w_qkv, b_q_norm, b_k_norm, b_w_o, c_w_down, c_q_lnorm, c_kv_lnorm, c_w_uq, c_w_ukv, c_q_norm, c_k_norm, c_w_o, d_w_qkv, d_q_norm, d_k_norm, d_rpb, d_w_o):
    n_p, n_s = BATCH * SEQ, DEC_BATCH * DEC_SEQ
    xp = x_prompt.reshape(n_p, D_MODEL)
    xs = x_sample.reshape(n_s, D_MODEL)
    cond = jnp.concatenate([c_ctx[None], c, jnp.zeros((MOD_ROWS - 1 - DEC_BATCH, D_MODEL), F32)], axis=0)
    modv = _mod_table(cond, mod_w, mod_b)
    s_row = _sample_row(TOKEN_TILE)
    gqa_tabs = _gqa_rope_tables()
    gqa_scale = 1.0
    flat_cache = lambda a: a.reshape(DEC_BATCH * PAST_LEN, -1)
    keys_last = lambda a: jnp.transpose(a, (0, 2, 3, 1))
    keys_first = lambda a: jnp.transpose(a, (0, 3, 1, 2))[:, None]
    ffn_w = [(ff1_w_gu.astype(BF16), ff1_w_down.astype(BF16)), (ff2_w_gu.astype(BF16), ff2_w_down.astype(BF16))]
    new = {}

    for i in range(DEPTH):
        kind, j = i % N_MIXERS, i // N_MIXERS
        xp = _half_ffn(xp, modv, i, 0, norm_ff1[i], *ffn_w[0], _prompt_row)
        xs = _half_ffn(xs, modv, i, 0, norm_ff1[i], *ffn_w[0], s_row)

        if kind in (0, 1, 3):
            w_qkv, q_norm, k_norm, w_o, n_kv, cache_k, cache_v = {
                0: (a_w_qkv, a_q_norm, a_k_norm, a_w_o, GQA_KV_HEADS, cache_a_k, cache_a_v),
                1: (b_w_qkv, b_q_norm, b_k_norm, b_w_o, GQA_KV_HEADS, cache_b_k, cache_b_v),
                3: (d_w_qkv, d_q_norm, d_k_norm, d_w_o, N_HEADS, cache_d_k, cache_d_v)}[kind]
            w_qkv = w_qkv[j].astype(BF16)
            qp, kp, vp = _gqa_project(xp, modv, i, norm_mix[i], w_qkv, q_norm[j], k_norm[j], n_kv,
                                      _prompt_row, None, True)
            qs, ks, vs = _gqa_project(xs, modv, i, norm_mix[i], w_qkv, q_norm[j], k_norm[j], n_kv,
                                      s_row, gqa_tabs if kind != 3 else None, False)
            op = _attention(qp, [[(kp, vp, SEQ, "fullT")]], n_batch=BATCH, n_tok=SEQ, tq=SEQ, n_kv=n_kv,
                            kv_step=n_kv, dk=HEAD_DIM, scale=gqa_scale, sink=a_sink[j] if kind == 0 else None)
            name = "abcd"[kind]
            new[name + "_k"], new[name + "_v"] = keys_first(kp), keys_first(vp)
            kc, vc = keys_last(cache_k[:, j]), keys_last(cache_v[:, j])
            if kind == 0:
                tq = 2 * A_WINDOW
                os_ = _attention(qs, [[(ks, vs, A_WINDOW, "prev"), (ks, vs, tq, "band"),
                                       (ks, vs, A_WINDOW, "next")], [(kc, vc, PAST_LEN, "fullT")]],
                                 n_batch=DEC_BATCH, n_tok=DEC_SEQ, tq=tq, n_kv=n_kv, kv_step=2, dk=HEAD_DIM,
                                 scale=gqa_scale, sink=a_sink[j])
            elif kind == 1:
                os_ = _attention(qs, [[(ks, vs, DEC_SEQ, "full")], [(kc, vc, PAST_LEN, "fullT")]],
                                 n_batch=DEC_BATCH, n_tok=DEC_SEQ, tq=512, n_kv=n_kv, kv_step=2, dk=HEAD_DIM,
                                 scale=gqa_scale)
            else:
                os_ = _na_attention(qs, ks, vs, kc, vc, d_rpb[j])
        else:
            w = _mla_weights(c_w_down[j], c_q_lnorm[j], c_kv_lnorm[j], c_w_uq[j], c_w_ukv[j],
                             c_q_norm[j], c_k_norm[j])
            w_o = c_w_o
            qp, kp, vp, ckv_p, kr_p = _mla_project(xp, modv, i, norm_mix[i], w, _prompt_row, None)
            qs, ks, vs, _, _ = _mla_project(xs, modv, i, norm_mix[i], w, s_row, _mla_rope_tables())
            new["c_kv"] = ckv_p.reshape(BATCH, 1, SEQ, C_KV_RANK)
            new["c_krope"] = kr_p[:, C_NOPE:C_QK].reshape(BATCH, 1, SEQ, C_ROPE)
            kr_cache = jnp.pad(flat_cache(cache_c_krope[:, j]), ((0, 0), (C_NOPE, LANES - C_QK)))
            kc, vc = _mla_expand_cache(flat_cache(cache_c_kv[:, j]), kr_cache, w)
            mla_scale = C_QK ** -0.5
            op = _attention(qp, [[(kp, vp, SEQ, "full")]], n_batch=BATCH, n_tok=SEQ, tq=SEQ,
                            n_kv=N_HEADS, kv_step=N_HEADS, dk=C_HEAD_PAD, scale=mla_scale)
            os_ = _attention(qs, [[(ks, vs, DEC_SEQ, "full")], [(kc, vc, PAST_LEN, "full")]],
                             n_batch=DEC_BATCH, n_tok=DEC_SEQ, tq=DEC_SEQ, n_kv=N_HEADS, kv_step=2, dk=C_HEAD_PAD,
                             scale=mla_scale)

        w_o = w_o[j].astype(BF16)
        xp = _half_ffn(xp, modv, i, 6, norm_ff2[i], *ffn_w[1], _prompt_row, mixer_out=(op, w_o))
        xs = _half_ffn(xs, modv, i, 6, norm_ff2[i], *ffn_w[1], s_row, mixer_out=(os_, w_o))

    return (xp.reshape(BATCH, SEQ, D_MODEL), xs.reshape(DEC_BATCH, DEC_SEQ, D_MODEL),
            new["a_k"], new["a_v"], new["b_k"], new["b_v"], new["c_kv"], new["c_krope"],
            new["d_k"], new["d_v"])
```

```python
import functools

import jax
import jax.numpy as jnp
from jax import lax
from jax.experimental import pallas as pl
from jax.experimental.pallas import tpu as pltpu

F32 = jnp.float32
BF16 = jnp.bfloat16

D_MODEL = 1024
BATCH = 16
SEQ = 256
DEPTH = 4
DEC_BATCH = 8
DEC_SEQ = 1024
PAST_LEN = 512
GRID_W = 64
N_MIXERS = 4
RMS_EPS = 1e-6
ROPE_THETA = 10000.0
NEG_INF = -1e30
D_FF = 2816
FFN_RES = 0.5
N_MOD = 9
HEAD_DIM = 64
N_HEADS = 16
GQA_KV_HEADS = 4
A_WINDOW = 128
C_Q_RANK = 384
C_KV_RANK = 256
C_NOPE = 64
C_ROPE = 32
C_V = 64
C_QK = C_NOPE + C_ROPE
NA_ROWS = 8
NA_COLS = 16

LANES = 128
MOD_ROWS = 16
C_DOWN_PAD = 768
C_HEAD_PAD = LANES
VMEM_LIMIT_BYTES = 56 * 1024 * 1024
TOKEN_TILE = 512
FF_CHUNK = 256
N_WEIGHT_SLOTS = 2
KEY_CHUNK = 512
LOG2_E = 1.4426950408889634
GQA_SCALE = HEAD_DIM ** -0.5
GRID_ROWS = DEC_SEQ // GRID_W
NA_WIN_ROWS = NA_ROWS + 2
NA_WIN_KEYS = NA_WIN_ROWS * GRID_W
NA_SLOTS = NA_WIN_ROWS


def _params(n_axes):
    return pltpu.CompilerParams(dimension_semantics=("arbitrary",) * n_axes,
                                vmem_limit_bytes=VMEM_LIMIT_BYTES)


def _resident(shape):
    nd = len(shape)
    return pl.BlockSpec(shape, lambda *_: (0,) * nd, pipeline_mode=pl.Buffered(1))


def _mod_spec(layer, part, row_fn):
    base = (layer * N_MOD + part) * MOD_ROWS
    return pl.BlockSpec((None, 1, D_MODEL), lambda i: (base + row_fn(i), 0, 0))


def _prompt_row(i):
    return 0


def _sample_row(tile):
    per_batch = DEC_SEQ // tile
    return lambda i: 1 + i // per_batch


def _silu(a):
    return a * (1.0 / (1.0 + jnp.exp(-a)))


def _modulate(x, g, shift, scale):
    y = x * lax.rsqrt(jnp.mean(x * x, axis=-1, keepdims=True) + RMS_EPS)
    return (y * g) * (1.0 + scale) + shift


def _swap_pair_columns(a):
    even = jnp.arange(a.shape[-1]) % 2 == 0
    return jnp.where(even, jnp.roll(a, -1, axis=-1), jnp.roll(a, 1, axis=-1))


def _mod_kernel(c_ref, w_ref, b_ref, o_ref):
    s = _silu(c_ref[...]).astype(BF16)
    o_ref[...] = jnp.dot(s, w_ref[...].astype(BF16), preferred_element_type=F32) + b_ref[...]


def _mod_table(cond, mod_w, mod_b):
    out = pl.pallas_call(
        _mod_kernel,
        grid=(DEPTH, N_MOD),
        in_specs=[
            pl.BlockSpec((MOD_ROWS, D_MODEL), lambda l, j: (0, 0)),
            pl.BlockSpec((None, D_MODEL, D_MODEL), lambda l, j: (l, 0, j)),
            pl.BlockSpec((None, 1, D_MODEL), lambda l, j: (l, 0, j)),
        ],
        out_specs=pl.BlockSpec((None, None, MOD_ROWS, D_MODEL), lambda l, j: (l, j, 0, 0)),
        out_shape=jax.ShapeDtypeStruct((DEPTH, N_MOD, MOD_ROWS, D_MODEL), F32),
        compiler_params=_params(2),
        name="mod_table",
    )(cond, mod_w, mod_b.reshape(DEPTH, 1, N_MOD * D_MODEL))
    return out.reshape(DEPTH * N_MOD * MOD_ROWS, 1, D_MODEL)


def _ffn_kernel(*refs, mixer_out, layer):
    if mixer_out:
        x_ref, attn_ref, gm_ref, wo_ref = refs[:4]
        refs = refs[4:]
    else:
        x_ref = refs[0]
        refs = refs[1:]
    (sh_ref, sc_ref, gt_ref, g_ref, wgu_hbm, wd_hbm, o_ref,
     act_ref, wgu_ref, wd_ref, gate_stage, up_stage, down_stage, sems) = refs
    n_chunks = D_FF // FF_CHUNK

    def residual_in():
        if not mixer_out:
            return x_ref[...]
        return x_ref[...] + gm_ref[...] * jnp.dot(attn_ref[...], wo_ref[...], preferred_element_type=F32)

    def chunk_copies(c, slot):
        c0 = c * FF_CHUNK
        return (pltpu.make_async_copy(wgu_hbm.at[layer, :, pl.ds(c0, FF_CHUNK)], gate_stage.at[slot],
                                      sems.at[0, slot]),
                pltpu.make_async_copy(wgu_hbm.at[layer, :, pl.ds(D_FF + c0, FF_CHUNK)], up_stage.at[slot],
                                      sems.at[1, slot]),
                pltpu.make_async_copy(wd_hbm.at[layer, pl.ds(c0, FF_CHUNK), :], down_stage.at[slot],
                                      sems.at[2, slot]))

    def fetch_chunk(c):
        slot = c % N_WEIGHT_SLOTS
        c0 = c * FF_CHUNK
        for copy in chunk_copies(c, slot):
            copy.wait()
        wgu_ref[:, c0:c0 + FF_CHUNK] = gate_stage[slot].astype(BF16)
        wgu_ref[:, D_FF + c0:D_FF + c0 + FF_CHUNK] = up_stage[slot].astype(BF16)
        wd_ref[c0:c0 + FF_CHUNK, :] = down_stage[slot].astype(BF16)
        if c + N_WEIGHT_SLOTS < n_chunks:
            for copy in chunk_copies(c + N_WEIGHT_SLOTS, slot):
                copy.start()

    def tile(stream_weights):
        if stream_weights:
            for c in range(min(N_WEIGHT_SLOTS, n_chunks)):
                for copy in chunk_copies(c, c):
                    copy.start()
        x = residual_in()
        h = _modulate(x, g_ref[...], sh_ref[...], sc_ref[...]).astype(BF16)
        for c in range(n_chunks):
            if stream_weights:
                fetch_chunk(c)
            c0 = c * FF_CHUNK
            a = jnp.dot(h, wgu_ref[:, c0:c0 + FF_CHUNK], preferred_element_type=F32)
            u = jnp.dot(h, wgu_ref[:, D_FF + c0:D_FF + c0 + FF_CHUNK], preferred_element_type=F32)
            act_ref[:, c0:c0 + FF_CHUNK] = (_silu(a) * u).astype(BF16)
        y = jnp.dot(act_ref[...], wd_ref[...], preferred_element_type=F32)
        o_ref[...] = x + (FFN_RES * gt_ref[...]) * y

    @pl.when(pl.program_id(0) == 0)
    def _():
        tile(True)

    @pl.when(pl.program_id(0) > 0)
    def _():
        tile(False)


def _half_ffn(x, modv, layer, part0, g, wgu, wd, row_fn, mixer_out=None):
    n_tok = x.shape[0]
    tile = pl.BlockSpec((TOKEN_TILE, D_MODEL), lambda i: (i, 0))
    in_specs, args = [tile], [x]
    if mixer_out is not None:
        in_specs += [tile, _mod_spec(layer, 5, row_fn), _resident((D_MODEL, D_MODEL))]
        args += [mixer_out[0], modv, mixer_out[1]]
    in_specs += [_mod_spec(layer, part0, row_fn), _mod_spec(layer, part0 + 1, row_fn),
                 _mod_spec(layer, part0 + 2, row_fn),
                 _resident((1, D_MODEL)), pl.BlockSpec(memory_space=pl.ANY), pl.BlockSpec(memory_space=pl.ANY)]
    args += [modv, modv, modv, g.reshape(1, D_MODEL), wgu, wd]
    return pl.pallas_call(
        functools.partial(_ffn_kernel, mixer_out=mixer_out is not None, layer=layer),
        grid=(n_tok // TOKEN_TILE,),
        in_specs=in_specs,
        out_specs=tile,
        out_shape=jax.ShapeDtypeStruct((n_tok, D_MODEL), F32),
        scratch_shapes=[pltpu.VMEM((TOKEN_TILE, D_FF), BF16),
                        pltpu.VMEM((D_MODEL, 2 * D_FF), BF16),
                        pltpu.VMEM((D_FF, D_MODEL), BF16),
                        pltpu.VMEM((N_WEIGHT_SLOTS, D_MODEL, FF_CHUNK), F32),
                        pltpu.VMEM((N_WEIGHT_SLOTS, D_MODEL, FF_CHUNK), F32),
                        pltpu.VMEM((N_WEIGHT_SLOTS, FF_CHUNK, D_MODEL), F32),
                        pltpu.SemaphoreType.DMA((3, N_WEIGHT_SLOTS))],
        compiler_params=_params(1),
        name=f"ffn_{n_tok}" + ("_mix" if mixer_out is not None else ""),
    )(*args)


def _head_pair_rstd(yp, lo):
    sq = yp * yp
    s_lo = jnp.sum(jnp.where(lo, sq, 0.0), axis=-1, keepdims=True)
    s_hi = jnp.sum(jnp.where(lo, 0.0, sq), axis=-1, keepdims=True)
    ms = jnp.where(lo, s_lo, s_hi) * (1.0 / HEAD_DIM)
    return lax.rsqrt(ms + RMS_EPS)


def _gqa_proj_kernel(*refs, n_q, n_kv, rope, kv_transposed):
    x_ref, sh_ref, sc_ref, g_ref, w_ref, qn_ref, kn_ref = refs[:7]
    if rope:
        wsw_ref, qc_ref, qs_ref, kc_ref, ks_ref = refs[7:12]
        q_ref, k_ref, v_ref = refs[12:]
    else:
        q_ref, k_ref, v_ref = refs[7:]
    h = _modulate(x_ref[...], g_ref[...], sh_ref[...], sc_ref[...]).astype(BF16)
    lane = lax.broadcasted_iota(jnp.int32, (1, LANES), 1)
    lo = lane < HEAD_DIM
    q_cols = n_q * HEAD_DIM
    k_cols = n_kv * HEAD_DIM

    def store(out_ref, o0, y, transposed):
        if not transposed:
            out_ref[:, o0:o0 + LANES] = y.astype(out_ref.dtype)
            return
        yt = y.T
        for b_i in range(y.shape[0] // SEQ):
            for h_i in range(LANES // HEAD_DIM):
                out_ref[b_i, o0 // HEAD_DIM + h_i] = yt[h_i * HEAD_DIM:(h_i + 1) * HEAD_DIM,
                                                        b_i * SEQ:(b_i + 1) * SEQ]

    def normed(c0, gain_ref, tabs, out_ref, o0, post_scale, transposed):
        y = jnp.dot(h, w_ref[:, c0:c0 + 2 * LANES], preferred_element_type=F32)
        if rope:
            y_sw = jnp.dot(h, wsw_ref[:, c0:c0 + 2 * LANES], preferred_element_type=F32)
        for p in range(2):
            part = slice(p * LANES, (p + 1) * LANES)
            rstd = _head_pair_rstd(y[:, part], lo)
            if rope:
                yn = (y[:, part] * tabs[0][...] + y_sw[:, part] * tabs[1][...]) * rstd
            else:
                yn = (y[:, part] * rstd) * gain_ref[...]
            if post_scale is not None:
                yn = yn * post_scale
            store(out_ref, o0 + p * LANES, yn, transposed)

    for c0 in range(0, q_cols, 2 * LANES):
        normed(c0, qn_ref, (qc_ref, qs_ref) if rope else None, q_ref, c0, GQA_SCALE, False)
    for c0 in range(0, k_cols, 2 * LANES):
        normed(q_cols + c0, kn_ref, (kc_ref, ks_ref) if rope else None, k_ref, c0, None, kv_transposed)
    for c0 in range(0, k_cols, 2 * LANES):
        v = jnp.dot(h, w_ref[:, q_cols + k_cols + c0:q_cols + k_cols + c0 + 2 * LANES],
                    preferred_element_type=F32)
        for p in range(2):
            store(v_ref, c0 + p * LANES, v[:, p * LANES:(p + 1) * LANES], kv_transposed)


def _gqa_project(x, modv, layer, g_mix, w_qkv, q_norm, k_norm, n_kv, row_fn, rope_tabs, kv_transposed):
    n_tok = x.shape[0]
    q_cols, k_cols = N_HEADS * HEAD_DIM, n_kv * HEAD_DIM
    rope = rope_tabs is not None
    if kv_transposed:
        per_tile = TOKEN_TILE // SEQ
        kv_spec = pl.BlockSpec((per_tile, n_kv, HEAD_DIM, SEQ), lambda i: (i, 0, 0, 0))
        kv_shape = jax.ShapeDtypeStruct((n_tok // SEQ, n_kv, HEAD_DIM, SEQ), F32)
    else:
        kv_spec = pl.BlockSpec((TOKEN_TILE, k_cols), lambda i: (i, 0))
        kv_shape = jax.ShapeDtypeStruct((n_tok, k_cols), BF16)
    tile = lambda w: pl.BlockSpec((TOKEN_TILE, w), lambda i: (i, 0))
    in_specs = [tile(D_MODEL), _mod_spec(layer, 3, row_fn), _mod_spec(layer, 4, row_fn),
                _resident((1, D_MODEL)), _resident((D_MODEL, q_cols + 2 * k_cols)),
                _resident((1, LANES)), _resident((1, LANES))]
    pair_tile = lambda g: jnp.tile(g, LANES // HEAD_DIM).reshape(1, LANES)
    args = [x, modv, modv, g_mix.reshape(1, D_MODEL), w_qkv, pair_tile(q_norm), pair_tile(k_norm)]
    if rope:
        per_batch = DEC_SEQ // TOKEN_TILE
        tab = pl.BlockSpec((TOKEN_TILE, LANES), lambda i: (i % per_batch, 0))
        cos, sin = rope_tabs
        in_specs += [_resident((D_MODEL, q_cols + k_cols)), tab, tab, tab, tab]
        args += [_swap_pair_columns(w_qkv[:, :q_cols + k_cols]),
                 cos * pair_tile(q_norm), sin * pair_tile(_swap_pair_columns(q_norm)),
                 cos * pair_tile(k_norm), sin * pair_tile(_swap_pair_columns(k_norm))]
    return pl.pallas_call(
        functools.partial(_gqa_proj_kernel, n_q=N_HEADS, n_kv=n_kv, rope=rope, kv_transposed=kv_transposed),
        grid=(n_tok // TOKEN_TILE,),
        in_specs=in_specs,
        out_specs=[tile(q_cols), kv_spec, kv_spec],
        out_shape=[jax.ShapeDtypeStruct((n_tok, q_cols), BF16), kv_shape, kv_shape],
        compiler_params=_params(1),
        name=f"gqa_proj_{n_tok}_kv{n_kv}",
    )(*args)


def _mla_group_rstd(y):
    ms = jnp.sum(y * y, axis=-1, keepdims=True) * (1.0 / C_QK)
    return lax.rsqrt(ms + RMS_EPS)


def _mla_expand_kv(ckv, kr128, wk_ref, wv_ref, kn_ref, rope, k_ref, v_ref):
    for c0 in range(0, N_HEADS * C_HEAD_PAD, 2 * LANES):
        y = jnp.dot(ckv, wk_ref[:, c0:c0 + 2 * LANES], preferred_element_type=F32)
        for p in range(2):
            kp = y[:, p * LANES:(p + 1) * LANES] + kr128
            rstd = _mla_group_rstd(kp)
            if rope is None:
                kn = (kp * rstd) * kn_ref[...]
            else:
                kn = (kp * rope[0][...] + rope[2] * rope[1][...]) * rstd
            k_ref[:, c0 + p * LANES:c0 + (p + 1) * LANES] = kn.astype(k_ref.dtype)
    for c0 in range(0, N_HEADS * C_V, 2 * LANES):
        v = jnp.dot(ckv, wv_ref[:, c0:c0 + 2 * LANES], preferred_element_type=F32)
        v_ref[:, c0:c0 + 2 * LANES] = v.astype(v_ref.dtype)


def _mla_proj_kernel(*refs, rope):
    (x_ref, sh_ref, sc_ref, g_ref, wd_ref, qln_ref, kvln_ref, wuq_ref, wk_ref, wv_ref,
     qn_ref, kn_ref) = refs[:12]
    if rope:
        wuq_sw_ref, qc_ref, qs_ref, kc_ref, ks_ref = refs[12:17]
        q_ref, k_ref, v_ref, ckv_ref, kr_ref = refs[17:]
    else:
        q_ref, k_ref, v_ref, ckv_ref, kr_ref = refs[12:]
    h = _modulate(x_ref[...], g_ref[...], sh_ref[...], sc_ref[...]).astype(BF16)
    y = jnp.dot(h, wd_ref[...], preferred_element_type=F32)

    def row_norm(z, gain):
        return (z * lax.rsqrt(jnp.mean(z * z, axis=-1, keepdims=True) + RMS_EPS)) * gain

    cq = row_norm(y[:, :C_Q_RANK], qln_ref[...]).astype(BF16)
    ckv = row_norm(y[:, C_Q_RANK:C_Q_RANK + C_KV_RANK], kvln_ref[...])
    kr128 = pltpu.roll(y[:, C_Q_RANK + C_KV_RANK:C_DOWN_PAD], C_NOPE, 1)
    ckv_ref[...] = ckv
    kr_ref[...] = kr128

    for c0 in range(0, N_HEADS * C_HEAD_PAD, 2 * LANES):
        yq = jnp.dot(cq, wuq_ref[:, c0:c0 + 2 * LANES], preferred_element_type=F32)
        if rope:
            yq_sw = jnp.dot(cq, wuq_sw_ref[:, c0:c0 + 2 * LANES], preferred_element_type=F32)
        for p in range(2):
            part = slice(p * LANES, (p + 1) * LANES)
            rstd = _mla_group_rstd(yq[:, part])
            if rope:
                qn = (yq[:, part] * qc_ref[...] + yq_sw[:, part] * qs_ref[...]) * rstd
            else:
                qn = (yq[:, part] * rstd) * qn_ref[...]
            q_ref[:, c0 + p * LANES:c0 + (p + 1) * LANES] = qn.astype(q_ref.dtype)
    k_rope = (kc_ref, ks_ref, pltpu.roll(y[:, C_DOWN_PAD:], C_NOPE, 1)) if rope else None
    _mla_expand_kv(ckv.astype(BF16), kr128, wk_ref, wv_ref, kn_ref, k_rope, k_ref, v_ref)


def _mla_project(x, modv, layer, g_mix, w, row_fn, rope_tabs):
    n_tok = x.shape[0]
    rope = rope_tabs is not None
    tile = lambda wd: pl.BlockSpec((TOKEN_TILE, wd), lambda i: (i, 0))
    qk_cols = N_HEADS * C_HEAD_PAD
    w_down = w["down_rope"] if rope else w["down"]
    in_specs = [tile(D_MODEL), _mod_spec(layer, 3, row_fn), _mod_spec(layer, 4, row_fn),
                _resident((1, D_MODEL)), _resident(w_down.shape),
                _resident((1, C_Q_RANK)), _resident((1, C_KV_RANK)),
                _resident((C_Q_RANK, qk_cols)), _resident((C_KV_RANK, qk_cols)),
                _resident((C_KV_RANK, N_HEADS * C_V)),
                _resident((1, LANES)), _resident((1, LANES))]
    args = [x, modv, modv, g_mix.reshape(1, D_MODEL), w_down, w["q_lnorm"], w["kv_lnorm"],
            w["uq"], w["uk"], w["uv"], w["q_norm"], w["k_norm"]]
    if rope:
        per_batch = DEC_SEQ // TOKEN_TILE
        tab = pl.BlockSpec((TOKEN_TILE, LANES), lambda i: (i % per_batch, 0))
        cos, sin = rope_tabs
        in_specs += [_resident((C_Q_RANK, qk_cols)), tab, tab, tab, tab]
        args += [w["uq_sw"], cos * w["q_norm"], sin * _swap_pair_columns(w["q_norm"]),
                 cos * w["k_norm"], sin * _swap_pair_columns(w["k_norm"])]
    return pl.pallas_call(
        functools.partial(_mla_proj_kernel, rope=rope),
        grid=(n_tok // TOKEN_TILE,),
        in_specs=in_specs,
        out_specs=[tile(qk_cols), tile(qk_cols), tile(N_HEADS * C_V), tile(C_KV_RANK), tile(LANES)],
        out_shape=[jax.ShapeDtypeStruct((n_tok, qk_cols), BF16),
                   jax.ShapeDtypeStruct((n_tok, qk_cols), BF16),
                   jax.ShapeDtypeStruct((n_tok, N_HEADS * C_V), BF16),
                   jax.ShapeDtypeStruct((n_tok, C_KV_RANK), F32),
                   jax.ShapeDtypeStruct((n_tok, LANES), F32)],
        compiler_params=_params(1),
        name=f"mla_proj_{n_tok}",
    )(*args)


def _mla_cache_kernel(ckv_ref, kr_ref, wk_ref, wv_ref, kn_ref, k_ref, v_ref):
    _mla_expand_kv(ckv_ref[...].astype(BF16), kr_ref[...], wk_ref, wv_ref, kn_ref, None, k_ref, v_ref)


def _mla_expand_cache(ckv, kr128, w):
    n_tok = ckv.shape[0]
    tile = lambda wd: pl.BlockSpec((TOKEN_TILE, wd), lambda i: (i, 0))
    qk_cols = N_HEADS * C_HEAD_PAD
    return pl.pallas_call(
        _mla_cache_kernel,
        grid=(n_tok // TOKEN_TILE,),
        in_specs=[tile(C_KV_RANK), tile(LANES), _resident((C_KV_RANK, qk_cols)),
                  _resident((C_KV_RANK, N_HEADS * C_V)), _resident((1, LANES))],
        out_specs=[tile(qk_cols), tile(N_HEADS * C_V)],
        out_shape=[jax.ShapeDtypeStruct((n_tok, qk_cols), BF16),
                   jax.ShapeDtypeStruct((n_tok, N_HEADS * C_V), BF16)],
        compiler_params=_params(1),
        name="mla_cache_kv",
    )(ckv, kr128, w["uk"], w["uv"], w["k_norm"])


def _online_softmax(q, segs, scale, sink):
    dims = (((1,), (1,)), ((), ()))
    to_exp2 = scale * LOG2_E
    m = None
    acc = None
    for k, v1, bias, transposed in segs:
        if transposed:
            s = jnp.dot(q, k, preferred_element_type=F32)
        else:
            s = lax.dot_general(q, k, dims, preferred_element_type=F32)
        if bias is not None:
            s = s + bias
        row_max = jnp.max(s, axis=-1, keepdims=True)
        m_new = row_max if m is None else jnp.maximum(m, row_max)
        p = jnp.exp2((s - m_new) * to_exp2).astype(BF16)
        if transposed:
            pv = lax.dot_general(p, v1, dims, preferred_element_type=F32)
        else:
            pv = jnp.dot(p, v1, preferred_element_type=F32)
        acc = pv if m is None else acc * jnp.exp2((m - m_new) * to_exp2) + pv
        m = m_new
    if sink is not None:
        lane = lax.broadcasted_iota(jnp.int32, (1, 2 * HEAD_DIM), 1)
        acc = acc + jnp.where(lane >= HEAD_DIM, jnp.exp2((sink - m * scale) * LOG2_E), 0.0)
    denom = pltpu.roll(acc, HEAD_DIM, 1)
    return (acc * (1.0 / denom))[:, :HEAD_DIM]


def _attn_kernel(*refs, kv_step, group, dk, scale, seg_kinds, has_sink, tq):
    n_piece = sum(len(kinds) for kinds in seg_kinds)
    q_ref = refs[0]
    kv_refs = refs[1:1 + 2 * n_piece]
    sink_ref = refs[1 + 2 * n_piece] if has_sink else None
    o_ref = refs[-1]
    first_head = pl.program_id(1) * (kv_step * group)
    blk = pl.program_id(2)
    n_blk = pl.num_programs(2)

    def cat(parts, axis):
        return parts[0] if len(parts) == 1 else jnp.concatenate(parts, axis=axis)

    def piece_bias(kind, n_keys):
        r = lax.broadcasted_iota(jnp.int32, (tq, n_keys), 0)
        c = lax.broadcasted_iota(jnp.int32, (tq, n_keys), 1)
        if kind == "prev":
            ok = (c >= r) & (blk > 0)
        elif kind == "next":
            ok = (c <= r - (tq - A_WINDOW)) & (blk < n_blk - 1)
        elif kind == "band":
            ok = jnp.abs(r - c) <= A_WINDOW
        else:
            return jnp.zeros((tq, n_keys), F32)
        return jnp.where(ok, 0.0, NEG_INF)

    seg_refs, seg_bias = [], []
    p_i = 0
    for kinds in seg_kinds:
        pieces = [(kv_refs[2 * (p_i + n)], kv_refs[2 * (p_i + n) + 1]) for n in range(len(kinds))]
        p_i += len(kinds)
        seg_refs.append(pieces)
        if all(kind in ("full", "fullT") for kind in kinds):
            seg_bias.append(None)
        else:
            bias = cat([piece_bias(kind, kr.shape[0]) for kind, (kr, _) in zip(kinds, pieces)], 1)
            seg_bias.append(cat([bias] * group, 0))

    if seg_kinds in ((("full",),), (("fullT",),)):
        transposed = seg_kinds[0][0] == "fullT"
        kr, vr = seg_refs[0][0]
        contract = lambda a_dim, b_dim: (((a_dim,), (b_dim,)), ((), ()))
        logits = []
        for hk in range(kv_step):
            q = cat([q_ref[:, h * dk:(h + 1) * dk] for h in range(hk * group, (hk + 1) * group)], 0)
            k = kr[hk].astype(BF16) if transposed else kr[:, hk * dk:(hk + 1) * dk].astype(BF16)
            logits.append(lax.dot_general(k, q, contract(0 if transposed else 1, 1),
                                          preferred_element_type=F32))
        probs, denoms = [], []
        for hk, s_t in enumerate(logits):
            m = jnp.max(s_t, axis=0, keepdims=True)
            p_t = jnp.exp2((s_t - m) * (scale * LOG2_E))
            denom = jnp.sum(p_t, axis=0, keepdims=True)
            if has_sink:
                sink = cat([jnp.full((1, tq), sink_ref[first_head + h], F32)
                            for h in range(hk * group, (hk + 1) * group)], 1)
                denom = denom + jnp.exp2((sink - m * scale) * LOG2_E)
            probs.append(p_t.astype(BF16))
            denoms.append(denom)
        for hk in range(kv_step):
            v = vr[hk].astype(BF16) if transposed else vr[:, hk * HEAD_DIM:(hk + 1) * HEAD_DIM].astype(BF16)
            o_t = lax.dot_general(v, probs[hk], contract(1 if transposed else 0, 0),
                                  preferred_element_type=F32) * (1.0 / denoms[hk])
            for g_i in range(group):
                h = hk * group + g_i
                o_ref[:, h * HEAD_DIM:(h + 1) * HEAD_DIM] = o_t[:, g_i * tq:(g_i + 1) * tq].T.astype(o_ref.dtype)
        return

    for hk in range(kv_step):
        heads = range(hk * group, (hk + 1) * group)
        q = cat([q_ref[:, h * dk:(h + 1) * dk] for h in heads], 0)
        segs = []
        for kinds, pieces, bias in zip(seg_kinds, seg_refs, seg_bias):
            if kinds == ("fullT",):
                k = pieces[0][0][hk].astype(BF16)
                v = pieces[0][1][hk].astype(BF16)
                v1 = jnp.concatenate([v, jnp.ones_like(v)], axis=0)
                for c0 in range(0, k.shape[1], KEY_CHUNK):
                    segs.append((k[:, c0:c0 + KEY_CHUNK], v1[:, c0:c0 + KEY_CHUNK], None, True))
                continue
            k = cat([kr[:, hk * dk:(hk + 1) * dk].astype(BF16) for kr, _ in pieces], 0)
            v = cat([vr[:, hk * HEAD_DIM:(hk + 1) * HEAD_DIM].astype(BF16) for _, vr in pieces], 0)
            v1 = jnp.concatenate([v, jnp.ones_like(v)], axis=1)
            for c0 in range(0, k.shape[0], KEY_CHUNK):
                chunk = slice(c0, c0 + KEY_CHUNK)
                segs.append((k[chunk], v1[chunk], None if bias is None else bias[:, chunk], False))
        sink = (cat([jnp.full((tq, 2 * HEAD_DIM), sink_ref[first_head + h], F32) for h in heads], 0)
                if has_sink else None)
        out = _online_softmax(q, segs, scale, sink)
        for g_i, h in enumerate(heads):
            o_ref[:, h * HEAD_DIM:(h + 1) * HEAD_DIM] = out[g_i * tq:(g_i + 1) * tq].astype(o_ref.dtype)


def _attention(q, segs, *, n_batch, n_tok, tq, n_kv, kv_step, dk, scale, sink=None):
    n_blk = n_tok // tq
    per_blk = tq // A_WINDOW
    n_win = n_tok // A_WINDOW
    group = N_HEADS // n_kv
    in_specs = [pl.BlockSpec((tq, kv_step * group * dk), lambda b, g, i: (b * n_blk + i, g))]
    args = [q]
    for pieces in segs:
        for k, v, rows, kind in pieces:
            if kind == "fullT":
                idx = lambda b, g, i: (b, g, 0, 0)
                in_specs += [pl.BlockSpec((None, kv_step, dk, rows), idx),
                             pl.BlockSpec((None, kv_step, HEAD_DIM, rows), idx)]
                args += [k, v]
                continue
            if kind == "full":
                idx = lambda b, g, i: (b, g)
            elif kind == "band":
                idx = lambda b, g, i: (b * n_blk + i, g)
            elif kind == "prev":
                idx = lambda b, g, i: (b * n_win + jnp.maximum(i * per_blk - 1, 0), g)
            else:
                idx = lambda b, g, i: (b * n_win + jnp.minimum((i + 1) * per_blk, n_win - 1), g)
            in_specs += [pl.BlockSpec((rows, kv_step * dk), idx), pl.BlockSpec((rows, kv_step * HEAD_DIM), idx)]
            args += [k, v]
    seg_kinds = tuple(tuple(kind for _, _, _, kind in pieces) for pieces in segs)
    if sink is not None:
        in_specs.append(pl.BlockSpec(memory_space=pltpu.SMEM))
        args.append(sink)
    return pl.pallas_call(
        functools.partial(_attn_kernel, kv_step=kv_step, group=group, dk=dk, scale=scale,
                          seg_kinds=seg_kinds, has_sink=sink is not None, tq=tq),
        grid=(n_batch, n_kv // kv_step, n_blk),
        in_specs=in_specs,
        out_specs=pl.BlockSpec((tq, kv_step * group * HEAD_DIM), lambda b, g, i: (b * n_blk + i, g)),
        out_shape=jax.ShapeDtypeStruct((n_batch * n_tok, N_HEADS * HEAD_DIM), BF16),
        compiler_params=_params(3),
        name=f"attn_{n_batch}x{n_tok}_kv{n_kv}_dk{dk}_" + "_".join(k[0] for ks in seg_kinds for k in ks),
    )(*args)


def _na_row_window(r):
    rs = min(max(r - NA_ROWS // 2, 0), GRID_ROWS - NA_ROWS)
    ws = min(rs - rs % 2, GRID_ROWS - NA_WIN_ROWS)
    return ws, r - ws, rs - ws


def _na_kernel(q_ref, k_ref, v_ref, kc_ref, vc_ref, rows_ref, o_ref,
               s_ref, sc_ref, p_ref, pc_ref, l_ref, bias_ref):
    dims = (((1,), (1,)), ((), ()))

    @pl.when((pl.program_id(0) == 0) & (pl.program_id(1) == 0))
    def _():
        p_ref[...] = jnp.zeros_like(p_ref)

    @pl.when(pl.program_id(1) == 0)
    def _():
        _na_build_bias(rows_ref, bias_ref)

    for hh in range(LANES // HEAD_DIM):
        cols = slice(hh * HEAD_DIM, (hh + 1) * HEAD_DIM)
        q = q_ref[:, cols]
        s_ref[hh] = lax.dot_general(q, k_ref[:, cols], dims, preferred_element_type=F32)
        sc_ref[hh] = jnp.dot(q, kc_ref[hh].astype(BF16), preferred_element_type=F32)
        for r in range(GRID_ROWS):
            ws, slot, _ = _na_row_window(r)
            rows = slice(r * GRID_W, (r + 1) * GRID_W)
            win = slice(ws * GRID_W, (ws + NA_WIN_ROWS) * GRID_W)
            s_n = s_ref[hh, rows, win] + bias_ref[hh, slot]
            s_c = sc_ref[hh, rows, :]
            m = jnp.maximum(jnp.max(s_n, axis=-1, keepdims=True), jnp.max(s_c, axis=-1, keepdims=True))
            p_n = jnp.exp(s_n - m)
            p_c = jnp.exp(s_c - m)
            l_ref[hh, rows, :] = jnp.sum(p_n, axis=-1, keepdims=True) + jnp.sum(p_c, axis=-1, keepdims=True)
            p_ref[hh, rows, win] = p_n.astype(BF16)
            pc_ref[hh, rows, :] = p_c.astype(BF16)
        out = (jnp.dot(p_ref[hh], v_ref[:, cols], preferred_element_type=F32)
               + lax.dot_general(pc_ref[hh], vc_ref[hh].astype(BF16), dims, preferred_element_type=F32))
        o_ref[:, cols] = (out * (1.0 / l_ref[hh])).astype(o_ref.dtype)


def _na_build_bias(rows_ref, bias_ref):
    first_visible = {slot: first for _, slot, first in map(_na_row_window, range(GRID_ROWS))}
    assert sorted(first_visible) == list(range(NA_SLOTS))
    qcol = lax.broadcasted_iota(jnp.int32, (GRID_W, GRID_W), 0)
    kcol = lax.broadcasted_iota(jnp.int32, (GRID_W, GRID_W), 1)
    cs = jnp.clip(qcol - NA_COLS // 2, 0, GRID_W - NA_COLS)
    valid = (kcol >= cs) & (kcol < cs + NA_COLS)
    masked = jnp.full((GRID_W, GRID_W), NEG_INF, F32)
    for hh in range(LANES // HEAD_DIM):
        toep = []
        for ri in range(2 * NA_ROWS - 1):
            rows = jnp.broadcast_to(rows_ref[hh, ri:ri + 1, :], (GRID_W, LANES))
            rolled = pltpu.roll(rows, 0, 1, stride=1, stride_axis=0)
            toep.append(jnp.where(valid, rolled[:, :GRID_W], NEG_INF))
        for d in range(NA_SLOTS):
            for j0 in range(0, NA_WIN_ROWS, LANES // GRID_W):
                pieces = []
                for j in range(j0, j0 + LANES // GRID_W):
                    visible = first_visible[d] <= j < first_visible[d] + NA_ROWS
                    pieces.append(toep[j - d + NA_ROWS - 1] if visible else masked)
                bias_ref[hh, d, :, j0 * GRID_W:j0 * GRID_W + LANES] = jnp.concatenate(pieces, axis=1)


def _na_bias_rows(rpb):
    n_h, n_ri, n_ci = rpb.shape
    return jnp.concatenate([rpb[..., NA_COLS - 1:], jnp.zeros((n_h, n_ri, LANES - n_ci), F32),
                            rpb[..., :NA_COLS - 1]], axis=-1)


def _na_attention(q, k, v, kc, vc, rpb):
    n_pairs = N_HEADS * HEAD_DIM // LANES
    lat = pl.BlockSpec((DEC_SEQ, LANES), lambda hp, b: (b, hp))
    ctx = pl.BlockSpec((None, LANES // HEAD_DIM, HEAD_DIM, PAST_LEN), lambda hp, b: (b, hp, 0, 0))
    return pl.pallas_call(
        _na_kernel,
        grid=(n_pairs, DEC_BATCH),
        in_specs=[lat, lat, lat, ctx, ctx,
                  pl.BlockSpec((LANES // HEAD_DIM, 2 * NA_ROWS - 1, LANES), lambda hp, b: (hp, 0, 0))],
        out_specs=lat,
        out_shape=jax.ShapeDtypeStruct((DEC_BATCH * DEC_SEQ, N_HEADS * HEAD_DIM), BF16),
        scratch_shapes=[pltpu.VMEM((LANES // HEAD_DIM, DEC_SEQ, DEC_SEQ), F32),
                        pltpu.VMEM((LANES // HEAD_DIM, DEC_SEQ, PAST_LEN), F32),
                        pltpu.VMEM((LANES // HEAD_DIM, DEC_SEQ, DEC_SEQ), BF16),
                        pltpu.VMEM((LANES // HEAD_DIM, DEC_SEQ, PAST_LEN), BF16),
                        pltpu.VMEM((LANES // HEAD_DIM, DEC_SEQ, 1), F32),
                        pltpu.VMEM((LANES // HEAD_DIM, NA_SLOTS, GRID_W, NA_WIN_KEYS), F32)],
        compiler_params=_params(2),
        name="na_attn",
    )(q, k, v, kc, vc, _na_bias_rows(rpb))


def _grid_angles(n, rot_dim):
    pos = jnp.arange(n, dtype=jnp.int32)
    row = (pos // GRID_W).astype(F32)
    col = (pos % GRID_W).astype(F32)
    n_ax = rot_dim // 4
    inv = ROPE_THETA ** (-jnp.arange(n_ax, dtype=F32) / n_ax)
    return jnp.concatenate([row[:, None] * inv, col[:, None] * inv], axis=-1)


def _pair_tables(ang):
    cos = jnp.repeat(jnp.cos(ang), 2, axis=-1)
    sin = jnp.stack([-jnp.sin(ang), jnp.sin(ang)], axis=-1).reshape(ang.shape[0], -1)
    return cos, sin


def _gqa_rope_tables():
    cos, sin = _pair_tables(_grid_angles(DEC_SEQ, HEAD_DIM))
    return jnp.tile(cos, (1, LANES // HEAD_DIM)), jnp.tile(sin, (1, LANES // HEAD_DIM))


def _mla_rope_tables():
    cos, sin = _pair_tables(_grid_angles(DEC_SEQ, C_ROPE))
    pad = LANES - C_QK
    cos = jnp.concatenate([jnp.ones((DEC_SEQ, C_NOPE), F32), cos, jnp.ones((DEC_SEQ, pad), F32)], axis=-1)
    sin = jnp.concatenate([jnp.zeros((DEC_SEQ, C_NOPE), F32), sin, jnp.zeros((DEC_SEQ, pad), F32)], axis=-1)
    return cos, sin


def _mla_weights(w_down, q_lnorm, kv_lnorm, w_uq, w_ukv, q_norm, k_norm):
    pad_head = lambda a: jnp.pad(a, [(0, 0)] * (a.ndim - 1) + [(0, C_HEAD_PAD - a.shape[-1])])
    ukv = w_ukv.reshape(C_KV_RANK, N_HEADS, C_NOPE + C_V)
    down = jnp.pad(w_down, ((0, 0), (0, C_DOWN_PAD - w_down.shape[1]))).astype(BF16)
    uq = pad_head(w_uq.reshape(C_Q_RANK, N_HEADS, C_QK)).reshape(C_Q_RANK, -1).astype(BF16)
    return {
        "down": down,
        "down_rope": jnp.concatenate([down, _swap_pair_columns(down[:, C_DOWN_PAD - LANES:])], axis=1),
        "q_lnorm": q_lnorm.reshape(1, C_Q_RANK),
        "kv_lnorm": kv_lnorm.reshape(1, C_KV_RANK),
        "uq": uq,
        "uq_sw": _swap_pair_columns(uq),
        "uk": pad_head(ukv[:, :, :C_NOPE]).reshape(C_KV_RANK, -1).astype(BF16),
        "uv": ukv[:, :, C_NOPE:].reshape(C_KV_RANK, -1).astype(BF16),
        "q_norm": pad_head(q_norm).reshape(1, C_HEAD_PAD),
        "k_norm": pad_head(k_norm).reshape(1, C_HEAD_PAD),
    }


def kernel(x_prompt, x_sample, cache_a_k, cache_a_v, cache_b_k, cache_b_v, cache_c_kv, cache_c_krope, cache_d_k, cache_d_v, c, c_ctx, mod_w, mod_b, norm_ff1, norm_mix, norm_ff2, ff1_w_gu, ff1_w_down, ff2_w_gu, ff2_w_down, a_w_qkv, a_q_norm, a_k_norm, a_sink, a_w_o, b_w_qkv, b_q_norm, b_k_norm, b_w_o, c_w_down, c_q_lnorm, c_kv_lnorm, c_w_uq, c_w_ukv, c_q_norm, c_k_norm, c_w_o, d_w_qkv, d_q_norm, d_k_norm, d_rpb, d_w_o):
    n_p, n_s = BATCH * SEQ, DEC_BATCH * DEC_SEQ
    xp = x_prompt.reshape(n_p, D_MODEL)
    xs = x_sample.reshape(n_s, D_MODEL)
    cond = jnp.concatenate([c_ctx[None], c, jnp.zeros((MOD_ROWS - 1 - DEC_BATCH, D_MODEL), F32)], axis=0)
    modv = _mod_table(cond, mod_w, mod_b)
    s_row = _sample_row(TOKEN_TILE)
    gqa_tabs = _gqa_rope_tables()
    gqa_scale = 1.0
    flat_cache = lambda a: a.reshape(DEC_BATCH * PAST_LEN, -1)
    keys_last = lambda a: jnp.transpose(a, (0, 2, 3, 1))
    keys_first = lambda a: jnp.transpose(a, (0, 3, 1, 2))[:, None]
    ffn_w = [(ff1_w_gu, ff1_w_down), (ff2_w_gu, ff2_w_down)]
    new = {}

    for i in range(DEPTH):
        kind, j = i % N_MIXERS, i // N_MIXERS
        xp = _half_ffn(xp, modv, i, 0, norm_ff1[i], *ffn_w[0], _prompt_row)
        xs = _half_ffn(xs, modv, i, 0, norm_ff1[i], *ffn_w[0], s_row)

        if kind in (0, 1, 3):
            w_qkv, q_norm, k_norm, w_o, n_kv, cache_k, cache_v = {
                0: (a_w_qkv, a_q_norm, a_k_norm, a_w_o, GQA_KV_HEADS, cache_a_k, cache_a_v),
                1: (b_w_qkv, b_q_norm, b_k_norm, b_w_o, GQA_KV_HEADS, cache_b_k, cache_b_v),
                3: (d_w_qkv, d_q_norm, d_k_norm, d_w_o, N_HEADS, cache_d_k, cache_d_v)}[kind]
            w_qkv = w_qkv[j].astype(BF16)
            qp, kp, vp = _gqa_project(xp, modv, i, norm_mix[i], w_qkv, q_norm[j], k_norm[j], n_kv,
                                      _prompt_row, None, True)
            qs, ks, vs = _gqa_project(xs, modv, i, norm_mix[i], w_qkv, q_norm[j], k_norm[j], n_kv,
                                      s_row, gqa_tabs if kind != 3 else None, False)
            op = _attention(qp, [[(kp, vp, SEQ, "fullT")]], n_batch=BATCH, n_tok=SEQ, tq=SEQ, n_kv=n_kv,
                            kv_step=n_kv, dk=HEAD_DIM, scale=gqa_scale, sink=a_sink[j] if kind == 0 else None)
            name = "abcd"[kind]
            new[name + "_k"], new[name + "_v"] = keys_first(kp), keys_first(vp)
            kc, vc = keys_last(cache_k[:, j]), keys_last(cache_v[:, j])
            if kind == 0:
                tq = 2 * A_WINDOW
                os_ = _attention(qs, [[(ks, vs, A_WINDOW, "prev"), (ks, vs, tq, "band"),
                                       (ks, vs, A_WINDOW, "next")], [(kc, vc, PAST_LEN, "fullT")]],
                                 n_batch=DEC_BATCH, n_tok=DEC_SEQ, tq=tq, n_kv=n_kv, kv_step=2, dk=HEAD_DIM,
                                 scale=gqa_scale, sink=a_sink[j])
            elif kind == 1:
                os_ = _attention(qs, [[(ks, vs, DEC_SEQ, "full")], [(kc, vc, PAST_LEN, "fullT")]],
                                 n_batch=DEC_BATCH, n_tok=DEC_SEQ, tq=512, n_kv=n_kv, kv_step=2, dk=HEAD_DIM,
                                 scale=gqa_scale)
            else:
                os_ = _na_attention(qs, ks, vs, kc, vc, d_rpb[j])
        else:
            w = _mla_weights(c_w_down[j], c_q_lnorm[j], c_kv_lnorm[j], c_w_uq[j], c_w_ukv[j],
                             c_q_norm[j], c_k_norm[j])
            w_o = c_w_o
            qp, kp, vp, ckv_p, kr_p = _mla_project(xp, modv, i, norm_mix[i], w, _prompt_row, None)
            qs, ks, vs, _, _ = _mla_project(xs, modv, i, norm_mix[i], w, s_row, _mla_rope_tables())
            new["c_kv"] = ckv_p.reshape(BATCH, 1, SEQ, C_KV_RANK)
            new["c_krope"] = kr_p[:, C_NOPE:C_QK].reshape(BATCH, 1, SEQ, C_ROPE)
            kr_cache = jnp.pad(flat_cache(cache_c_krope[:, j]), ((0, 0), (C_NOPE, LANES - C_QK)))
            kc, vc = _mla_expand_cache(flat_cache(cache_c_kv[:, j]), kr_cache, w)
            mla_scale = C_QK ** -0.5
            op = _attention(qp, [[(kp, vp, SEQ, "full")]], n_batch=BATCH, n_tok=SEQ, tq=SEQ,
                            n_kv=N_HEADS, kv_step=N_HEADS, dk=C_HEAD_PAD, scale=mla_scale)
            os_ = _attention(qs, [[(ks, vs, DEC_SEQ, "full")], [(kc, vc, PAST_LEN, "full")]],
                             n_batch=DEC_BATCH, n_tok=DEC_SEQ, tq=DEC_SEQ, n_kv=N_HEADS, kv_step=2, dk=C_HEAD_PAD,
                             scale=mla_scale)

        w_o = w_o[j].astype(BF16)
        xp = _half_ffn(xp, modv, i, 6, norm_ff2[i], *ffn_w[1], _prompt_row, mixer_out=(op, w_o))
        xs = _half_ffn(xs, modv, i, 6, norm_ff2[i], *ffn_w[1], s_row, mixer_out=(os_, w_o))

    return (xp.reshape(BATCH, SEQ, D_MODEL), xs.reshape(DEC_BATCH, DEC_SEQ, D_MODEL),
            new["a_k"], new["a_v"], new["b_k"], new["b_v"], new["c_kv"], new["c_krope"],
            new["d_k"], new["d_v"])
```

```python
import functools

import jax
import jax.numpy as jnp
from jax import lax
from jax.experimental import pallas as pl
from jax.experimental.pallas import tpu as pltpu

F32 = jnp.float32
BF16 = jnp.bfloat16

D_MODEL = 1024
BATCH = 16
SEQ = 256
DEPTH = 4
DEC_BATCH = 8
DEC_SEQ = 1024
PAST_LEN = 512
GRID_W = 64
N_MIXERS = 4
RMS_EPS = 1e-6
ROPE_THETA = 10000.0
NEG_INF = -1e30
D_FF = 2816
FFN_RES = 0.5
N_MOD = 9
HEAD_DIM = 64
N_HEADS = 16
GQA_KV_HEADS = 4
A_WINDOW = 128
C_Q_RANK = 384
C_KV_RANK = 256
C_NOPE = 64
C_ROPE = 32
C_V = 64
C_QK = C_NOPE + C_ROPE
NA_ROWS = 8
NA_COLS = 16

LANES = 128
MOD_ROWS = 16
C_DOWN_PAD = 768
C_HEAD_PAD = LANES
VMEM_LIMIT_BYTES = 56 * 1024 * 1024
TOKEN_TILE = 512
FF_CHUNK = 256
N_WEIGHT_SLOTS = 2
KEY_CHUNK = 1024
LOG2_E = 1.4426950408889634
GQA_SCALE = HEAD_DIM ** -0.5
GRID_ROWS = DEC_SEQ // GRID_W
NA_WIN_ROWS = NA_ROWS + 2
NA_WIN_KEYS = NA_WIN_ROWS * GRID_W
NA_SLOTS = NA_WIN_ROWS


def _params(n_axes):
    return pltpu.CompilerParams(dimension_semantics=("arbitrary",) * n_axes,
                                vmem_limit_bytes=VMEM_LIMIT_BYTES)


def _resident(shape):
    nd = len(shape)
    return pl.BlockSpec(shape, lambda *_: (0,) * nd, pipeline_mode=pl.Buffered(1))


def _mod_spec(layer, part, row_fn):
    base = (layer * N_MOD + part) * MOD_ROWS
    return pl.BlockSpec((None, 1, D_MODEL), lambda i: (base + row_fn(i), 0, 0))


def _prompt_row(i):
    return 0


def _sample_row(tile):
    per_batch = DEC_SEQ // tile
    return lambda i: 1 + i // per_batch


def _silu(a):
    return a * (1.0 / (1.0 + jnp.exp(-a)))


def _modulate(x, g, shift, scale):
    y = x * lax.rsqrt(jnp.mean(x * x, axis=-1, keepdims=True) + RMS_EPS)
    return (y * g) * (1.0 + scale) + shift


def _swap_pair_columns(a):
    even = jnp.arange(a.shape[-1]) % 2 == 0
    return jnp.where(even, jnp.roll(a, -1, axis=-1), jnp.roll(a, 1, axis=-1))


def _mod_kernel(c_ref, w_ref, b_ref, o_ref):
    s = _silu(c_ref[...]).astype(BF16)
    o_ref[...] = jnp.dot(s, w_ref[...].astype(BF16), preferred_element_type=F32) + b_ref[...]


def _mod_table(cond, mod_w, mod_b):
    out = pl.pallas_call(
        _mod_kernel,
        grid=(DEPTH, N_MOD),
        in_specs=[
            pl.BlockSpec((MOD_ROWS, D_MODEL), lambda l, j: (0, 0)),
            pl.BlockSpec((None, D_MODEL, D_MODEL), lambda l, j: (l, 0, j)),
            pl.BlockSpec((None, 1, D_MODEL), lambda l, j: (l, 0, j)),
        ],
        out_specs=pl.BlockSpec((None, None, MOD_ROWS, D_MODEL), lambda l, j: (l, j, 0, 0)),
        out_shape=jax.ShapeDtypeStruct((DEPTH, N_MOD, MOD_ROWS, D_MODEL), F32),
        compiler_params=_params(2),
        name="mod_table",
    )(cond, mod_w, mod_b.reshape(DEPTH, 1, N_MOD * D_MODEL))
    return out.reshape(DEPTH * N_MOD * MOD_ROWS, 1, D_MODEL)


def _ffn_kernel(*refs, mixer_out, layer):
    if mixer_out:
        x_ref, attn_ref, gm_ref, wo_ref = refs[:4]
        refs = refs[4:]
    else:
        x_ref = refs[0]
        refs = refs[1:]
    (sh_ref, sc_ref, gt_ref, g_ref, wgu_hbm, wd_hbm, o_ref,
     act_ref, wgu_ref, wd_ref, gate_stage, up_stage, down_stage, sems) = refs
    n_chunks = D_FF // FF_CHUNK

    def residual_in():
        if not mixer_out:
            return x_ref[...]
        return x_ref[...] + gm_ref[...] * jnp.dot(attn_ref[...], wo_ref[...], preferred_element_type=F32)

    def chunk_copies(c, slot):
        c0 = c * FF_CHUNK
        return (pltpu.make_async_copy(wgu_hbm.at[layer, :, pl.ds(c0, FF_CHUNK)], gate_stage.at[slot],
                                      sems.at[0, slot]),
                pltpu.make_async_copy(wgu_hbm.at[layer, :, pl.ds(D_FF + c0, FF_CHUNK)], up_stage.at[slot],
                                      sems.at[1, slot]),
                pltpu.make_async_copy(wd_hbm.at[layer, pl.ds(c0, FF_CHUNK), :], down_stage.at[slot],
                                      sems.at[2, slot]))

    def fetch_chunk(c):
        slot = c % N_WEIGHT_SLOTS
        c0 = c * FF_CHUNK
        for copy in chunk_copies(c, slot):
            copy.wait()
        wgu_ref[:, c0:c0 + FF_CHUNK] = gate_stage[slot].astype(BF16)
        wgu_ref[:, D_FF + c0:D_FF + c0 + FF_CHUNK] = up_stage[slot].astype(BF16)
        wd_ref[c0:c0 + FF_CHUNK, :] = down_stage[slot].astype(BF16)
        if c + N_WEIGHT_SLOTS < n_chunks:
            for copy in chunk_copies(c + N_WEIGHT_SLOTS, slot):
                copy.start()

    def tile(stream_weights):
        if stream_weights:
            for c in range(min(N_WEIGHT_SLOTS, n_chunks)):
                for copy in chunk_copies(c, c):
                    copy.start()
        x = residual_in()
        h = _modulate(x, g_ref[...], sh_ref[...], sc_ref[...]).astype(BF16)
        for c in range(n_chunks):
            if stream_weights:
                fetch_chunk(c)
            c0 = c * FF_CHUNK
            a = jnp.dot(h, wgu_ref[:, c0:c0 + FF_CHUNK], preferred_element_type=F32)
            u = jnp.dot(h, wgu_ref[:, D_FF + c0:D_FF + c0 + FF_CHUNK], preferred_element_type=F32)
            act_ref[:, c0:c0 + FF_CHUNK] = (_silu(a) * u).astype(BF16)
        y = jnp.dot(act_ref[...], wd_ref[...], preferred_element_type=F32)
        o_ref[...] = x + (FFN_RES * gt_ref[...]) * y

    @pl.when(pl.program_id(0) == 0)
    def _():
        tile(True)

    @pl.when(pl.program_id(0) > 0)
    def _():
        tile(False)


def _half_ffn(x, modv, layer, part0, g, wgu, wd, row_fn, mixer_out=None):
    n_tok = x.shape[0]
    tile = pl.BlockSpec((TOKEN_TILE, D_MODEL), lambda i: (i, 0))
    in_specs, args = [tile], [x]
    if mixer_out is not None:
        in_specs += [tile, _mod_spec(layer, 5, row_fn), _resident((D_MODEL, D_MODEL))]
        args += [mixer_out[0], modv, mixer_out[1]]
    in_specs += [_mod_spec(layer, part0, row_fn), _mod_spec(layer, part0 + 1, row_fn),
                 _mod_spec(layer, part0 + 2, row_fn),
                 _resident((1, D_MODEL)), pl.BlockSpec(memory_space=pl.ANY), pl.BlockSpec(memory_space=pl.ANY)]
    args += [modv, modv, modv, g.reshape(1, D_MODEL), wgu, wd]
    return pl.pallas_call(
        functools.partial(_ffn_kernel, mixer_out=mixer_out is not None, layer=layer),
        grid=(n_tok // TOKEN_TILE,),
        in_specs=in_specs,
        out_specs=tile,
        out_shape=jax.ShapeDtypeStruct((n_tok, D_MODEL), F32),
        scratch_shapes=[pltpu.VMEM((TOKEN_TILE, D_FF), BF16),
                        pltpu.VMEM((D_MODEL, 2 * D_FF), BF16),
                        pltpu.VMEM((D_FF, D_MODEL), BF16),
                        pltpu.VMEM((N_WEIGHT_SLOTS, D_MODEL, FF_CHUNK), F32),
                        pltpu.VMEM((N_WEIGHT_SLOTS, D_MODEL, FF_CHUNK), F32),
                        pltpu.VMEM((N_WEIGHT_SLOTS, FF_CHUNK, D_MODEL), F32),
                        pltpu.SemaphoreType.DMA((3, N_WEIGHT_SLOTS))],
        compiler_params=_params(1),
        name=f"ffn_{n_tok}" + ("_mix" if mixer_out is not None else ""),
    )(*args)


def _head_pair_rstd(yp, lo):
    sq = yp * yp
    s_lo = jnp.sum(jnp.where(lo, sq, 0.0), axis=-1, keepdims=True)
    s_hi = jnp.sum(jnp.where(lo, 0.0, sq), axis=-1, keepdims=True)
    ms = jnp.where(lo, s_lo, s_hi) * (1.0 / HEAD_DIM)
    return lax.rsqrt(ms + RMS_EPS)


def _gqa_proj_kernel(*refs, n_q, n_kv, rope, kv_transposed):
    x_ref, sh_ref, sc_ref, g_ref, w_ref, qn_ref, kn_ref = refs[:7]
    if rope:
        wsw_ref, qc_ref, qs_ref, kc_ref, ks_ref = refs[7:12]
        q_ref, k_ref, v_ref = refs[12:]
    else:
        q_ref, k_ref, v_ref = refs[7:]
    h = _modulate(x_ref[...], g_ref[...], sh_ref[...], sc_ref[...]).astype(BF16)
    lane = lax.broadcasted_iota(jnp.int32, (1, LANES), 1)
    lo = lane < HEAD_DIM
    q_cols = n_q * HEAD_DIM
    k_cols = n_kv * HEAD_DIM

    def store(out_ref, o0, y, transposed):
        if not transposed:
            out_ref[:, o0:o0 + LANES] = y.astype(out_ref.dtype)
            return
        yt = y.T
        for b_i in range(y.shape[0] // SEQ):
            for h_i in range(LANES // HEAD_DIM):
                out_ref[b_i, o0 // HEAD_DIM + h_i] = yt[h_i * HEAD_DIM:(h_i + 1) * HEAD_DIM,
                                                        b_i * SEQ:(b_i + 1) * SEQ]

    def normed(c0, gain_ref, tabs, out_ref, o0, post_scale, transposed):
        y = jnp.dot(h, w_ref[:, c0:c0 + 2 * LANES], preferred_element_type=F32)
        if rope:
            y_sw = jnp.dot(h, wsw_ref[:, c0:c0 + 2 * LANES], preferred_element_type=F32)
        for p in range(2):
            part = slice(p * LANES, (p + 1) * LANES)
            rstd = _head_pair_rstd(y[:, part], lo)
            if rope:
                yn = (y[:, part] * tabs[0][...] + y_sw[:, part] * tabs[1][...]) * rstd
            else:
                yn = (y[:, part] * rstd) * gain_ref[...]
            if post_scale is not None:
                yn = yn * post_scale
            store(out_ref, o0 + p * LANES, yn, transposed)

    for c0 in range(0, q_cols, 2 * LANES):
        normed(c0, qn_ref, (qc_ref, qs_ref) if rope else None, q_ref, c0, GQA_SCALE, False)
    for c0 in range(0, k_cols, 2 * LANES):
        normed(q_cols + c0, kn_ref, (kc_ref, ks_ref) if rope else None, k_ref, c0, None, kv_transposed)
    for c0 in range(0, k_cols, 2 * LANES):
        v = jnp.dot(h, w_ref[:, q_cols + k_cols + c0:q_cols + k_cols + c0 + 2 * LANES],
                    preferred_element_type=F32)
        for p in range(2):
            store(v_ref, c0 + p * LANES, v[:, p * LANES:(p + 1) * LANES], kv_transposed)


def _gqa_project(x, modv, layer, g_mix, w_qkv, q_norm, k_norm, n_kv, row_fn, rope_tabs, kv_transposed):
    n_tok = x.shape[0]
    q_cols, k_cols = N_HEADS * HEAD_DIM, n_kv * HEAD_DIM
    rope = rope_tabs is not None
    if kv_transposed:
        per_tile = TOKEN_TILE // SEQ
        kv_spec = pl.BlockSpec((per_tile, n_kv, HEAD_DIM, SEQ), lambda i: (i, 0, 0, 0))
        kv_shape = jax.ShapeDtypeStruct((n_tok // SEQ, n_kv, HEAD_DIM, SEQ), F32)
    else:
        kv_spec = pl.BlockSpec((TOKEN_TILE, k_cols), lambda i: (i, 0))
        kv_shape = jax.ShapeDtypeStruct((n_tok, k_cols), BF16)
    tile = lambda w: pl.BlockSpec((TOKEN_TILE, w), lambda i: (i, 0))
    in_specs = [tile(D_MODEL), _mod_spec(layer, 3, row_fn), _mod_spec(layer, 4, row_fn),
                _resident((1, D_MODEL)), _resident((D_MODEL, q_cols + 2 * k_cols)),
                _resident((1, LANES)), _resident((1, LANES))]
    pair_tile = lambda g: jnp.tile(g, LANES // HEAD_DIM).reshape(1, LANES)
    args = [x, modv, modv, g_mix.reshape(1, D_MODEL), w_qkv, pair_tile(q_norm), pair_tile(k_norm)]
    if rope:
        per_batch = DEC_SEQ // TOKEN_TILE
        tab = pl.BlockSpec((TOKEN_TILE, LANES), lambda i: (i % per_batch, 0))
        cos, sin = rope_tabs
        in_specs += [_resident((D_MODEL, q_cols + k_cols)), tab, tab, tab, tab]
        args += [_swap_pair_columns(w_qkv[:, :q_cols + k_cols]),
                 cos * pair_tile(q_norm), sin * pair_tile(_swap_pair_columns(q_norm)),
                 cos * pair_tile(k_norm), sin * pair_tile(_swap_pair_columns(k_norm))]
    return pl.pallas_call(
        functools.partial(_gqa_proj_kernel, n_q=N_HEADS, n_kv=n_kv, rope=rope, kv_transposed=kv_transposed),
        grid=(n_tok // TOKEN_TILE,),
        in_specs=in_specs,
        out_specs=[tile(q_cols), kv_spec, kv_spec],
        out_shape=[jax.ShapeDtypeStruct((n_tok, q_cols), BF16), kv_shape, kv_shape],
        compiler_params=_params(1),
        name=f"gqa_proj_{n_tok}_kv{n_kv}",
    )(*args)


def _mla_group_rstd(y):
    ms = jnp.sum(y * y, axis=-1, keepdims=True) * (1.0 / C_QK)
    return lax.rsqrt(ms + RMS_EPS)


def _mla_expand_kv(ckv, kr128, wk_ref, wv_ref, kn_ref, rope, k_ref, v_ref):
    for c0 in range(0, N_HEADS * C_HEAD_PAD, 2 * LANES):
        y = jnp.dot(ckv, wk_ref[:, c0:c0 + 2 * LANES], preferred_element_type=F32)
        for p in range(2):
            kp = y[:, p * LANES:(p + 1) * LANES] + kr128
            rstd = _mla_group_rstd(kp)
            if rope is None:
                kn = (kp * rstd) * kn_ref[...]
            else:
                kn = (kp * rope[0][...] + rope[2] * rope[1][...]) * rstd
            k_ref[:, c0 + p * LANES:c0 + (p + 1) * LANES] = kn.astype(k_ref.dtype)
    for c0 in range(0, N_HEADS * C_V, 2 * LANES):
        v = jnp.dot(ckv, wv_ref[:, c0:c0 + 2 * LANES], preferred_element_type=F32)
        v_ref[:, c0:c0 + 2 * LANES] = v.astype(v_ref.dtype)


def _mla_proj_kernel(*refs, rope):
    (x_ref, sh_ref, sc_ref, g_ref, wd_ref, qln_ref, kvln_ref, wuq_ref, wk_ref, wv_ref,
     qn_ref, kn_ref) = refs[:12]
    if rope:
        wuq_sw_ref, qc_ref, qs_ref, kc_ref, ks_ref = refs[12:17]
        q_ref, k_ref, v_ref, ckv_ref, kr_ref = refs[17:]
    else:
        q_ref, k_ref, v_ref, ckv_ref, kr_ref = refs[12:]
    h = _modulate(x_ref[...], g_ref[...], sh_ref[...], sc_ref[...]).astype(BF16)
    y = jnp.dot(h, wd_ref[...], preferred_element_type=F32)

    def row_norm(z, gain):
        return (z * lax.rsqrt(jnp.mean(z * z, axis=-1, keepdims=True) + RMS_EPS)) * gain

    cq = row_norm(y[:, :C_Q_RANK], qln_ref[...]).astype(BF16)
    ckv = row_norm(y[:, C_Q_RANK:C_Q_RANK + C_KV_RANK], kvln_ref[...])
    kr128 = pltpu.roll(y[:, C_Q_RANK + C_KV_RANK:C_DOWN_PAD], C_NOPE, 1)
    ckv_ref[...] = ckv
    kr_ref[...] = kr128

    for c0 in range(0, N_HEADS * C_HEAD_PAD, 2 * LANES):
        yq = jnp.dot(cq, wuq_ref[:, c0:c0 + 2 * LANES], preferred_element_type=F32)
        if rope:
            yq_sw = jnp.dot(cq, wuq_sw_ref[:, c0:c0 + 2 * LANES], preferred_element_type=F32)
        for p in range(2):
            part = slice(p * LANES, (p + 1) * LANES)
            rstd = _mla_group_rstd(yq[:, part])
            if rope:
                qn = (yq[:, part] * qc_ref[...] + yq_sw[:, part] * qs_ref[...]) * rstd
            else:
                qn = (yq[:, part] * rstd) * qn_ref[...]
            q_ref[:, c0 + p * LANES:c0 + (p + 1) * LANES] = qn.astype(q_ref.dtype)
    k_rope = (kc_ref, ks_ref, pltpu.roll(y[:, C_DOWN_PAD:], C_NOPE, 1)) if rope else None
    _mla_expand_kv(ckv.astype(BF16), kr128, wk_ref, wv_ref, kn_ref, k_rope, k_ref, v_ref)


def _mla_project(x, modv, layer, g_mix, w, row_fn, rope_tabs):
    n_tok = x.shape[0]
    rope = rope_tabs is not None
    tile = lambda wd: pl.BlockSpec((TOKEN_TILE, wd), lambda i: (i, 0))
    qk_cols = N_HEADS * C_HEAD_PAD
    w_down = w["down_rope"] if rope else w["down"]
    in_specs = [tile(D_MODEL), _mod_spec(layer, 3, row_fn), _mod_spec(layer, 4, row_fn),
                _resident((1, D_MODEL)), _resident(w_down.shape),
                _resident((1, C_Q_RANK)), _resident((1, C_KV_RANK)),
                _resident((C_Q_RANK, qk_cols)), _resident((C_KV_RANK, qk_cols)),
                _resident((C_KV_RANK, N_HEADS * C_V)),
                _resident((1, LANES)), _resident((1, LANES))]
    args = [x, modv, modv, g_mix.reshape(1, D_MODEL), w_down, w["q_lnorm"], w["kv_lnorm"],
            w["uq"], w["uk"], w["uv"], w["q_norm"], w["k_norm"]]
    if rope:
        per_batch = DEC_SEQ // TOKEN_TILE
        tab = pl.BlockSpec((TOKEN_TILE, LANES), lambda i: (i % per_batch, 0))
        cos, sin = rope_tabs
        in_specs += [_resident((C_Q_RANK, qk_cols)), tab, tab, tab, tab]
        args += [w["uq_sw"], cos * w["q_norm"], sin * _swap_pair_columns(w["q_norm"]),
                 cos * w["k_norm"], sin * _swap_pair_columns(w["k_norm"])]
    return pl.pallas_call(
        functools.partial(_mla_proj_kernel, rope=rope),
        grid=(n_tok // TOKEN_TILE,),
        in_specs=in_specs,
        out_specs=[tile(qk_cols), tile(qk_cols), tile(N_HEADS * C_V), tile(C_KV_RANK), tile(LANES)],
        out_shape=[jax.ShapeDtypeStruct((n_tok, qk_cols), BF16),
                   jax.ShapeDtypeStruct((n_tok, qk_cols), BF16),
                   jax.ShapeDtypeStruct((n_tok, N_HEADS * C_V), BF16),
                   jax.ShapeDtypeStruct((n_tok, C_KV_RANK), F32),
                   jax.ShapeDtypeStruct((n_tok, LANES), F32)],
        compiler_params=_params(1),
        name=f"mla_proj_{n_tok}",
    )(*args)


def _mla_cache_kernel(ckv_ref, kr_ref, wk_ref, wv_ref, kn_ref, k_ref, v_ref):
    _mla_expand_kv(ckv_ref[...].astype(BF16), kr_ref[...], wk_ref, wv_ref, kn_ref, None, k_ref, v_ref)


def _mla_expand_cache(ckv, kr128, w):
    n_tok = ckv.shape[0]
    tile = lambda wd: pl.BlockSpec((TOKEN_TILE, wd), lambda i: (i, 0))
    qk_cols = N_HEADS * C_HEAD_PAD
    return pl.pallas_call(
        _mla_cache_kernel,
        grid=(n_tok // TOKEN_TILE,),
        in_specs=[tile(C_KV_RANK), tile(LANES), _resident((C_KV_RANK, qk_cols)),
                  _resident((C_KV_RANK, N_HEADS * C_V)), _resident((1, LANES))],
        out_specs=[tile(qk_cols), tile(N_HEADS * C_V)],
        out_shape=[jax.ShapeDtypeStruct((n_tok, qk_cols), BF16),
                   jax.ShapeDtypeStruct((n_tok, N_HEADS * C_V), BF16)],
        compiler_params=_params(1),
        name="mla_cache_kv",
    )(ckv, kr128, w["uk"], w["uv"], w["k_norm"])


def _online_softmax(q, segs, scale, sink):
    dims = (((1,), (1,)), ((), ()))
    to_exp2 = scale * LOG2_E
    m = None
    acc = None
    for k, v1, bias, transposed in segs:
        if transposed:
            s = jnp.dot(q, k, preferred_element_type=F32)
        else:
            s = lax.dot_general(q, k, dims, preferred_element_type=F32)
        if bias is not None:
            s = s + bias
        row_max = jnp.max(s, axis=-1, keepdims=True)
        m_new = row_max if m is None else jnp.maximum(m, row_max)
        p = jnp.exp2((s - m_new) * to_exp2).astype(BF16)
        if transposed:
            pv = lax.dot_general(p, v1, dims, preferred_element_type=F32)
        else:
            pv = jnp.dot(p, v1, preferred_element_type=F32)
        acc = pv if m is None else acc * jnp.exp2((m - m_new) * to_exp2) + pv
        m = m_new
    if sink is not None:
        lane = lax.broadcasted_iota(jnp.int32, (1, 2 * HEAD_DIM), 1)
        acc = acc + jnp.where(lane >= HEAD_DIM, jnp.exp2((sink - m * scale) * LOG2_E), 0.0)
    denom = pltpu.roll(acc, HEAD_DIM, 1)
    return (acc * (1.0 / denom))[:, :HEAD_DIM]


def _attn_kernel(*refs, kv_step, group, dk, scale, seg_kinds, has_sink, tq):
    n_piece = sum(len(kinds) for kinds in seg_kinds)
    q_ref = refs[0]
    kv_refs = refs[1:1 + 2 * n_piece]
    sink_ref = refs[1 + 2 * n_piece] if has_sink else None
    o_ref = refs[-1]
    first_head = pl.program_id(1) * (kv_step * group)
    blk = pl.program_id(2)
    n_blk = pl.num_programs(2)

    def cat(parts, axis):
        return parts[0] if len(parts) == 1 else jnp.concatenate(parts, axis=axis)

    def piece_bias(kind, n_keys):
        r = lax.broadcasted_iota(jnp.int32, (tq, n_keys), 0)
        c = lax.broadcasted_iota(jnp.int32, (tq, n_keys), 1)
        if kind == "prev":
            ok = (c >= r) & (blk > 0)
        elif kind == "next":
            ok = (c <= r - (tq - A_WINDOW)) & (blk < n_blk - 1)
        elif kind == "band":
            ok = jnp.abs(r - c) <= A_WINDOW
        else:
            return jnp.zeros((tq, n_keys), F32)
        return jnp.where(ok, 0.0, NEG_INF)

    seg_refs, seg_bias = [], []
    p_i = 0
    for kinds in seg_kinds:
        pieces = [(kv_refs[2 * (p_i + n)], kv_refs[2 * (p_i + n) + 1]) for n in range(len(kinds))]
        p_i += len(kinds)
        seg_refs.append(pieces)
        if all(kind in ("full", "fullT") for kind in kinds):
            seg_bias.append(None)
        else:
            bias = cat([piece_bias(kind, kr.shape[0]) for kind, (kr, _) in zip(kinds, pieces)], 1)
            seg_bias.append(cat([bias] * group, 0))

    if seg_kinds in ((("full",),), (("fullT",),)):
        transposed = seg_kinds[0][0] == "fullT"
        kr, vr = seg_refs[0][0]
        contract = lambda a_dim, b_dim: (((a_dim,), (b_dim,)), ((), ()))
        logits = []
        for hk in range(kv_step):
            q = cat([q_ref[:, h * dk:(h + 1) * dk] for h in range(hk * group, (hk + 1) * group)], 0)
            k = kr[hk].astype(BF16) if transposed else kr[:, hk * dk:(hk + 1) * dk].astype(BF16)
            logits.append(lax.dot_general(k, q, contract(0 if transposed else 1, 1),
                                          preferred_element_type=F32))
        probs, denoms = [], []
        for hk, s_t in enumerate(logits):
            m = jnp.max(s_t, axis=0, keepdims=True)
            p_t = jnp.exp2((s_t - m) * (scale * LOG2_E))
            denom = jnp.sum(p_t, axis=0, keepdims=True)
            if has_sink:
                sink = cat([jnp.full((1, tq), sink_ref[first_head + h], F32)
                            for h in range(hk * group, (hk + 1) * group)], 1)
                denom = denom + jnp.exp2((sink - m * scale) * LOG2_E)
            probs.append(p_t.astype(BF16))
            denoms.append(denom)
        for hk in range(kv_step):
            v = vr[hk].astype(BF16) if transposed else vr[:, hk * HEAD_DIM:(hk + 1) * HEAD_DIM].astype(BF16)
            o_t = lax.dot_general(v, probs[hk], contract(1 if transposed else 0, 0),
                                  preferred_element_type=F32) * (1.0 / denoms[hk])
            for g_i in range(group):
                h = hk * group + g_i
                o_ref[:, h * HEAD_DIM:(h + 1) * HEAD_DIM] = o_t[:, g_i * tq:(g_i + 1) * tq].T.astype(o_ref.dtype)
        return

    for hk in range(kv_step):
        heads = range(hk * group, (hk + 1) * group)
        q = cat([q_ref[:, h * dk:(h + 1) * dk] for h in heads], 0)
        segs = []
        for kinds, pieces, bias in zip(seg_kinds, seg_refs, seg_bias):
            if kinds == ("fullT",):
                k = pieces[0][0][hk].astype(BF16)
                v = pieces[0][1][hk].astype(BF16)
                v1 = jnp.concatenate([v, jnp.ones_like(v)], axis=0)
                for c0 in range(0, k.shape[1], KEY_CHUNK):
                    segs.append((k[:, c0:c0 + KEY_CHUNK], v1[:, c0:c0 + KEY_CHUNK], None, True))
                continue
            k = cat([kr[:, hk * dk:(hk + 1) * dk].astype(BF16) for kr, _ in pieces], 0)
            v = cat([vr[:, hk * HEAD_DIM:(hk + 1) * HEAD_DIM].astype(BF16) for _, vr in pieces], 0)
            v1 = jnp.concatenate([v, jnp.ones_like(v)], axis=1)
            for c0 in range(0, k.shape[0], KEY_CHUNK):
                chunk = slice(c0, c0 + KEY_CHUNK)
                segs.append((k[chunk], v1[chunk], None if bias is None else bias[:, chunk], False))
        sink = (cat([jnp.full((tq, 2 * HEAD_DIM), sink_ref[first_head + h], F32) for h in heads], 0)
                if has_sink else None)
        out = _online_softmax(q, segs, scale, sink)
        for g_i, h in enumerate(heads):
            o_ref[:, h * HEAD_DIM:(h + 1) * HEAD_DIM] = out[g_i * tq:(g_i + 1) * tq].astype(o_ref.dtype)


def _attention(q, segs, *, n_batch, n_tok, tq, n_kv, kv_step, dk, scale, sink=None):
    n_blk = n_tok // tq
    per_blk = tq // A_WINDOW
    n_win = n_tok // A_WINDOW
    group = N_HEADS // n_kv
    in_specs = [pl.BlockSpec((tq, kv_step * group * dk), lambda b, g, i: (b * n_blk + i, g))]
    args = [q]
    for pieces in segs:
        for k, v, rows, kind in pieces:
            if kind == "fullT":
                idx = lambda b, g, i: (b, g, 0, 0)
                in_specs += [pl.BlockSpec((None, kv_step, dk, rows), idx),
                             pl.BlockSpec((None, kv_step, HEAD_DIM, rows), idx)]
                args += [k, v]
                continue
            if kind == "full":
                idx = lambda b, g, i: (b, g)
            elif kind == "band":
                idx = lambda b, g, i: (b * n_blk + i, g)
            elif kind == "prev":
                idx = lambda b, g, i: (b * n_win + jnp.maximum(i * per_blk - 1, 0), g)
            else:
                idx = lambda b, g, i: (b * n_win + jnp.minimum((i + 1) * per_blk, n_win - 1), g)
            in_specs += [pl.BlockSpec((rows, kv_step * dk), idx), pl.BlockSpec((rows, kv_step * HEAD_DIM), idx)]
            args += [k, v]
    seg_kinds = tuple(tuple(kind for _, _, _, kind in pieces) for pieces in segs)
    if sink is not None:
        in_specs.append(pl.BlockSpec(memory_space=pltpu.SMEM))
        args.append(sink)
    return pl.pallas_call(
        functools.partial(_attn_kernel, kv_step=kv_step, group=group, dk=dk, scale=scale,
                          seg_kinds=seg_kinds, has_sink=sink is not None, tq=tq),
        grid=(n_batch, n_kv // kv_step, n_blk),
        in_specs=in_specs,
        out_specs=pl.BlockSpec((tq, kv_step * group * HEAD_DIM), lambda b, g, i: (b * n_blk + i, g)),
        out_shape=jax.ShapeDtypeStruct((n_batch * n_tok, N_HEADS * HEAD_DIM), BF16),
        compiler_params=_params(3),
        name=f"attn_{n_batch}x{n_tok}_kv{n_kv}_dk{dk}_" + "_".join(k[0] for ks in seg_kinds for k in ks),
    )(*args)


def _na_row_window(r):
    rs = min(max(r - NA_ROWS // 2, 0), GRID_ROWS - NA_ROWS)
    ws = min(rs - rs % 2, GRID_ROWS - NA_WIN_ROWS)
    return ws, r - ws, rs - ws


def _na_kernel(q_ref, k_ref, v_ref, kc_ref, vc_ref, rows_ref, o_ref,
               s_ref, sc_ref, p_ref, pc_ref, l_ref, bias_ref):
    dims = (((1,), (1,)), ((), ()))

    @pl.when((pl.program_id(0) == 0) & (pl.program_id(1) == 0))
    def _():
        p_ref[...] = jnp.zeros_like(p_ref)

    @pl.when(pl.program_id(1) == 0)
    def _():
        _na_build_bias(rows_ref, bias_ref)

    head_cols = lambda hh: slice(hh * HEAD_DIM, (hh + 1) * HEAD_DIM)

    def logits(hh):
        q = q_ref[:, head_cols(hh)]
        s_ref[hh] = lax.dot_general(q, k_ref[:, head_cols(hh)], dims, preferred_element_type=F32)
        sc_ref[hh] = jnp.dot(q, kc_ref[hh].astype(BF16), preferred_element_type=F32)

    def softmax_rows(hh):
        for r in range(GRID_ROWS):
            ws, slot, _ = _na_row_window(r)
            rows = slice(r * GRID_W, (r + 1) * GRID_W)
            win = slice(ws * GRID_W, (ws + NA_WIN_ROWS) * GRID_W)
            s_n = s_ref[hh, rows, win] + bias_ref[hh, slot]
            s_c = sc_ref[hh, rows, :]
            m = jnp.maximum(jnp.max(s_n, axis=-1, keepdims=True), jnp.max(s_c, axis=-1, keepdims=True))
            p_n = jnp.exp(s_n - m)
            p_c = jnp.exp(s_c - m)
            l_ref[hh, rows, :] = jnp.sum(p_n, axis=-1, keepdims=True) + jnp.sum(p_c, axis=-1, keepdims=True)
            p_ref[hh, rows, win] = p_n.astype(BF16)
            pc_ref[hh, rows, :] = p_c.astype(BF16)

    def values(hh):
        out = (jnp.dot(p_ref[hh], v_ref[:, head_cols(hh)], preferred_element_type=F32)
               + lax.dot_general(pc_ref[hh], vc_ref[hh].astype(BF16), dims, preferred_element_type=F32))
        o_ref[:, head_cols(hh)] = (out * (1.0 / l_ref[hh])).astype(o_ref.dtype)

    logits(0)
    logits(1)
    softmax_rows(0)
    values(0)
    softmax_rows(1)
    values(1)


def _na_build_bias(rows_ref, bias_ref):
    first_visible = {slot: first for _, slot, first in map(_na_row_window, range(GRID_ROWS))}
    assert sorted(first_visible) == list(range(NA_SLOTS))
    qcol = lax.broadcasted_iota(jnp.int32, (GRID_W, GRID_W), 0)
    kcol = lax.broadcasted_iota(jnp.int32, (GRID_W, GRID_W), 1)
    cs = jnp.clip(qcol - NA_COLS // 2, 0, GRID_W - NA_COLS)
    valid = (kcol >= cs) & (kcol < cs + NA_COLS)
    masked = jnp.full((GRID_W, GRID_W), NEG_INF, F32)
    for hh in range(LANES // HEAD_DIM):
        toep = []
        for ri in range(2 * NA_ROWS - 1):
            rows = jnp.broadcast_to(rows_ref[hh, ri:ri + 1, :], (GRID_W, LANES))
            rolled = pltpu.roll(rows, 0, 1, stride=1, stride_axis=0)
            toep.append(jnp.where(valid, rolled[:, :GRID_W], NEG_INF))
        for d in range(NA_SLOTS):
            for j0 in range(0, NA_WIN_ROWS, LANES // GRID_W):
                pieces = []
                for j in range(j0, j0 + LANES // GRID_W):
                    visible = first_visible[d] <= j < first_visible[d] + NA_ROWS
                    pieces.append(toep[j - d + NA_ROWS - 1] if visible else masked)
                bias_ref[hh, d, :, j0 * GRID_W:j0 * GRID_W + LANES] = jnp.concatenate(pieces, axis=1)


def _na_bias_rows(rpb):
    n_h, n_ri, n_ci = rpb.shape
    return jnp.concatenate([rpb[..., NA_COLS - 1:], jnp.zeros((n_h, n_ri, LANES - n_ci), F32),
                            rpb[..., :NA_COLS - 1]], axis=-1)


def _na_attention(q, k, v, kc, vc, rpb):
    n_pairs = N_HEADS * HEAD_DIM // LANES
    lat = pl.BlockSpec((DEC_SEQ, LANES), lambda hp, b: (b, hp))
    ctx = pl.BlockSpec((None, LANES // HEAD_DIM, HEAD_DIM, PAST_LEN), lambda hp, b: (b, hp, 0, 0))
    return pl.pallas_call(
        _na_kernel,
        grid=(n_pairs, DEC_BATCH),
        in_specs=[lat, lat, lat, ctx, ctx,
                  pl.BlockSpec((LANES // HEAD_DIM, 2 * NA_ROWS - 1, LANES), lambda hp, b: (hp, 0, 0))],
        out_specs=lat,
        out_shape=jax.ShapeDtypeStruct((DEC_BATCH * DEC_SEQ, N_HEADS * HEAD_DIM), BF16),
        scratch_shapes=[pltpu.VMEM((LANES // HEAD_DIM, DEC_SEQ, DEC_SEQ), F32),
                        pltpu.VMEM((LANES // HEAD_DIM, DEC_SEQ, PAST_LEN), F32),
                        pltpu.VMEM((LANES // HEAD_DIM, DEC_SEQ, DEC_SEQ), BF16),
                        pltpu.VMEM((LANES // HEAD_DIM, DEC_SEQ, PAST_LEN), BF16),
                        pltpu.VMEM((LANES // HEAD_DIM, DEC_SEQ, 1), F32),
                        pltpu.VMEM((LANES // HEAD_DIM, NA_SLOTS, GRID_W, NA_WIN_KEYS), F32)],
        compiler_params=_params(2),
        name="na_attn",
    )(q, k, v, kc, vc, _na_bias_rows(rpb))


def _grid_angles(n, rot_dim):
    pos = jnp.arange(n, dtype=jnp.int32)
    row = (pos // GRID_W).astype(F32)
    col = (pos % GRID_W).astype(F32)
    n_ax = rot_dim // 4
    inv = ROPE_THETA ** (-jnp.arange(n_ax, dtype=F32) / n_ax)
    return jnp.concatenate([row[:, None] * inv, col[:, None] * inv], axis=-1)


def _pair_tables(ang):
    cos = jnp.repeat(jnp.cos(ang), 2, axis=-1)
    sin = jnp.stack([-jnp.sin(ang), jnp.sin(ang)], axis=-1).reshape(ang.shape[0], -1)
    return cos, sin


def _gqa_rope_tables():
    cos, sin = _pair_tables(_grid_angles(DEC_SEQ, HEAD_DIM))
    return jnp.tile(cos, (1, LANES // HEAD_DIM)), jnp.tile(sin, (1, LANES // HEAD_DIM))


def _mla_rope_tables():
    cos, sin = _pair_tables(_grid_angles(DEC_SEQ, C_ROPE))
    pad = LANES - C_QK
    cos = jnp.concatenate([jnp.ones((DEC_SEQ, C_NOPE), F32), cos, jnp.ones((DEC_SEQ, pad), F32)], axis=-1)
    sin = jnp.concatenate([jnp.zeros((DEC_SEQ, C_NOPE), F32), sin, jnp.zeros((DEC_SEQ, pad), F32)], axis=-1)
    return cos, sin


def _mla_weights(w_down, q_lnorm, kv_lnorm, w_uq, w_ukv, q_norm, k_norm):
    pad_head = lambda a: jnp.pad(a, [(0, 0)] * (a.ndim - 1) + [(0, C_HEAD_PAD - a.shape[-1])])
    ukv = w_ukv.reshape(C_KV_RANK, N_HEADS, C_NOPE + C_V)
    down = jnp.pad(w_down, ((0, 0), (0, C_DOWN_PAD - w_down.shape[1]))).astype(BF16)
    uq = pad_head(w_uq.reshape(C_Q_RANK, N_HEADS, C_QK)).reshape(C_Q_RANK, -1).astype(BF16)
    return {
        "down": down,
        "down_rope": jnp.concatenate([down, _swap_pair_columns(down[:, C_DOWN_PAD - LANES:])], axis=1),
        "q_lnorm": q_lnorm.reshape(1, C_Q_RANK),
        "kv_lnorm": kv_lnorm.reshape(1, C_KV_RANK),
        "uq": uq,
        "uq_sw": _swap_pair_columns(uq),
        "uk": pad_head(ukv[:, :, :C_NOPE]).reshape(C_KV_RANK, -1).astype(BF16),
        "uv": ukv[:, :, C_NOPE:].reshape(C_KV_RANK, -1).astype(BF16),
        "q_norm": pad_head(q_norm).reshape(1, C_HEAD_PAD),
        "k_norm": pad_head(k_norm).reshape(1, C_HEAD_PAD),
    }


def kernel(x_prompt, x_sample, cache_a_k, cache_a_v, cache_b_k, cache_b_v, cache_c_kv, cache_c_krope, cache_d_k, cache_d_v, c, c_ctx, mod_w, mod_b, norm_ff1, norm_mix, norm_ff2, ff1_w_gu, ff1_w_down, ff2_w_gu, ff2_w_down, a_w_qkv, a_q_norm, a_k_norm, a_sink, a_w_o, b_w_qkv, b_q_norm, b_k_norm, b_w_o, c_w_down, c_q_lnorm, c_kv_lnorm, c_w_uq, c_w_ukv, c_q_norm, c_k_norm, c_w_o, d_w_qkv, d_q_norm, d_k_norm, d_rpb, d_w_o):
    n_p, n_s = BATCH * SEQ, DEC_BATCH * DEC_SEQ
    xp = x_prompt.reshape(n_p, D_MODEL)
    xs = x_sample.reshape(n_s, D_MODEL)
    cond = jnp.concatenate([c_ctx[None], c, jnp.zeros((MOD_ROWS - 1 - DEC_BATCH, D_MODEL), F32)], axis=0)
    modv = _mod_table(cond, mod_w, mod_b)
    s_row = _sample_row(TOKEN_TILE)
    gqa_tabs = _gqa_rope_tables()
    gqa_scale = 1.0
    flat_cache = lambda a: a.reshape(DEC_BATCH * PAST_LEN, -1)
    keys_last = lambda a: jnp.transpose(a, (0, 2, 3, 1))
    keys_first = lambda a: jnp.transpose(a, (0, 3, 1, 2))[:, None]
    ffn_w = [(ff1_w_gu, ff1_w_down), (ff2_w_gu, ff2_w_down)]
    new = {}

    for i in range(DEPTH):
        kind, j = i % N_MIXERS, i // N_MIXERS
        xp = _half_ffn(xp, modv, i, 0, norm_ff1[i], *ffn_w[0], _prompt_row)
        xs = _half_ffn(xs, modv, i, 0, norm_ff1[i], *ffn_w[0], s_row)

        if kind in (0, 1, 3):
            w_qkv, q_norm, k_norm, w_o, n_kv, cache_k, cache_v = {
                0: (a_w_qkv, a_q_norm, a_k_norm, a_w_o, GQA_KV_HEADS, cache_a_k, cache_a_v),
                1: (b_w_qkv, b_q_norm, b_k_norm, b_w_o, GQA_KV_HEADS, cache_b_k, cache_b_v),
                3: (d_w_qkv, d_q_norm, d_k_norm, d_w_o, N_HEADS, cache_d_k, cache_d_v)}[kind]
            w_qkv = w_qkv[j].astype(BF16)
            qp, kp, vp = _gqa_project(xp, modv, i, norm_mix[i], w_qkv, q_norm[j], k_norm[j], n_kv,
                                      _prompt_row, None, True)
            qs, ks, vs = _gqa_project(xs, modv, i, norm_mix[i], w_qkv, q_norm[j], k_norm[j], n_kv,
                                      s_row, gqa_tabs if kind != 3 else None, False)
            op = _attention(qp, [[(kp, vp, SEQ, "fullT")]], n_batch=BATCH, n_tok=SEQ, tq=SEQ, n_kv=n_kv,
                            kv_step=n_kv, dk=HEAD_DIM, scale=gqa_scale, sink=a_sink[j] if kind == 0 else None)
            name = "abcd"[kind]
            new[name + "_k"], new[name + "_v"] = keys_first(kp), keys_first(vp)
            kc, vc = keys_last(cache_k[:, j]), keys_last(cache_v[:, j])
            if kind == 0:
                tq = 2 * A_WINDOW
                os_ = _attention(qs, [[(ks, vs, A_WINDOW, "prev"), (ks, vs, tq, "band"),
                                       (ks, vs, A_WINDOW, "next")], [(kc, vc, PAST_LEN, "fullT")]],
                                 n_batch=DEC_BATCH, n_tok=DEC_SEQ, tq=tq, n_kv=n_kv, kv_step=2, dk=HEAD_DIM,
                                 scale=gqa_scale, sink=a_sink[j])
            elif kind == 1:
                os_ = _attention(qs, [[(ks, vs, DEC_SEQ, "full")], [(kc, vc, PAST_LEN, "fullT")]],
                                 n_batch=DEC_BATCH, n_tok=DEC_SEQ, tq=512, n_kv=n_kv, kv_step=2, dk=HEAD_DIM,
                                 scale=gqa_scale)
            else:
                os_ = _na_attention(qs, ks, vs, kc, vc, d_rpb[j])
        else:
            w = _mla_weights(c_w_down[j], c_q_lnorm[j], c_kv_lnorm[j], c_w_uq[j], c_w_ukv[j],
                             c_q_norm[j], c_k_norm[j])
            w_o = c_w_o
            qp, kp, vp, ckv_p, kr_p = _mla_project(xp, modv, i, norm_mix[i], w, _prompt_row, None)
            qs, ks, vs, _, _ = _mla_project(xs, modv, i, norm_mix[i], w, s_row, _mla_rope_tables())
            new["c_kv"] = ckv_p.reshape(BATCH, 1, SEQ, C_KV_RANK)
            new["c_krope"] = kr_p[:, C_NOPE:C_QK].reshape(BATCH, 1, SEQ, C_ROPE)
            kr_cache = jnp.pad(flat_cache(cache_c_krope[:, j]), ((0, 0), (C_NOPE, LANES - C_QK)))
            kc, vc = _mla_expand_cache(flat_cache(cache_c_kv[:, j]), kr_cache, w)
            mla_scale = C_QK ** -0.5
            op = _attention(qp, [[(kp, vp, SEQ, "full")]], n_batch=BATCH, n_tok=SEQ, tq=SEQ,
                            n_kv=N_HEADS, kv_step=N_HEADS, dk=C_HEAD_PAD, scale=mla_scale)
            os_ = _attention(qs, [[(ks, vs, DEC_SEQ, "full")], [(kc, vc, PAST_LEN, "full")]],
                             n_batch=DEC_BATCH, n_tok=DEC_SEQ, tq=DEC_SEQ, n_kv=N_HEADS, kv_step=4, dk=C_HEAD_PAD,
                             scale=mla_scale)

        w_o = w_o[j].astype(BF16)
        xp = _half_ffn(xp, modv, i, 6, norm_ff2[i], *ffn_w[1], _prompt_row, mixer_out=(op, w_o))
        xs = _half_ffn(xs, modv, i, 6, norm_ff2[i], *ffn_w[1], s_row, mixer_out=(os_, w_o))

    return (xp.reshape(BATCH, SEQ, D_MODEL), xs.reshape(DEC_BATCH, DEC_SEQ, D_MODEL),
            new["a_k"], new["a_v"], new["b_k"], new["b_v"], new["c_kv"], new["c_krope"],
            new["d_k"], new["d_v"])
```

```python
import functools

import jax
import jax.numpy as jnp
from jax import lax
from jax.experimental import pallas as pl
from jax.experimental.pallas import tpu as pltpu

F32 = jnp.float32
BF16 = jnp.bfloat16

D_MODEL = 1024
BATCH = 16
SEQ = 256
DEPTH = 4
DEC_BATCH = 8
DEC_SEQ = 1024
PAST_LEN = 512
GRID_W = 64
N_MIXERS = 4
RMS_EPS = 1e-6
ROPE_THETA = 10000.0
NEG_INF = -1e30
D_FF = 2816
FFN_RES = 0.5
N_MOD = 9
HEAD_DIM = 64
N_HEADS = 16
GQA_KV_HEADS = 4
A_WINDOW = 128
C_Q_RANK = 384
C_KV_RANK = 256
C_NOPE = 64
C_ROPE = 32
C_V = 64
C_QK = C_NOPE + C_ROPE
NA_ROWS = 8
NA_COLS = 16

LANES = 128
MOD_PARTS_PER_STEP = 3
MOD_ROWS = 16
C_DOWN_PAD = 768
C_HEAD_PAD = LANES
VMEM_LIMIT_BYTES = 56 * 1024 * 1024
TOKEN_TILE = 512
FFN_TILE = 1024
FF_CHUNK = 256
N_WEIGHT_SLOTS = 2
KEY_CHUNK = 1024
LOG2_E = 1.4426950408889634
GQA_SCALE = HEAD_DIM ** -0.5
GRID_ROWS = DEC_SEQ // GRID_W
NA_WIN_ROWS = NA_ROWS + 2
NA_WIN_KEYS = NA_WIN_ROWS * GRID_W
NA_SLOTS = NA_WIN_ROWS


def _params(n_axes):
    return pltpu.CompilerParams(dimension_semantics=("arbitrary",) * n_axes,
                                vmem_limit_bytes=VMEM_LIMIT_BYTES)


def _resident(shape):
    nd = len(shape)
    return pl.BlockSpec(shape, lambda *_: (0,) * nd, pipeline_mode=pl.Buffered(1))


def _mod_spec(layer, part, row_fn):
    base = (layer * N_MOD + part) * MOD_ROWS
    return pl.BlockSpec((None, 1, D_MODEL), lambda i: (base + row_fn(i), 0, 0))


def _prompt_row(i):
    return 0


def _sample_row(tile):
    per_batch = DEC_SEQ // tile
    return lambda i: 1 + i // per_batch


def _silu(a):
    return a * (1.0 / (1.0 + jnp.exp(-a)))


def _modulate(x, g, shift, scale):
    y = x * lax.rsqrt(jnp.mean(x * x, axis=-1, keepdims=True) + RMS_EPS)
    return (y * g) * (1.0 + scale) + shift


def _swap_pair_columns(a):
    even = jnp.arange(a.shape[-1]) % 2 == 0
    return jnp.where(even, jnp.roll(a, -1, axis=-1), jnp.roll(a, 1, axis=-1))


def _mod_kernel(c_ref, w_ref, b_ref, o_ref):
    s = _silu(c_ref[...]).astype(BF16)
    for p in range(MOD_PARTS_PER_STEP):
        cols = slice(p * D_MODEL, (p + 1) * D_MODEL)
        o_ref[p] = jnp.dot(s, w_ref[:, cols].astype(BF16), preferred_element_type=F32) + b_ref[:, cols]


def _mod_table(cond, mod_w, mod_b):
    out = pl.pallas_call(
        _mod_kernel,
        grid=(DEPTH, N_MOD // MOD_PARTS_PER_STEP),
        in_specs=[
            pl.BlockSpec((MOD_ROWS, D_MODEL), lambda l, j: (0, 0)),
            pl.BlockSpec((None, D_MODEL, MOD_PARTS_PER_STEP * D_MODEL), lambda l, j: (l, 0, j)),
            pl.BlockSpec((None, 1, MOD_PARTS_PER_STEP * D_MODEL), lambda l, j: (l, 0, j)),
        ],
        out_specs=pl.BlockSpec((None, MOD_PARTS_PER_STEP, MOD_ROWS, D_MODEL), lambda l, j: (l, j, 0, 0)),
        out_shape=jax.ShapeDtypeStruct((DEPTH, N_MOD, MOD_ROWS, D_MODEL), F32),
        compiler_params=_params(2),
        name="mod_table",
    )(cond, mod_w, mod_b.reshape(DEPTH, 1, N_MOD * D_MODEL))
    return out.reshape(DEPTH * N_MOD * MOD_ROWS, 1, D_MODEL)


def _ffn_kernel(*refs, mixer_out, layer):
    if mixer_out:
        x_ref, attn_ref, gm_ref, wo_ref = refs[:4]
        refs = refs[4:]
    else:
        x_ref = refs[0]
        refs = refs[1:]
    (sh_ref, sc_ref, gt_ref, g_ref, wgu_hbm, wd_hbm, o_ref,
     act_ref, wgu_ref, wd_ref, gate_stage, up_stage, down_stage, sems) = refs
    n_chunks = D_FF // FF_CHUNK

    def residual_in():
        if not mixer_out:
            return x_ref[...]
        return x_ref[...] + gm_ref[...] * jnp.dot(attn_ref[...], wo_ref[...], preferred_element_type=F32)

    def chunk_copies(c, slot):
        c0 = c * FF_CHUNK
        return (pltpu.make_async_copy(wgu_hbm.at[layer, :, pl.ds(c0, FF_CHUNK)], gate_stage.at[slot],
                                      sems.at[0, slot]),
                pltpu.make_async_copy(wgu_hbm.at[layer, :, pl.ds(D_FF + c0, FF_CHUNK)], up_stage.at[slot],
                                      sems.at[1, slot]),
                pltpu.make_async_copy(wd_hbm.at[layer, pl.ds(c0, FF_CHUNK), :], down_stage.at[slot],
                                      sems.at[2, slot]))

    def fetch_chunk(c):
        slot = c % N_WEIGHT_SLOTS
        c0 = c * FF_CHUNK
        for copy in chunk_copies(c, slot):
            copy.wait()
        wgu_ref[:, c0:c0 + FF_CHUNK] = gate_stage[slot].astype(BF16)
        wgu_ref[:, D_FF + c0:D_FF + c0 + FF_CHUNK] = up_stage[slot].astype(BF16)
        wd_ref[c0:c0 + FF_CHUNK, :] = down_stage[slot].astype(BF16)
        if c + N_WEIGHT_SLOTS < n_chunks:
            for copy in chunk_copies(c + N_WEIGHT_SLOTS, slot):
                copy.start()

    def tile(stream_weights):
        if stream_weights:
            for c in range(min(N_WEIGHT_SLOTS, n_chunks)):
                for copy in chunk_copies(c, c):
                    copy.start()
        x = residual_in()
        h = _modulate(x, g_ref[...], sh_ref[...], sc_ref[...]).astype(BF16)
        for c in range(n_chunks):
            if stream_weights:
                fetch_chunk(c)
            c0 = c * FF_CHUNK
            a = jnp.dot(h, wgu_ref[:, c0:c0 + FF_CHUNK], preferred_element_type=F32)
            u = jnp.dot(h, wgu_ref[:, D_FF + c0:D_FF + c0 + FF_CHUNK], preferred_element_type=F32)
            act_ref[:, c0:c0 + FF_CHUNK] = (_silu(a) * u).astype(BF16)
        y = jnp.dot(act_ref[...], wd_ref[...], preferred_element_type=F32)
        o_ref[...] = x + (FFN_RES * gt_ref[...]) * y

    @pl.when(pl.program_id(0) == 0)
    def _():
        tile(True)

    @pl.when(pl.program_id(0) > 0)
    def _():
        tile(False)


def _half_ffn(x, modv, layer, part0, g, wgu, wd, latent, mixer_out=None):
    n_tok = x.shape[0]
    rows = TOKEN_TILE if mixer_out is not None else FFN_TILE
    row_fn = _sample_row(rows) if latent else _prompt_row
    tile = pl.BlockSpec((rows, D_MODEL), lambda i: (i, 0))
    in_specs, args = [tile], [x]
    if mixer_out is not None:
        in_specs += [tile, _mod_spec(layer, 5, row_fn), _resident((D_MODEL, D_MODEL))]
        args += [mixer_out[0], modv, mixer_out[1]]
    in_specs += [_mod_spec(layer, part0, row_fn), _mod_spec(layer, part0 + 1, row_fn),
                 _mod_spec(layer, part0 + 2, row_fn),
                 _resident((1, D_MODEL)), pl.BlockSpec(memory_space=pl.ANY), pl.BlockSpec(memory_space=pl.ANY)]
    args += [modv, modv, modv, g.reshape(1, D_MODEL), wgu, wd]
    return pl.pallas_call(
        functools.partial(_ffn_kernel, mixer_out=mixer_out is not None, layer=layer),
        grid=(n_tok // rows,),
        in_specs=in_specs,
        out_specs=tile,
        out_shape=jax.ShapeDtypeStruct((n_tok, D_MODEL), F32),
        scratch_shapes=[pltpu.VMEM((rows, D_FF), BF16),
                        pltpu.VMEM((D_MODEL, 2 * D_FF), BF16),
                        pltpu.VMEM((D_FF, D_MODEL), BF16),
                        pltpu.VMEM((N_WEIGHT_SLOTS, D_MODEL, FF_CHUNK), F32),
                        pltpu.VMEM((N_WEIGHT_SLOTS, D_MODEL, FF_CHUNK), F32),
                        pltpu.VMEM((N_WEIGHT_SLOTS, FF_CHUNK, D_MODEL), F32),
                        pltpu.SemaphoreType.DMA((3, N_WEIGHT_SLOTS))],
        compiler_params=_params(1),
        name=f"ffn_{n_tok}" + ("_mix" if mixer_out is not None else ""),
    )(*args)


def _head_pair_rstd(yp, lo):
    sq = yp * yp
    s_lo = jnp.sum(jnp.where(lo, sq, 0.0), axis=-1, keepdims=True)
    s_hi = jnp.sum(jnp.where(lo, 0.0, sq), axis=-1, keepdims=True)
    ms = jnp.where(lo, s_lo, s_hi) * (1.0 / HEAD_DIM)
    return lax.rsqrt(ms + RMS_EPS)


def _gqa_proj_kernel(*refs, n_q, n_kv, rope, kv_transposed):
    x_ref, sh_ref, sc_ref, g_ref, w_ref, qn_ref, kn_ref = refs[:7]
    if rope:
        wsw_ref, qc_ref, qs_ref, kc_ref, ks_ref = refs[7:12]
        q_ref, k_ref, v_ref = refs[12:]
    else:
        q_ref, k_ref, v_ref = refs[7:]
    h = _modulate(x_ref[...], g_ref[...], sh_ref[...], sc_ref[...]).astype(BF16)
    lane = lax.broadcasted_iota(jnp.int32, (1, LANES), 1)
    lo = lane < HEAD_DIM
    q_cols = n_q * HEAD_DIM
    k_cols = n_kv * HEAD_DIM

    def store(out_ref, o0, y, transposed):
        if not transposed:
            out_ref[:, o0:o0 + LANES] = y.astype(out_ref.dtype)
            return
        yt = y.T
        for b_i in range(y.shape[0] // SEQ):
            for h_i in range(LANES // HEAD_DIM):
                out_ref[b_i, o0 // HEAD_DIM + h_i] = yt[h_i * HEAD_DIM:(h_i + 1) * HEAD_DIM,
                                                        b_i * SEQ:(b_i + 1) * SEQ]

    def normed(c0, gain_ref, tabs, out_ref, o0, post_scale, transposed):
        y = jnp.dot(h, w_ref[:, c0:c0 + 2 * LANES], preferred_element_type=F32)
        if rope:
            y_sw = jnp.dot(h, wsw_ref[:, c0:c0 + 2 * LANES], preferred_element_type=F32)
        for p in range(2):
            part = slice(p * LANES, (p + 1) * LANES)
            rstd = _head_pair_rstd(y[:, part], lo)
            if rope:
                yn = (y[:, part] * tabs[0][...] + y_sw[:, part] * tabs[1][...]) * rstd
            else:
                yn = (y[:, part] * rstd) * gain_ref[...]
            if post_scale is not None:
                yn = yn * post_scale
            store(out_ref, o0 + p * LANES, yn, transposed)

    for c0 in range(0, q_cols, 2 * LANES):
        normed(c0, qn_ref, (qc_ref, qs_ref) if rope else None, q_ref, c0, GQA_SCALE, False)
    for c0 in range(0, k_cols, 2 * LANES):
        normed(q_cols + c0, kn_ref, (kc_ref, ks_ref) if rope else None, k_ref, c0, None, kv_transposed)
    for c0 in range(0, k_cols, 2 * LANES):
        v = jnp.dot(h, w_ref[:, q_cols + k_cols + c0:q_cols + k_cols + c0 + 2 * LANES],
                    preferred_element_type=F32)
        for p in range(2):
            store(v_ref, c0 + p * LANES, v[:, p * LANES:(p + 1) * LANES], kv_transposed)


def _gqa_project(x, modv, layer, g_mix, w_qkv, q_norm, k_norm, n_kv, row_fn, rope_tabs, kv_transposed):
    n_tok = x.shape[0]
    q_cols, k_cols = N_HEADS * HEAD_DIM, n_kv * HEAD_DIM
    rope = rope_tabs is not None
    if kv_transposed:
        per_tile = TOKEN_TILE // SEQ
        kv_spec = pl.BlockSpec((per_tile, n_kv, HEAD_DIM, SEQ), lambda i: (i, 0, 0, 0))
        kv_shape = jax.ShapeDtypeStruct((n_tok // SEQ, n_kv, HEAD_DIM, SEQ), F32)
    else:
        kv_spec = pl.BlockSpec((TOKEN_TILE, k_cols), lambda i: (i, 0))
        kv_shape = jax.ShapeDtypeStruct((n_tok, k_cols), BF16)
    tile = lambda w: pl.BlockSpec((TOKEN_TILE, w), lambda i: (i, 0))
    in_specs = [tile(D_MODEL), _mod_spec(layer, 3, row_fn), _mod_spec(layer, 4, row_fn),
                _resident((1, D_MODEL)), _resident((D_MODEL, q_cols + 2 * k_cols)),
                _resident((1, LANES)), _resident((1, LANES))]
    pair_tile = lambda g: jnp.tile(g, LANES // HEAD_DIM).reshape(1, LANES)
    args = [x, modv, modv, g_mix.reshape(1, D_MODEL), w_qkv, pair_tile(q_norm), pair_tile(k_norm)]
    if rope:
        per_batch = DEC_SEQ // TOKEN_TILE
        tab = pl.BlockSpec((TOKEN_TILE, LANES), lambda i: (i % per_batch, 0))
        cos, sin = rope_tabs
        in_specs += [_resident((D_MODEL, q_cols + k_cols)), tab, tab, tab, tab]
        args += [_swap_pair_columns(w_qkv[:, :q_cols + k_cols]),
                 cos * pair_tile(q_norm), sin * pair_tile(_swap_pair_columns(q_norm)),
                 cos * pair_tile(k_norm), sin * pair_tile(_swap_pair_columns(k_norm))]
    return pl.pallas_call(
        functools.partial(_gqa_proj_kernel, n_q=N_HEADS, n_kv=n_kv, rope=rope, kv_transposed=kv_transposed),
        grid=(n_tok // TOKEN_TILE,),
        in_specs=in_specs,
        out_specs=[tile(q_cols), kv_spec, kv_spec],
        out_shape=[jax.ShapeDtypeStruct((n_tok, q_cols), BF16), kv_shape, kv_shape],
        compiler_params=_params(1),
        name=f"gqa_proj_{n_tok}_kv{n_kv}",
    )(*args)


def _mla_group_rstd(y):
    ms = jnp.sum(y * y, axis=-1, keepdims=True) * (1.0 / C_QK)
    return lax.rsqrt(ms + RMS_EPS)


def _mla_expand_kv(ckv, kr128, wk_ref, wv_ref, kn_ref, rope, k_ref, v_ref):
    for c0 in range(0, N_HEADS * C_HEAD_PAD, 2 * LANES):
        y = jnp.dot(ckv, wk_ref[:, c0:c0 + 2 * LANES], preferred_element_type=F32)
        for p in range(2):
            kp = y[:, p * LANES:(p + 1) * LANES] + kr128
            rstd = _mla_group_rstd(kp)
            if rope is None:
                kn = (kp * rstd) * kn_ref[...]
            else:
                kn = (kp * rope[0][...] + rope[2] * rope[1][...]) * rstd
            k_ref[:, c0 + p * LANES:c0 + (p + 1) * LANES] = kn.astype(k_ref.dtype)
    for c0 in range(0, N_HEADS * C_V, 2 * LANES):
        v = jnp.dot(ckv, wv_ref[:, c0:c0 + 2 * LANES], preferred_element_type=F32)
        v_ref[:, c0:c0 + 2 * LANES] = v.astype(v_ref.dtype)


def _mla_proj_kernel(*refs, rope):
    (x_ref, sh_ref, sc_ref, g_ref, wd_ref, qln_ref, kvln_ref, wuq_ref, wk_ref, wv_ref,
     qn_ref, kn_ref) = refs[:12]
    if rope:
        wuq_sw_ref, qc_ref, qs_ref, kc_ref, ks_ref = refs[12:17]
        q_ref, k_ref, v_ref, ckv_ref, kr_ref = refs[17:]
    else:
        q_ref, k_ref, v_ref, ckv_ref, kr_ref = refs[12:]
    h = _modulate(x_ref[...], g_ref[...], sh_ref[...], sc_ref[...]).astype(BF16)
    y = jnp.dot(h, wd_ref[...], preferred_element_type=F32)

    def row_norm(z, gain):
        return (z * lax.rsqrt(jnp.mean(z * z, axis=-1, keepdims=True) + RMS_EPS)) * gain

    cq = row_norm(y[:, :C_Q_RANK], qln_ref[...]).astype(BF16)
    ckv = row_norm(y[:, C_Q_RANK:C_Q_RANK + C_KV_RANK], kvln_ref[...])
    kr128 = pltpu.roll(y[:, C_Q_RANK + C_KV_RANK:C_DOWN_PAD], C_NOPE, 1)
    ckv_ref[...] = ckv
    kr_ref[...] = kr128

    for c0 in range(0, N_HEADS * C_HEAD_PAD, 2 * LANES):
        yq = jnp.dot(cq, wuq_ref[:, c0:c0 + 2 * LANES], preferred_element_type=F32)
        if rope:
            yq_sw = jnp.dot(cq, wuq_sw_ref[:, c0:c0 + 2 * LANES], preferred_element_type=F32)
        for p in range(2):
            part = slice(p * LANES, (p + 1) * LANES)
            rstd = _mla_group_rstd(yq[:, part])
            if rope:
                qn = (yq[:, part] * qc_ref[...] + yq_sw[:, part] * qs_ref[...]) * rstd
            else:
                qn = (yq[:, part] * rstd) * qn_ref[...]
            q_ref[:, c0 + p * LANES:c0 + (p + 1) * LANES] = qn.astype(q_ref.dtype)
    k_rope = (kc_ref, ks_ref, pltpu.roll(y[:, C_DOWN_PAD:], C_NOPE, 1)) if rope else None
    _mla_expand_kv(ckv.astype(BF16), kr128, wk_ref, wv_ref, kn_ref, k_rope, k_ref, v_ref)


def _mla_project(x, modv, layer, g_mix, w, row_fn, rope_tabs):
    n_tok = x.shape[0]
    rope = rope_tabs is not None
    tile = lambda wd: pl.BlockSpec((TOKEN_TILE, wd), lambda i: (i, 0))
    qk_cols = N_HEADS * C_HEAD_PAD
    w_down = w["down_rope"] if rope else w["down"]
    in_specs = [tile(D_MODEL), _mod_spec(layer, 3, row_fn), _mod_spec(layer, 4, row_fn),
                _resident((1, D_MODEL)), _resident(w_down.shape),
                _resident((1, C_Q_RANK)), _resident((1, C_KV_RANK)),
                _resident((C_Q_RANK, qk_cols)), _resident((C_KV_RANK, qk_cols)),
                _resident((C_KV_RANK, N_HEADS * C_V)),
                _resident((1, LANES)), _resident((1, LANES))]
    args = [x, modv, modv, g_mix.reshape(1, D_MODEL), w_down, w["q_lnorm"], w["kv_lnorm"],
            w["uq"], w["uk"], w["uv"], w["q_norm"], w["k_norm"]]
    if rope:
        per_batch = DEC_SEQ // TOKEN_TILE
        tab = pl.BlockSpec((TOKEN_TILE, LANES), lambda i: (i % per_batch, 0))
        cos, sin = rope_tabs
        in_specs += [_resident((C_Q_RANK, qk_cols)), tab, tab, tab, tab]
        args += [w["uq_sw"], cos * w["q_norm"], sin * _swap_pair_columns(w["q_norm"]),
                 cos * w["k_norm"], sin * _swap_pair_columns(w["k_norm"])]
    return pl.pallas_call(
        functools.partial(_mla_proj_kernel, rope=rope),
        grid=(n_tok // TOKEN_TILE,),
        in_specs=in_specs,
        out_specs=[tile(qk_cols), tile(qk_cols), tile(N_HEADS * C_V), tile(C_KV_RANK), tile(LANES)],
        out_shape=[jax.ShapeDtypeStruct((n_tok, qk_cols), BF16),
                   jax.ShapeDtypeStruct((n_tok, qk_cols), BF16),
                   jax.ShapeDtypeStruct((n_tok, N_HEADS * C_V), BF16),
                   jax.ShapeDtypeStruct((n_tok, C_KV_RANK), F32),
                   jax.ShapeDtypeStruct((n_tok, LANES), F32)],
        compiler_params=_params(1),
        name=f"mla_proj_{n_tok}",
    )(*args)


def _mla_cache_kernel(ckv_ref, kr_ref, wk_ref, wv_ref, kn_ref, k_ref, v_ref):
    _mla_expand_kv(ckv_ref[...].astype(BF16), kr_ref[...], wk_ref, wv_ref, kn_ref, None, k_ref, v_ref)


def _mla_expand_cache(ckv, kr128, w):
    n_tok = ckv.shape[0]
    tile = lambda wd: pl.BlockSpec((TOKEN_TILE, wd), lambda i: (i, 0))
    qk_cols = N_HEADS * C_HEAD_PAD
    return pl.pallas_call(
        _mla_cache_kernel,
        grid=(n_tok // TOKEN_TILE,),
        in_specs=[tile(C_KV_RANK), tile(LANES), _resident((C_KV_RANK, qk_cols)),
                  _resident((C_KV_RANK, N_HEADS * C_V)), _resident((1, LANES))],
        out_specs=[tile(qk_cols), tile(N_HEADS * C_V)],
        out_shape=[jax.ShapeDtypeStruct((n_tok, qk_cols), BF16),
                   jax.ShapeDtypeStruct((n_tok, N_HEADS * C_V), BF16)],
        compiler_params=_params(1),
        name="mla_cache_kv",
    )(ckv, kr128, w["uk"], w["uv"], w["k_norm"])


def _online_softmax(q, segs, scale, sink):
    dims = (((1,), (1,)), ((), ()))
    to_exp2 = scale * LOG2_E
    m = None
    acc = None
    for k, v1, bias, transposed in segs:
        if transposed:
            s = jnp.dot(q, k, preferred_element_type=F32)
        else:
            s = lax.dot_general(q, k, dims, preferred_element_type=F32)
        if bias is not None:
            s = s + bias
        row_max = jnp.max(s, axis=-1, keepdims=True)
        m_new = row_max if m is None else jnp.maximum(m, row_max)
        p = jnp.exp2((s - m_new) * to_exp2).astype(BF16)
        if transposed:
            pv = lax.dot_general(p, v1, dims, preferred_element_type=F32)
        else:
            pv = jnp.dot(p, v1, preferred_element_type=F32)
        acc = pv if m is None else acc * jnp.exp2((m - m_new) * to_exp2) + pv
        m = m_new
    if sink is not None:
        lane = lax.broadcasted_iota(jnp.int32, (1, 2 * HEAD_DIM), 1)
        acc = acc + jnp.where(lane >= HEAD_DIM, jnp.exp2((sink - m * scale) * LOG2_E), 0.0)
    denom = pltpu.roll(acc, HEAD_DIM, 1)
    return (acc * (1.0 / denom))[:, :HEAD_DIM]


def _attn_kernel(*refs, kv_step, group, dk, scale, seg_kinds, has_sink, tq):
    n_piece = sum(len(kinds) for kinds in seg_kinds)
    q_ref = refs[0]
    kv_refs = refs[1:1 + 2 * n_piece]
    sink_ref = refs[1 + 2 * n_piece] if has_sink else None
    o_ref = refs[-1]
    first_head = pl.program_id(1) * (kv_step * group)
    blk = pl.program_id(2)
    n_blk = pl.num_programs(2)

    def cat(parts, axis):
        return parts[0] if len(parts) == 1 else jnp.concatenate(parts, axis=axis)

    def piece_bias(kind, n_keys):
        r = lax.broadcasted_iota(jnp.int32, (tq, n_keys), 0)
        c = lax.broadcasted_iota(jnp.int32, (tq, n_keys), 1)
        if kind == "prev":
            ok = (c >= r) & (blk > 0)
        elif kind == "next":
            ok = (c <= r - (tq - A_WINDOW)) & (blk < n_blk - 1)
        elif kind == "band":
            ok = jnp.abs(r - c) <= A_WINDOW
        else:
            return jnp.zeros((tq, n_keys), F32)
        return jnp.where(ok, 0.0, NEG_INF)

    seg_refs, seg_bias = [], []
    p_i = 0
    for kinds in seg_kinds:
        pieces = [(kv_refs[2 * (p_i + n)], kv_refs[2 * (p_i + n) + 1]) for n in range(len(kinds))]
        p_i += len(kinds)
        seg_refs.append(pieces)
        if all(kind in ("full", "fullT") for kind in kinds):
            seg_bias.append(None)
        else:
            bias = cat([piece_bias(kind, kr.shape[0]) for kind, (kr, _) in zip(kinds, pieces)], 1)
            seg_bias.append(cat([bias] * group, 0))

    if seg_kinds in ((("full",),), (("fullT",),)):
        transposed = seg_kinds[0][0] == "fullT"
        kr, vr = seg_refs[0][0]
        contract = lambda a_dim, b_dim: (((a_dim,), (b_dim,)), ((), ()))
        logits = []
        for hk in range(kv_step):
            q = cat([q_ref[:, h * dk:(h + 1) * dk] for h in range(hk * group, (hk + 1) * group)], 0)
            k = kr[hk].astype(BF16) if transposed else kr[:, hk * dk:(hk + 1) * dk].astype(BF16)
            logits.append(lax.dot_general(k, q, contract(0 if transposed else 1, 1),
                                          preferred_element_type=F32))
        probs, denoms = [], []
        for hk, s_t in enumerate(logits):
            m = jnp.max(s_t, axis=0, keepdims=True)
            p_t = jnp.exp2((s_t - m) * (scale * LOG2_E))
            denom = jnp.sum(p_t, axis=0, keepdims=True)
            if has_sink:
                sink = cat([jnp.full((1, tq), sink_ref[first_head + h], F32)
                            for h in range(hk * group, (hk + 1) * group)], 1)
                denom = denom + jnp.exp2((sink - m * scale) * LOG2_E)
            probs.append(p_t.astype(BF16))
            denoms.append(denom)
        for hk in range(kv_step):
            v = vr[hk].astype(BF16) if transposed else vr[:, hk * HEAD_DIM:(hk + 1) * HEAD_DIM].astype(BF16)
            o_t = lax.dot_general(v, probs[hk], contract(1 if transposed else 0, 0),
                                  preferred_element_type=F32) * (1.0 / denoms[hk])
            for g_i in range(group):
                h = hk * group + g_i
                o_ref[:, h * HEAD_DIM:(h + 1) * HEAD_DIM] = o_t[:, g_i * tq:(g_i + 1) * tq].T.astype(o_ref.dtype)
        return

    for hk in range(kv_step):
        heads = range(hk * group, (hk + 1) * group)
        q = cat([q_ref[:, h * dk:(h + 1) * dk] for h in heads], 0)
        segs = []
        for kinds, pieces, bias in zip(seg_kinds, seg_refs, seg_bias):
            if kinds == ("fullT",):
                k = pieces[0][0][hk].astype(BF16)
                v = pieces[0][1][hk].astype(BF16)
                v1 = jnp.concatenate([v, jnp.ones_like(v)], axis=0)
                for c0 in range(0, k.shape[1], KEY_CHUNK):
                    segs.append((k[:, c0:c0 + KEY_CHUNK], v1[:, c0:c0 + KEY_CHUNK], None, True))
                continue
            k = cat([kr[:, hk * dk:(hk + 1) * dk].astype(BF16) for kr, _ in pieces], 0)
            v = cat([vr[:, hk * HEAD_DIM:(hk + 1) * HEAD_DIM].astype(BF16) for _, vr in pieces], 0)
            v1 = jnp.concatenate([v, jnp.ones_like(v)], axis=1)
            for c0 in range(0, k.shape[0], KEY_CHUNK):
                chunk = slice(c0, c0 + KEY_CHUNK)
                segs.append((k[chunk], v1[chunk], None if bias is None else bias[:, chunk], False))
        sink = (cat([jnp.full((tq, 2 * HEAD_DIM), sink_ref[first_head + h], F32) for h in heads], 0)
                if has_sink else None)
        out = _online_softmax(q, segs, scale, sink)
        for g_i, h in enumerate(heads):
            o_ref[:, h * HEAD_DIM:(h + 1) * HEAD_DIM] = out[g_i * tq:(g_i + 1) * tq].astype(o_ref.dtype)


def _attention(q, segs, *, n_batch, n_tok, tq, n_kv, kv_step, dk, scale, sink=None):
    n_blk = n_tok // tq
    per_blk = tq // A_WINDOW
    n_win = n_tok // A_WINDOW
    group = N_HEADS // n_kv
    in_specs = [pl.BlockSpec((tq, kv_step * group * dk), lambda b, g, i: (b * n_blk + i, g))]
    args = [q]
    for pieces in segs:
        for k, v, rows, kind in pieces:
            if kind == "fullT":
                idx = lambda b, g, i: (b, g, 0, 0)
                in_specs += [pl.BlockSpec((None, kv_step, dk, rows), idx),
                             pl.BlockSpec((None, kv_step, HEAD_DIM, rows), idx)]
                args += [k, v]
                continue
            if kind == "full":
                idx = lambda b, g, i: (b, g)
            elif kind == "band":
                idx = lambda b, g, i: (b * n_blk + i, g)
            elif kind == "prev":
                idx = lambda b, g, i: (b * n_win + jnp.maximum(i * per_blk - 1, 0), g)
            else:
                idx = lambda b, g, i: (b * n_win + jnp.minimum((i + 1) * per_blk, n_win - 1), g)
            in_specs += [pl.BlockSpec((rows, kv_step * dk), idx), pl.BlockSpec((rows, kv_step * HEAD_DIM), idx)]
            args += [k, v]
    seg_kinds = tuple(tuple(kind for _, _, _, kind in pieces) for pieces in segs)
    if sink is not None:
        in_specs.append(pl.BlockSpec(memory_space=pltpu.SMEM))
        args.append(sink)
    return pl.pallas_call(
        functools.partial(_attn_kernel, kv_step=kv_step, group=group, dk=dk, scale=scale,
                          seg_kinds=seg_kinds, has_sink=sink is not None, tq=tq),
        grid=(n_batch, n_kv // kv_step, n_blk),
        in_specs=in_specs,
        out_specs=pl.BlockSpec((tq, kv_step * group * HEAD_DIM), lambda b, g, i: (b * n_blk + i, g)),
        out_shape=jax.ShapeDtypeStruct((n_batch * n_tok, N_HEADS * HEAD_DIM), BF16),
        compiler_params=_params(3),
        name=f"attn_{n_batch}x{n_tok}_kv{n_kv}_dk{dk}_" + "_".join(k[0] for ks in seg_kinds for k in ks),
    )(*args)


def _na_row_window(r):
    rs = min(max(r - NA_ROWS // 2, 0), GRID_ROWS - NA_ROWS)
    ws = min(rs - rs % 2, GRID_ROWS - NA_WIN_ROWS)
    return ws, r - ws, rs - ws


def _na_kernel(q_ref, k_ref, v_ref, kc_ref, vc_ref, rows_ref, o_ref,
               s_ref, sc_ref, p_ref, pc_ref, l_ref, bias_ref):
    dims = (((1,), (1,)), ((), ()))

    @pl.when((pl.program_id(0) == 0) & (pl.program_id(1) == 0))
    def _():
        p_ref[...] = jnp.zeros_like(p_ref)

    @pl.when(pl.program_id(1) == 0)
    def _():
        _na_build_bias(rows_ref, bias_ref)

    head_cols = lambda hh: slice(hh * HEAD_DIM, (hh + 1) * HEAD_DIM)

    def logits(hh):
        q = q_ref[:, head_cols(hh)]
        s_ref[hh] = lax.dot_general(q, k_ref[:, head_cols(hh)], dims, preferred_element_type=F32)
        sc_ref[hh] = jnp.dot(q, kc_ref[hh].astype(BF16), preferred_element_type=F32)

    def softmax_rows(hh):
        for r in range(GRID_ROWS):
            ws, slot, _ = _na_row_window(r)
            rows = slice(r * GRID_W, (r + 1) * GRID_W)
            win = slice(ws * GRID_W, (ws + NA_WIN_ROWS) * GRID_W)
            s_n = s_ref[hh, rows, win] + bias_ref[hh, slot]
            s_c = sc_ref[hh, rows, :]
            m = jnp.maximum(jnp.max(s_n, axis=-1, keepdims=True), jnp.max(s_c, axis=-1, keepdims=True))
            p_n = jnp.exp(s_n - m)
            p_c = jnp.exp(s_c - m)
            l_ref[hh, rows, :] = jnp.sum(p_n, axis=-1, keepdims=True) + jnp.sum(p_c, axis=-1, keepdims=True)
            p_ref[hh, rows, win] = p_n.astype(BF16)
            pc_ref[hh, rows, :] = p_c.astype(BF16)

    def values(hh):
        out = (jnp.dot(p_ref[hh], v_ref[:, head_cols(hh)], preferred_element_type=F32)
               + lax.dot_general(pc_ref[hh], vc_ref[hh].astype(BF16), dims, preferred_element_type=F32))
        o_ref[:, head_cols(hh)] = (out * (1.0 / l_ref[hh])).astype(o_ref.dtype)

    logits(0)
    logits(1)
    softmax_rows(0)
    values(0)
    softmax_rows(1)
    values(1)


def _na_build_bias(rows_ref, bias_ref):
    first_visible = {slot: first for _, slot, first in map(_na_row_window, range(GRID_ROWS))}
    assert sorted(first_visible) == list(range(NA_SLOTS))
    qcol = lax.broadcasted_iota(jnp.int32, (GRID_W, GRID_W), 0)
    kcol = lax.broadcasted_iota(jnp.int32, (GRID_W, GRID_W), 1)
    cs = jnp.clip(qcol - NA_COLS // 2, 0, GRID_W - NA_COLS)
    valid = (kcol >= cs) & (kcol < cs + NA_COLS)
    masked = jnp.full((GRID_W, GRID_W), NEG_INF, F32)
    for hh in range(LANES // HEAD_DIM):
        toep = []
        for ri in range(2 * NA_ROWS - 1):
            rows = jnp.broadcast_to(rows_ref[hh, ri:ri + 1, :], (GRID_W, LANES))
            rolled = pltpu.roll(rows, 0, 1, stride=1, stride_axis=0)
            toep.append(jnp.where(valid, rolled[:, :GRID_W], NEG_INF))
        for d in range(NA_SLOTS):
            for j0 in range(0, NA_WIN_ROWS, LANES // GRID_W):
                pieces = []
                for j in range(j0, j0 + LANES // GRID_W):
                    visible = first_visible[d] <= j < first_visible[d] + NA_ROWS
                    pieces.append(toep[j - d + NA_ROWS - 1] if visible else masked)
                bias_ref[hh, d, :, j0 * GRID_W:j0 * GRID_W + LANES] = jnp.concatenate(pieces, axis=1)


def _na_bias_rows(rpb):
    n_h, n_ri, n_ci = rpb.shape
    return jnp.concatenate([rpb[..., NA_COLS - 1:], jnp.zeros((n_h, n_ri, LANES - n_ci), F32),
                            rpb[..., :NA_COLS - 1]], axis=-1)


def _na_attention(q, k, v, kc, vc, rpb):
    n_pairs = N_HEADS * HEAD_DIM // LANES
    lat = pl.BlockSpec((DEC_SEQ, LANES), lambda hp, b: (b, hp))
    ctx = pl.BlockSpec((None, LANES // HEAD_DIM, HEAD_DIM, PAST_LEN), lambda hp, b: (b, hp, 0, 0))
    return pl.pallas_call(
        _na_kernel,
        grid=(n_pairs, DEC_BATCH),
        in_specs=[lat, lat, lat, ctx, ctx,
                  pl.BlockSpec((LANES // HEAD_DIM, 2 * NA_ROWS - 1, LANES), lambda hp, b: (hp, 0, 0))],
        out_specs=lat,
        out_shape=jax.ShapeDtypeStruct((DEC_BATCH * DEC_SEQ, N_HEADS * HEAD_DIM), BF16),
        scratch_shapes=[pltpu.VMEM((LANES // HEAD_DIM, DEC_SEQ, DEC_SEQ), F32),
                        pltpu.VMEM((LANES // HEAD_DIM, DEC_SEQ, PAST_LEN), F32),
                        pltpu.VMEM((LANES // HEAD_DIM, DEC_SEQ, DEC_SEQ), BF16),
                        pltpu.VMEM((LANES // HEAD_DIM, DEC_SEQ, PAST_LEN), BF16),
                        pltpu.VMEM((LANES // HEAD_DIM, DEC_SEQ, 1), F32),
                        pltpu.VMEM((LANES // HEAD_DIM, NA_SLOTS, GRID_W, NA_WIN_KEYS), F32)],
        compiler_params=_params(2),
        name="na_attn",
    )(q, k, v, kc, vc, _na_bias_rows(rpb))


def _grid_angles(n, rot_dim):
    pos = jnp.arange(n, dtype=jnp.int32)
    row = (pos // GRID_W).astype(F32)
    col = (pos % GRID_W).astype(F32)
    n_ax = rot_dim // 4
    inv = ROPE_THETA ** (-jnp.arange(n_ax, dtype=F32) / n_ax)
    return jnp.concatenate([row[:, None] * inv, col[:, None] * inv], axis=-1)


def _pair_tables(ang):
    cos = jnp.repeat(jnp.cos(ang), 2, axis=-1)
    sin = jnp.stack([-jnp.sin(ang), jnp.sin(ang)], axis=-1).reshape(ang.shape[0], -1)
    return cos, sin


def _gqa_rope_tables():
    cos, sin = _pair_tables(_grid_angles(DEC_SEQ, HEAD_DIM))
    return jnp.tile(cos, (1, LANES // HEAD_DIM)), jnp.tile(sin, (1, LANES // HEAD_DIM))


def _mla_rope_tables():
    cos, sin = _pair_tables(_grid_angles(DEC_SEQ, C_ROPE))
    pad = LANES - C_QK
    cos = jnp.concatenate([jnp.ones((DEC_SEQ, C_NOPE), F32), cos, jnp.ones((DEC_SEQ, pad), F32)], axis=-1)
    sin = jnp.concatenate([jnp.zeros((DEC_SEQ, C_NOPE), F32), sin, jnp.zeros((DEC_SEQ, pad), F32)], axis=-1)
    return cos, sin


def _mla_weights(w_down, q_lnorm, kv_lnorm, w_uq, w_ukv, q_norm, k_norm):
    pad_head = lambda a: jnp.pad(a, [(0, 0)] * (a.ndim - 1) + [(0, C_HEAD_PAD - a.shape[-1])])
    ukv = w_ukv.reshape(C_KV_RANK, N_HEADS, C_NOPE + C_V)
    down = jnp.pad(w_down, ((0, 0), (0, C_DOWN_PAD - w_down.shape[1]))).astype(BF16)
    uq = pad_head(w_uq.reshape(C_Q_RANK, N_HEADS, C_QK)).reshape(C_Q_RANK, -1).astype(BF16)
    return {
        "down": down,
        "down_rope": jnp.concatenate([down, _swap_pair_columns(down[:, C_DOWN_PAD - LANES:])], axis=1),
        "q_lnorm": q_lnorm.reshape(1, C_Q_RANK),
        "kv_lnorm": kv_lnorm.reshape(1, C_KV_RANK),
        "uq": uq,
        "uq_sw": _swap_pair_columns(uq),
        "uk": pad_head(ukv[:, :, :C_NOPE]).reshape(C_KV_RANK, -1).astype(BF16),
        "uv": ukv[:, :, C_NOPE:].reshape(C_KV_RANK, -1).astype(BF16),
        "q_norm": pad_head(q_norm).reshape(1, C_HEAD_PAD),
        "k_norm": pad_head(k_norm).reshape(1, C_HEAD_PAD),
    }


def kernel(x_prompt, x_sample, cache_a_k, cache_a_v, cache_b_k, cache_b_v, cache_c_kv, cache_c_krope, cache_d_k, cache_d_v, c, c_ctx, mod_w, mod_b, norm_ff1, norm_mix, norm_ff2, ff1_w_gu, ff1_w_down, ff2_w_gu, ff2_w_down, a_w_qkv, a_q_norm, a_k_norm, a_sink, a_w_o, b_w_qkv, b_q_norm, b_k_norm, b_w_o, c_w_down, c_q_lnorm, c_kv_lnorm, c_w_uq, c_w_ukv, c_q_norm, c_k_norm, c_w_o, d_w_qkv, d_q_norm, d_k_norm, d_rpb, d_w_o):
    n_p, n_s = BATCH * SEQ, DEC_BATCH * DEC_SEQ
    xp = x_prompt.reshape(n_p, D_MODEL)
    xs = x_sample.reshape(n_s, D_MODEL)
    cond = jnp.concatenate([c_ctx[None], c, jnp.zeros((MOD_ROWS - 1 - DEC_BATCH, D_MODEL), F32)], axis=0)
    modv = _mod_table(cond, mod_w, mod_b)
    s_row = _sample_row(TOKEN_TILE)
    gqa_tabs = _gqa_rope_tables()
    gqa_scale = 1.0
    flat_cache = lambda a: a.reshape(DEC_BATCH * PAST_LEN, -1)
    keys_last = lambda a: jnp.transpose(a, (0, 2, 3, 1))
    keys_first = lambda a: jnp.transpose(a, (0, 3, 1, 2))[:, None]
    ffn_w = [(ff1_w_gu, ff1_w_down), (ff2_w_gu, ff2_w_down)]
    new = {}

    for i in range(DEPTH):
        kind, j = i % N_MIXERS, i // N_MIXERS
        xp = _half_ffn(xp, modv, i, 0, norm_ff1[i], *ffn_w[0], False)
        xs = _half_ffn(xs, modv, i, 0, norm_ff1[i], *ffn_w[0], True)

        if kind in (0, 1, 3):
            w_qkv, q_norm, k_norm, w_o, n_kv, cache_k, cache_v = {
                0: (a_w_qkv, a_q_norm, a_k_norm, a_w_o, GQA_KV_HEADS, cache_a_k, cache_a_v),
                1: (b_w_qkv, b_q_norm, b_k_norm, b_w_o, GQA_KV_HEADS, cache_b_k, cache_b_v),
                3: (d_w_qkv, d_q_norm, d_k_norm, d_w_o, N_HEADS, cache_d_k, cache_d_v)}[kind]
            w_qkv = w_qkv[j].astype(BF16)
            qp, kp, vp = _gqa_project(xp, modv, i, norm_mix[i], w_qkv, q_norm[j], k_norm[j], n_kv,
                                      _prompt_row, None, True)
            qs, ks, vs = _gqa_project(xs, modv, i, norm_mix[i], w_qkv, q_norm[j], k_norm[j], n_kv,
                                      s_row, gqa_tabs if kind != 3 else None, False)
            op = _attention(qp, [[(kp, vp, SEQ, "fullT")]], n_batch=BATCH, n_tok=SEQ, tq=SEQ, n_kv=n_kv,
                            kv_step=n_kv, dk=HEAD_DIM, scale=gqa_scale, sink=a_sink[j] if kind == 0 else None)
            name = "abcd"[kind]
            new[name + "_k"], new[name + "_v"] = keys_first(kp), keys_first(vp)
            kc, vc = keys_last(cache_k[:, j]), keys_last(cache_v[:, j])
            if kind == 0:
                tq = 2 * A_WINDOW
                os_ = _attention(qs, [[(ks, vs, A_WINDOW, "prev"), (ks, vs, tq, "band"),
                                       (ks, vs, A_WINDOW, "next")], [(kc, vc, PAST_LEN, "fullT")]],
                                 n_batch=DEC_BATCH, n_tok=DEC_SEQ, tq=tq, n_kv=n_kv, kv_step=2, dk=HEAD_DIM,
                                 scale=gqa_scale, sink=a_sink[j])
            elif kind == 1:
                os_ = _attention(qs, [[(ks, vs, DEC_SEQ, "full")], [(kc, vc, PAST_LEN, "fullT")]],
                                 n_batch=DEC_BATCH, n_tok=DEC_SEQ, tq=512, n_kv=n_kv, kv_step=2, dk=HEAD_DIM,
                                 scale=gqa_scale)
            else:
                os_ = _na_attention(qs, ks, vs, kc, vc, d_rpb[j])
        else:
            w = _mla_weights(c_w_down[j], c_q_lnorm[j], c_kv_lnorm[j], c_w_uq[j], c_w_ukv[j],
                             c_q_norm[j], c_k_norm[j])
            w_o = c_w_o
            qp, kp, vp, ckv_p, kr_p = _mla_project(xp, modv, i, norm_mix[i], w, _prompt_row, None)
            qs, ks, vs, _, _ = _mla_project(xs, modv, i, norm_mix[i], w, s_row, _mla_rope_tables())
            new["c_kv"] = ckv_p.reshape(BATCH, 1, SEQ, C_KV_RANK)
            new["c_krope"] = kr_p[:, C_NOPE:C_QK].reshape(BATCH, 1, SEQ, C_ROPE)
            kr_cache = jnp.pad(flat_cache(cache_c_krope[:, j]), ((0, 0), (C_NOPE, LANES - C_QK)))
            kc, vc = _mla_expand_cache(flat_cache(cache_c_kv[:, j]), kr_cache, w)
            mla_scale = C_QK ** -0.5
            op = _attention(qp, [[(kp, vp, SEQ, "full")]], n_batch=BATCH, n_tok=SEQ, tq=SEQ,
                            n_kv=N_HEADS, kv_step=N_HEADS, dk=C_HEAD_PAD, scale=mla_scale)
            os_ = _attention(qs, [[(ks, vs, DEC_SEQ, "full")], [(kc, vc, PAST_LEN, "full")]],
                             n_batch=DEC_BATCH, n_tok=DEC_SEQ, tq=DEC_SEQ, n_kv=N_HEADS, kv_step=4, dk=C_HEAD_PAD,
                             scale=mla_scale)

        w_o = w_o[j].astype(BF16)
        xp = _half_ffn(xp, modv, i, 6, norm_ff2[i], *ffn_w[1], False, mixer_out=(op, w_o))
        xs = _half_ffn(xs, modv, i, 6, norm_ff2[i], *ffn_w[1], True, mixer_out=(os_, w_o))

    return (xp.reshape(BATCH, SEQ, D_MODEL), xs.reshape(DEC_BATCH, DEC_SEQ, D_MODEL),
            new["a_k"], new["a_v"], new["b_k"], new["b_v"], new["c_kv"], new["c_krope"],
            new["d_k"], new["d_v"])
```

```python
import functools

import jax
import jax.numpy as jnp
from jax import lax
from jax.experimental import pallas as pl
from jax.experimental.pallas import tpu as pltpu

F32 = jnp.float32
BF16 = jnp.bfloat16

D_MODEL = 1024
BATCH = 16
SEQ = 256
DEPTH = 4
DEC_BATCH = 8
DEC_SEQ = 1024
PAST_LEN = 512
GRID_W = 64
N_MIXERS = 4
RMS_EPS = 1e-6
ROPE_THETA = 10000.0
NEG_INF = -1e30
D_FF = 2816
FFN_RES = 0.5
N_MOD = 9
HEAD_DIM = 64
N_HEADS = 16
GQA_KV_HEADS = 4
A_WINDOW = 128
C_Q_RANK = 384
C_KV_RANK = 256
C_NOPE = 64
C_ROPE = 32
C_V = 64
C_QK = C_NOPE + C_ROPE
NA_ROWS = 8
NA_COLS = 16

LANES = 128
MOD_PARTS_PER_STEP = 3
MOD_ROWS = 16
C_DOWN_PAD = 768
C_HEAD_PAD = LANES
VMEM_LIMIT_BYTES = 56 * 1024 * 1024
TOKEN_TILE = 512
FF_CHUNK = 256
N_WEIGHT_SLOTS = 2
KEY_CHUNK = 1024
LOG2_E = 1.4426950408889634
GQA_SCALE = HEAD_DIM ** -0.5
GRID_ROWS = DEC_SEQ // GRID_W
NA_WIN_ROWS = NA_ROWS + 2
NA_WIN_KEYS = NA_WIN_ROWS * GRID_W
NA_SLOTS = NA_WIN_ROWS


def _params(n_axes):
    return pltpu.CompilerParams(dimension_semantics=("arbitrary",) * n_axes,
                                vmem_limit_bytes=VMEM_LIMIT_BYTES)


def _resident(shape):
    nd = len(shape)
    return pl.BlockSpec(shape, lambda *_: (0,) * nd, pipeline_mode=pl.Buffered(1))


def _mod_spec(layer, part, row_fn):
    base = (layer * N_MOD + part) * MOD_ROWS
    return pl.BlockSpec((None, 1, D_MODEL), lambda i: (base + row_fn(i), 0, 0))


def _prompt_row(i):
    return 0


def _sample_row(tile):
    per_batch = DEC_SEQ // tile
    return lambda i: 1 + i // per_batch


def _silu(a):
    return a * (1.0 / (1.0 + jnp.exp(-a)))


def _modulate(x, g, shift, scale):
    y = x * lax.rsqrt(jnp.mean(x * x, axis=-1, keepdims=True) + RMS_EPS)
    return (y * g) * (1.0 + scale) + shift


def _swap_pair_columns(a):
    even = jnp.arange(a.shape[-1]) % 2 == 0
    return jnp.where(even, jnp.roll(a, -1, axis=-1), jnp.roll(a, 1, axis=-1))


def _mod_kernel(c_ref, w_ref, b_ref, o_ref):
    s = _silu(c_ref[...]).astype(BF16)
    for p in range(MOD_PARTS_PER_STEP):
        cols = slice(p * D_MODEL, (p + 1) * D_MODEL)
        o_ref[p] = jnp.dot(s, w_ref[:, cols].astype(BF16), preferred_element_type=F32) + b_ref[:, cols]


def _mod_table(cond, mod_w, mod_b):
    out = pl.pallas_call(
        _mod_kernel,
        grid=(DEPTH, N_MOD // MOD_PARTS_PER_STEP),
        in_specs=[
            pl.BlockSpec((MOD_ROWS, D_MODEL), lambda l, j: (0, 0)),
            pl.BlockSpec((None, D_MODEL, MOD_PARTS_PER_STEP * D_MODEL), lambda l, j: (l, 0, j)),
            pl.BlockSpec((None, 1, MOD_PARTS_PER_STEP * D_MODEL), lambda l, j: (l, 0, j)),
        ],
        out_specs=pl.BlockSpec((None, MOD_PARTS_PER_STEP, MOD_ROWS, D_MODEL), lambda l, j: (l, j, 0, 0)),
        out_shape=jax.ShapeDtypeStruct((DEPTH, N_MOD, MOD_ROWS, D_MODEL), F32),
        compiler_params=_params(2),
        name="mod_table",
    )(cond, mod_w, mod_b.reshape(DEPTH, 1, N_MOD * D_MODEL))
    return out.reshape(DEPTH * N_MOD * MOD_ROWS, 1, D_MODEL)


def _ffn_kernel(*refs, mixer_out, layer):
    if mixer_out:
        x_ref, attn_ref, gm_ref, wo_ref = refs[:4]
        refs = refs[4:]
    else:
        x_ref = refs[0]
        refs = refs[1:]
    (sh_ref, sc_ref, gt_ref, g_ref, wgu_hbm, wd_hbm, o_ref,
     act_ref, wgu_ref, wd_ref, gate_stage, up_stage, down_stage, sems) = refs
    n_chunks = D_FF // FF_CHUNK

    def residual_in():
        if not mixer_out:
            return x_ref[...]
        return x_ref[...] + gm_ref[...] * jnp.dot(attn_ref[...], wo_ref[...], preferred_element_type=F32)

    def chunk_copies(c, slot):
        c0 = c * FF_CHUNK
        return (pltpu.make_async_copy(wgu_hbm.at[layer, :, pl.ds(c0, FF_CHUNK)], gate_stage.at[slot],
                                      sems.at[0, slot]),
                pltpu.make_async_copy(wgu_hbm.at[layer, :, pl.ds(D_FF + c0, FF_CHUNK)], up_stage.at[slot],
                                      sems.at[1, slot]),
                pltpu.make_async_copy(wd_hbm.at[layer, pl.ds(c0, FF_CHUNK), :], down_stage.at[slot],
                                      sems.at[2, slot]))

    def fetch_chunk(c):
        slot = c % N_WEIGHT_SLOTS
        c0 = c * FF_CHUNK
        for copy in chunk_copies(c, slot):
            copy.wait()
        wgu_ref[:, c0:c0 + FF_CHUNK] = gate_stage[slot].astype(BF16)
        wgu_ref[:, D_FF + c0:D_FF + c0 + FF_CHUNK] = up_stage[slot].astype(BF16)
        wd_ref[c0:c0 + FF_CHUNK, :] = down_stage[slot].astype(BF16)
        if c + N_WEIGHT_SLOTS < n_chunks:
            for copy in chunk_copies(c + N_WEIGHT_SLOTS, slot):
                copy.start()

    def tile(stream_weights):
        if stream_weights:
            for c in range(min(N_WEIGHT_SLOTS, n_chunks)):
                for copy in chunk_copies(c, c):
                    copy.start()
        x = residual_in()
        h = _modulate(x, g_ref[...], sh_ref[...], sc_ref[...]).astype(BF16)
        for c in range(n_chunks):
            if stream_weights:
                fetch_chunk(c)
            c0 = c * FF_CHUNK
            a = jnp.dot(h, wgu_ref[:, c0:c0 + FF_CHUNK], preferred_element_type=F32)
            u = jnp.dot(h, wgu_ref[:, D_FF + c0:D_FF + c0 + FF_CHUNK], preferred_element_type=F32)
            act_ref[:, c0:c0 + FF_CHUNK] = (_silu(a) * u).astype(BF16)
        y = jnp.dot(act_ref[...], wd_ref[...], preferred_element_type=F32)
        o_ref[...] = x + (FFN_RES * gt_ref[...]) * y

    @pl.when(pl.program_id(0) == 0)
    def _():
        tile(True)

    @pl.when(pl.program_id(0) > 0)
    def _():
        tile(False)


def _half_ffn(x, modv, layer, part0, g, wgu, wd, latent, mixer_out=None):
    n_tok = x.shape[0]
    rows = TOKEN_TILE
    row_fn = _sample_row(rows) if latent else _prompt_row
    tile = pl.BlockSpec((rows, D_MODEL), lambda i: (i, 0))
    in_specs, args = [tile], [x]
    if mixer_out is not None:
        in_specs += [tile, _mod_spec(layer, 5, row_fn), _resident((D_MODEL, D_MODEL))]
        args += [mixer_out[0], modv, mixer_out[1]]
    in_specs += [_mod_spec(layer, part0, row_fn), _mod_spec(layer, part0 + 1, row_fn),
                 _mod_spec(layer, part0 + 2, row_fn),
                 _resident((1, D_MODEL)), pl.BlockSpec(memory_space=pl.ANY), pl.BlockSpec(memory_space=pl.ANY)]
    args += [modv, modv, modv, g.reshape(1, D_MODEL), wgu, wd]
    return pl.pallas_call(
        functools.partial(_ffn_kernel, mixer_out=mixer_out is not None, layer=layer),
        grid=(n_tok // rows,),
        in_specs=in_specs,
        out_specs=tile,
        out_shape=jax.ShapeDtypeStruct((n_tok, D_MODEL), F32),
        scratch_shapes=[pltpu.VMEM((rows, D_FF), BF16),
                        pltpu.VMEM((D_MODEL, 2 * D_FF), BF16),
                        pltpu.VMEM((D_FF, D_MODEL), BF16),
                        pltpu.VMEM((N_WEIGHT_SLOTS, D_MODEL, FF_CHUNK), F32),
                        pltpu.VMEM((N_WEIGHT_SLOTS, D_MODEL, FF_CHUNK), F32),
                        pltpu.VMEM((N_WEIGHT_SLOTS, FF_CHUNK, D_MODEL), F32),
                        pltpu.SemaphoreType.DMA((3, N_WEIGHT_SLOTS))],
        compiler_params=_params(1),
        name=f"ffn_{n_tok}" + ("_mix" if mixer_out is not None else ""),
    )(*args)


def _head_pair_rstd(yp, lo):
    sq = yp * yp
    s_lo = jnp.sum(jnp.where(lo, sq, 0.0), axis=-1, keepdims=True)
    s_hi = jnp.sum(jnp.where(lo, 0.0, sq), axis=-1, keepdims=True)
    ms = jnp.where(lo, s_lo, s_hi) * (1.0 / HEAD_DIM)
    return lax.rsqrt(ms + RMS_EPS)


def _gqa_proj_kernel(*refs, n_q, n_kv, rope, kv_transposed):
    x_ref, sh_ref, sc_ref, g_ref, w_ref, qn_ref, kn_ref = refs[:7]
    if rope:
        wsw_ref, qc_ref, qs_ref, kc_ref, ks_ref = refs[7:12]
        q_ref, k_ref, v_ref = refs[12:]
    else:
        q_ref, k_ref, v_ref = refs[7:]
    h = _modulate(x_ref[...], g_ref[...], sh_ref[...], sc_ref[...]).astype(BF16)
    lane = lax.broadcasted_iota(jnp.int32, (1, LANES), 1)
    lo = lane < HEAD_DIM
    q_cols = n_q * HEAD_DIM
    k_cols = n_kv * HEAD_DIM

    def store(out_ref, o0, y, transposed):
        if not transposed:
            out_ref[:, o0:o0 + LANES] = y.astype(out_ref.dtype)
            return
        yt = y.T
        for b_i in range(y.shape[0] // SEQ):
            for h_i in range(LANES // HEAD_DIM):
                out_ref[b_i, o0 // HEAD_DIM + h_i] = yt[h_i * HEAD_DIM:(h_i + 1) * HEAD_DIM,
                                                        b_i * SEQ:(b_i + 1) * SEQ]

    def normed(c0, gain_ref, tabs, out_ref, o0, post_scale, transposed):
        y = jnp.dot(h, w_ref[:, c0:c0 + 2 * LANES], preferred_element_type=F32)
        if rope:
            y_sw = jnp.dot(h, wsw_ref[:, c0:c0 + 2 * LANES], preferred_element_type=F32)
        for p in range(2):
            part = slice(p * LANES, (p + 1) * LANES)
            rstd = _head_pair_rstd(y[:, part], lo)
            if rope:
                yn = (y[:, part] * tabs[0][...] + y_sw[:, part] * tabs[1][...]) * rstd
            else:
                yn = (y[:, part] * rstd) * gain_ref[...]
            if post_scale is not None:
                yn = yn * post_scale
            store(out_ref, o0 + p * LANES, yn, transposed)

    for c0 in range(0, q_cols, 2 * LANES):
        normed(c0, qn_ref, (qc_ref, qs_ref) if rope else None, q_ref, c0, GQA_SCALE, False)
    for c0 in range(0, k_cols, 2 * LANES):
        normed(q_cols + c0, kn_ref, (kc_ref, ks_ref) if rope else None, k_ref, c0, None, kv_transposed)
    for c0 in range(0, k_cols, 2 * LANES):
        v = jnp.dot(h, w_ref[:, q_cols + k_cols + c0:q_cols + k_cols + c0 + 2 * LANES],
                    preferred_element_type=F32)
        for p in range(2):
            store(v_ref, c0 + p * LANES, v[:, p * LANES:(p + 1) * LANES], kv_transposed)


def _gqa_project(x, modv, layer, g_mix, w_qkv, q_norm, k_norm, n_kv, row_fn, rope_tabs, kv_transposed):
    n_tok = x.shape[0]
    q_cols, k_cols = N_HEADS * HEAD_DIM, n_kv * HEAD_DIM
    rope = rope_tabs is not None
    if kv_transposed:
        per_tile = TOKEN_TILE // SEQ
        kv_spec = pl.BlockSpec((per_tile, n_kv, HEAD_DIM, SEQ), lambda i: (i, 0, 0, 0))
        kv_shape = jax.ShapeDtypeStruct((n_tok // SEQ, n_kv, HEAD_DIM, SEQ), F32)
    else:
        kv_spec = pl.BlockSpec((TOKEN_TILE, k_cols), lambda i: (i, 0))
        kv_shape = jax.ShapeDtypeStruct((n_tok, k_cols), BF16)
    tile = lambda w: pl.BlockSpec((TOKEN_TILE, w), lambda i: (i, 0))
    in_specs = [tile(D_MODEL), _mod_spec(layer, 3, row_fn), _mod_spec(layer, 4, row_fn),
                _resident((1, D_MODEL)), _resident((D_MODEL, q_cols + 2 * k_cols)),
                _resident((1, LANES)), _resident((1, LANES))]
    pair_tile = lambda g: jnp.tile(g, LANES // HEAD_DIM).reshape(1, LANES)
    args = [x, modv, modv, g_mix.reshape(1, D_MODEL), w_qkv, pair_tile(q_norm), pair_tile(k_norm)]
    if rope:
        per_batch = DEC_SEQ // TOKEN_TILE
        tab = pl.BlockSpec((TOKEN_TILE, LANES), lambda i: (i % per_batch, 0))
        cos, sin = rope_tabs
        in_specs += [_resident((D_MODEL, q_cols + k_cols)), tab, tab, tab, tab]
        args += [_swap_pair_columns(w_qkv[:, :q_cols + k_cols]),
                 cos * pair_tile(q_norm), sin * pair_tile(_swap_pair_columns(q_norm)),
                 cos * pair_tile(k_norm), sin * pair_tile(_swap_pair_columns(k_norm))]
    return pl.pallas_call(
        functools.partial(_gqa_proj_kernel, n_q=N_HEADS, n_kv=n_kv, rope=rope, kv_transposed=kv_transposed),
        grid=(n_tok // TOKEN_TILE,),
        in_specs=in_specs,
        out_specs=[tile(q_cols), kv_spec, kv_spec],
        out_shape=[jax.ShapeDtypeStruct((n_tok, q_cols), BF16), kv_shape, kv_shape],
        compiler_params=_params(1),
        name=f"gqa_proj_{n_tok}_kv{n_kv}",
    )(*args)


def _mla_group_rstd(y):
    ms = jnp.sum(y * y, axis=-1, keepdims=True) * (1.0 / C_QK)
    return lax.rsqrt(ms + RMS_EPS)


def _mla_expand_kv(ckv, kr128, wk_ref, wv_ref, kn_ref, rope, k_ref, v_ref):
    for c0 in range(0, N_HEADS * C_HEAD_PAD, 2 * LANES):
        y = jnp.dot(ckv, wk_ref[:, c0:c0 + 2 * LANES], preferred_element_type=F32)
        for p in range(2):
            kp = y[:, p * LANES:(p + 1) * LANES] + kr128
            rstd = _mla_group_rstd(kp)
            if rope is None:
                kn = (kp * rstd) * kn_ref[...]
            else:
                kn = (kp * rope[0][...] + rope[2] * rope[1][...]) * rstd
            k_ref[:, c0 + p * LANES:c0 + (p + 1) * LANES] = kn.astype(k_ref.dtype)
    for c0 in range(0, N_HEADS * C_V, 2 * LANES):
        v = jnp.dot(ckv, wv_ref[:, c0:c0 + 2 * LANES], preferred_element_type=F32)
        v_ref[:, c0:c0 + 2 * LANES] = v.astype(v_ref.dtype)


def _mla_proj_kernel(*refs, rope):
    (x_ref, sh_ref, sc_ref, g_ref, wd_ref, qln_ref, kvln_ref, wuq_ref, wk_ref, wv_ref,
     qn_ref, kn_ref) = refs[:12]
    if rope:
        wuq_sw_ref, qc_ref, qs_ref, kc_ref, ks_ref = refs[12:17]
        q_ref, k_ref, v_ref, ckv_ref, kr_ref = refs[17:]
    else:
        q_ref, k_ref, v_ref, ckv_ref, kr_ref = refs[12:]
    h = _modulate(x_ref[...], g_ref[...], sh_ref[...], sc_ref[...]).astype(BF16)
    y = jnp.dot(h, wd_ref[...], preferred_element_type=F32)

    def row_norm(z, gain):
        return (z * lax.rsqrt(jnp.mean(z * z, axis=-1, keepdims=True) + RMS_EPS)) * gain

    cq = row_norm(y[:, :C_Q_RANK], qln_ref[...]).astype(BF16)
    ckv = row_norm(y[:, C_Q_RANK:C_Q_RANK + C_KV_RANK], kvln_ref[...])
    kr128 = pltpu.roll(y[:, C_Q_RANK + C_KV_RANK:C_DOWN_PAD], C_NOPE, 1)
    ckv_ref[...] = ckv
    kr_ref[...] = kr128

    for c0 in range(0, N_HEADS * C_HEAD_PAD, 2 * LANES):
        yq = jnp.dot(cq, wuq_ref[:, c0:c0 + 2 * LANES], preferred_element_type=F32)
        if rope:
            yq_sw = jnp.dot(cq, wuq_sw_ref[:, c0:c0 + 2 * LANES], preferred_element_type=F32)
        for p in range(2):
            part = slice(p * LANES, (p + 1) * LANES)
            rstd = _mla_group_rstd(yq[:, part])
            if rope:
                qn = (yq[:, part] * qc_ref[...] + yq_sw[:, part] * qs_ref[...]) * rstd
            else:
                qn = (yq[:, part] * rstd) * qn_ref[...]
            q_ref[:, c0 + p * LANES:c0 + (p + 1) * LANES] = qn.astype(q_ref.dtype)
    k_rope = (kc_ref, ks_ref, pltpu.roll(y[:, C_DOWN_PAD:], C_NOPE, 1)) if rope else None
    _mla_expand_kv(ckv.astype(BF16), kr128, wk_ref, wv_ref, kn_ref, k_rope, k_ref, v_ref)


def _mla_project(x, modv, layer, g_mix, w, row_fn, rope_tabs):
    n_tok = x.shape[0]
    rope = rope_tabs is not None
    tile = lambda wd: pl.BlockSpec((TOKEN_TILE, wd), lambda i: (i, 0))
    qk_cols = N_HEADS * C_HEAD_PAD
    w_down = w["down_rope"] if rope else w["down"]
    in_specs = [tile(D_MODEL), _mod_spec(layer, 3, row_fn), _mod_spec(layer, 4, row_fn),
                _resident((1, D_MODEL)), _resident(w_down.shape),
                _resident((1, C_Q_RANK)), _resident((1, C_KV_RANK)),
                _resident((C_Q_RANK, qk_cols)), _resident((C_KV_RANK, qk_cols)),
                _resident((C_KV_RANK, N_HEADS * C_V)),
                _resident((1, LANES)), _resident((1, LANES))]
    args = [x, modv, modv, g_mix.reshape(1, D_MODEL), w_down, w["q_lnorm"], w["kv_lnorm"],
            w["uq"], w["uk"], w["uv"], w["q_norm"], w["k_norm"]]
    if rope:
        per_batch = DEC_SEQ // TOKEN_TILE
        tab = pl.BlockSpec((TOKEN_TILE, LANES), lambda i: (i % per_batch, 0))
        cos, sin = rope_tabs
        in_specs += [_resident((C_Q_RANK, qk_cols)), tab, tab, tab, tab]
        args += [w["uq_sw"], cos * w["q_norm"], sin * _swap_pair_columns(w["q_norm"]),
                 cos * w["k_norm"], sin * _swap_pair_columns(w["k_norm"])]
    return pl.pallas_call(
        functools.partial(_mla_proj_kernel, rope=rope),
        grid=(n_tok // TOKEN_TILE,),
        in_specs=in_specs,
        out_specs=[tile(qk_cols), tile(qk_cols), tile(N_HEADS * C_V), tile(C_KV_RANK), tile(LANES)],
        out_shape=[jax.ShapeDtypeStruct((n_tok, qk_cols), BF16),
                   jax.ShapeDtypeStruct((n_tok, qk_cols), BF16),
                   jax.ShapeDtypeStruct((n_tok, N_HEADS * C_V), BF16),
                   jax.ShapeDtypeStruct((n_tok, C_KV_RANK), F32),
                   jax.ShapeDtypeStruct((n_tok, LANES), F32)],
        compiler_params=_params(1),
        name=f"mla_proj_{n_tok}",
    )(*args)


def _mla_cache_kernel(ckv_ref, kr_ref, wk_ref, wv_ref, kn_ref, k_ref, v_ref):
    _mla_expand_kv(ckv_ref[...].astype(BF16), kr_ref[...], wk_ref, wv_ref, kn_ref, None, k_ref, v_ref)


def _mla_expand_cache(ckv, kr128, w):
    n_tok = ckv.shape[0]
    tile = lambda wd: pl.BlockSpec((TOKEN_TILE, wd), lambda i: (i, 0))
    qk_cols = N_HEADS * C_HEAD_PAD
    return pl.pallas_call(
        _mla_cache_kernel,
        grid=(n_tok // TOKEN_TILE,),
        in_specs=[tile(C_KV_RANK), tile(LANES), _resident((C_KV_RANK, qk_cols)),
                  _resident((C_KV_RANK, N_HEADS * C_V)), _resident((1, LANES))],
        out_specs=[tile(qk_cols), tile(N_HEADS * C_V)],
        out_shape=[jax.ShapeDtypeStruct((n_tok, qk_cols), BF16),
                   jax.ShapeDtypeStruct((n_tok, N_HEADS * C_V), BF16)],
        compiler_params=_params(1),
        name="mla_cache_kv",
    )(ckv, kr128, w["uk"], w["uv"], w["k_norm"])


def _online_softmax(q, segs, scale, sink):
    dims = (((1,), (1,)), ((), ()))
    to_exp2 = scale * LOG2_E
    m = None
    acc = None
    for k, v1, bias, transposed in segs:
        if transposed:
            s = jnp.dot(q, k, preferred_element_type=F32)
        else:
            s = lax.dot_general(q, k, dims, preferred_element_type=F32)
        if bias is not None:
            s = s + bias
        row_max = jnp.max(s, axis=-1, keepdims=True)
        m_new = row_max if m is None else jnp.maximum(m, row_max)
        p = jnp.exp2((s - m_new) * to_exp2).astype(BF16)
        if transposed:
            pv = lax.dot_general(p, v1, dims, preferred_element_type=F32)
        else:
            pv = jnp.dot(p, v1, preferred_element_type=F32)
        acc = pv if m is None else acc * jnp.exp2((m - m_new) * to_exp2) + pv
        m = m_new
    if sink is not None:
        lane = lax.broadcasted_iota(jnp.int32, (1, 2 * HEAD_DIM), 1)
        acc = acc + jnp.where(lane >= HEAD_DIM, jnp.exp2((sink - m * scale) * LOG2_E), 0.0)
    denom = pltpu.roll(acc, HEAD_DIM, 1)
    return (acc * (1.0 / denom))[:, :HEAD_DIM]


def _attn_kernel(*refs, kv_step, group, dk, scale, seg_kinds, has_sink, tq):
    n_piece = sum(len(kinds) for kinds in seg_kinds)
    q_ref = refs[0]
    kv_refs = refs[1:1 + 2 * n_piece]
    sink_ref = refs[1 + 2 * n_piece] if has_sink else None
    o_ref = refs[-1]
    first_head = pl.program_id(1) * (kv_step * group)
    blk = pl.program_id(2)
    n_blk = pl.num_programs(2)

    def cat(parts, axis):
        return parts[0] if len(parts) == 1 else jnp.concatenate(parts, axis=axis)

    def piece_bias(kind, n_keys):
        r = lax.broadcasted_iota(jnp.int32, (tq, n_keys), 0)
        c = lax.broadcasted_iota(jnp.int32, (tq, n_keys), 1)
        if kind == "prev":
            ok = (c >= r) & (blk > 0)
        elif kind == "next":
            ok = (c <= r - (tq - A_WINDOW)) & (blk < n_blk - 1)
        elif kind == "band":
            ok = jnp.abs(r - c) <= A_WINDOW
        else:
            return jnp.zeros((tq, n_keys), F32)
        return jnp.where(ok, 0.0, NEG_INF)

    seg_refs, seg_bias = [], []
    p_i = 0
    for kinds in seg_kinds:
        pieces = [(kv_refs[2 * (p_i + n)], kv_refs[2 * (p_i + n) + 1]) for n in range(len(kinds))]
        p_i += len(kinds)
        seg_refs.append(pieces)
        if all(kind in ("full", "fullT") for kind in kinds):
            seg_bias.append(None)
        else:
            bias = cat([piece_bias(kind, kr.shape[0]) for kind, (kr, _) in zip(kinds, pieces)], 1)
            seg_bias.append(cat([bias] * group, 0))

    if seg_kinds in ((("full",),), (("fullT",),)):
        transposed = seg_kinds[0][0] == "fullT"
        kr, vr = seg_refs[0][0]
        contract = lambda a_dim, b_dim: (((a_dim,), (b_dim,)), ((), ()))
        logits = []
        for hk in range(kv_step):
            q = cat([q_ref[:, h * dk:(h + 1) * dk] for h in range(hk * group, (hk + 1) * group)], 0)
            k = kr[hk].astype(BF16) if transposed else kr[:, hk * dk:(hk + 1) * dk].astype(BF16)
            logits.append(lax.dot_general(k, q, contract(0 if transposed else 1, 1),
                                          preferred_element_type=F32))
        probs, denoms = [], []
        for hk, s_t in enumerate(logits):
            m = jnp.max(s_t, axis=0, keepdims=True)
            p_t = jnp.exp2((s_t - m) * (scale * LOG2_E))
            denom = jnp.sum(p_t, axis=0, keepdims=True)
            if has_sink:
                sink = cat([jnp.full((1, tq), sink_ref[first_head + h], F32)
                            for h in range(hk * group, (hk + 1) * group)], 1)
                denom = denom + jnp.exp2((sink - m * scale) * LOG2_E)
            probs.append(p_t.astype(BF16))
            denoms.append(denom)
        for hk in range(kv_step):
            v = vr[hk].astype(BF16) if transposed else vr[:, hk * HEAD_DIM:(hk + 1) * HEAD_DIM].astype(BF16)
            o_t = lax.dot_general(v, probs[hk], contract(1 if transposed else 0, 0),
                                  preferred_element_type=F32) * (1.0 / denoms[hk])
            for g_i in range(group):
                h = hk * group + g_i
                o_ref[:, h * HEAD_DIM:(h + 1) * HEAD_DIM] = o_t[:, g_i * tq:(g_i + 1) * tq].T.astype(o_ref.dtype)
        return

    for hk in range(kv_step):
        heads = range(hk * group, (hk + 1) * group)
        q = cat([q_ref[:, h * dk:(h + 1) * dk] for h in heads], 0)
        segs = []
        for kinds, pieces, bias in zip(seg_kinds, seg_refs, seg_bias):
            if kinds == ("fullT",):
                k = pieces[0][0][hk].astype(BF16)
                v = pieces[0][1][hk].astype(BF16)
                v1 = jnp.concatenate([v, jnp.ones_like(v)], axis=0)
                for c0 in range(0, k.shape[1], KEY_CHUNK):
                    segs.append((k[:, c0:c0 + KEY_CHUNK], v1[:, c0:c0 + KEY_CHUNK], None, True))
                continue
            k = cat([kr[:, hk * dk:(hk + 1) * dk].astype(BF16) for kr, _ in pieces], 0)
            v = cat([vr[:, hk * HEAD_DIM:(hk + 1) * HEAD_DIM].astype(BF16) for _, vr in pieces], 0)
            v1 = jnp.concatenate([v, jnp.ones_like(v)], axis=1)
            for c0 in range(0, k.shape[0], KEY_CHUNK):
                chunk = slice(c0, c0 + KEY_CHUNK)
                segs.append((k[chunk], v1[chunk], None if bias is None else bias[:, chunk], False))
        sink = (cat([jnp.full((tq, 2 * HEAD_DIM), sink_ref[first_head + h], F32) for h in heads], 0)
                if has_sink else None)
        out = _online_softmax(q, segs, scale, sink)
        for g_i, h in enumerate(heads):
            o_ref[:, h * HEAD_DIM:(h + 1) * HEAD_DIM] = out[g_i * tq:(g_i + 1) * tq].astype(o_ref.dtype)


def _attention(q, segs, *, n_batch, n_tok, tq, n_kv, kv_step, dk, scale, sink=None):
    n_blk = n_tok // tq
    per_blk = tq // A_WINDOW
    n_win = n_tok // A_WINDOW
    group = N_HEADS // n_kv
    in_specs = [pl.BlockSpec((tq, kv_step * group * dk), lambda b, g, i: (b * n_blk + i, g))]
    args = [q]
    for pieces in segs:
        for k, v, rows, kind in pieces:
            if kind == "fullT":
                idx = lambda b, g, i: (b, g, 0, 0)
                in_specs += [pl.BlockSpec((None, kv_step, dk, rows), idx),
                             pl.BlockSpec((None, kv_step, HEAD_DIM, rows), idx)]
                args += [k, v]
                continue
            if kind == "full":
                idx = lambda b, g, i: (b, g)
            elif kind == "band":
                idx = lambda b, g, i: (b * n_blk + i, g)
            elif kind == "prev":
                idx = lambda b, g, i: (b * n_win + jnp.maximum(i * per_blk - 1, 0), g)
            else:
                idx = lambda b, g, i: (b * n_win + jnp.minimum((i + 1) * per_blk, n_win - 1), g)
            in_specs += [pl.BlockSpec((rows, kv_step * dk), idx), pl.BlockSpec((rows, kv_step * HEAD_DIM), idx)]
            args += [k, v]
    seg_kinds = tuple(tuple(kind for _, _, _, kind in pieces) for pieces in segs)
    if sink is not None:
        in_specs.append(pl.BlockSpec(memory_space=pltpu.SMEM))
        args.append(sink)
    return pl.pallas_call(
        functools.partial(_attn_kernel, kv_step=kv_step, group=group, dk=dk, scale=scale,
                          seg_kinds=seg_kinds, has_sink=sink is not None, tq=tq),
        grid=(n_batch, n_kv // kv_step, n_blk),
        in_specs=in_specs,
        out_specs=pl.BlockSpec((tq, kv_step * group * HEAD_DIM), lambda b, g, i: (b * n_blk + i, g)),
        out_shape=jax.ShapeDtypeStruct((n_batch * n_tok, N_HEADS * HEAD_DIM), BF16),
        compiler_params=_params(3),
        name=f"attn_{n_batch}x{n_tok}_kv{n_kv}_dk{dk}_" + "_".join(k[0] for ks in seg_kinds for k in ks),
    )(*args)


def _na_row_window(r):
    rs = min(max(r - NA_ROWS // 2, 0), GRID_ROWS - NA_ROWS)
    ws = min(rs - rs % 2, GRID_ROWS - NA_WIN_ROWS)
    return ws, r - ws, rs - ws


def _na_kernel(q_ref, k_ref, v_ref, kc_ref, vc_ref, rows_ref, o_ref,
               s_ref, sc_ref, p_ref, pc_ref, l_ref, bias_ref):
    dims = (((1,), (1,)), ((), ()))

    @pl.when((pl.program_id(0) == 0) & (pl.program_id(1) == 0))
    def _():
        p_ref[...] = jnp.zeros_like(p_ref)

    @pl.when(pl.program_id(1) == 0)
    def _():
        _na_build_bias(rows_ref, bias_ref)

    head_cols = lambda hh: slice(hh * HEAD_DIM, (hh + 1) * HEAD_DIM)

    def logits(hh):
        q = q_ref[:, head_cols(hh)]
        s_ref[hh] = lax.dot_general(q, k_ref[:, head_cols(hh)], dims, preferred_element_type=F32)
        sc_ref[hh] = jnp.dot(q, kc_ref[hh].astype(BF16), preferred_element_type=F32)

    def softmax_rows(hh):
        for r in range(GRID_ROWS):
            ws, slot, _ = _na_row_window(r)
            rows = slice(r * GRID_W, (r + 1) * GRID_W)
            win = slice(ws * GRID_W, (ws + NA_WIN_ROWS) * GRID_W)
            s_n = s_ref[hh, rows, win] + bias_ref[hh, slot]
            s_c = sc_ref[hh, rows, :]
            m = jnp.maximum(jnp.max(s_n, axis=-1, keepdims=True), jnp.max(s_c, axis=-1, keepdims=True))
            p_n = jnp.exp(s_n - m)
            p_c = jnp.exp(s_c - m)
            l_ref[hh, rows, :] = jnp.sum(p_n, axis=-1, keepdims=True) + jnp.sum(p_c, axis=-1, keepdims=True)
            p_ref[hh, rows, win] = p_n.astype(BF16)
            pc_ref[hh, rows, :] = p_c.astype(BF16)

    def values(hh):
        out = (jnp.dot(p_ref[hh], v_ref[:, head_cols(hh)], preferred_element_type=F32)
               + lax.dot_general(pc_ref[hh], vc_ref[hh].astype(BF16), dims, preferred_element_type=F32))
        o_ref[:, head_cols(hh)] = (out * (1.0 / l_ref[hh])).astype(o_ref.dtype)

    logits(0)
    logits(1)
    softmax_rows(0)
    values(0)
    softmax_rows(1)
    values(1)


def _na_build_bias(rows_ref, bias_ref):
    first_visible = {slot: first for _, slot, first in map(_na_row_window, range(GRID_ROWS))}
    assert sorted(first_visible) == list(range(NA_SLOTS))
    qcol = lax.broadcasted_iota(jnp.int32, (GRID_W, GRID_W), 0)
    kcol = lax.broadcasted_iota(jnp.int32, (GRID_W, GRID_W), 1)
    cs = jnp.clip(qcol - NA_COLS // 2, 0, GRID_W - NA_COLS)
    valid = (kcol >= cs) & (kcol < cs + NA_COLS)
    masked = jnp.full((GRID_W, GRID_W), NEG_INF, F32)
    for hh in range(LANES // HEAD_DIM):
        toep = []
        for ri in range(2 * NA_ROWS - 1):
            rows = jnp.broadcast_to(rows_ref[hh, ri:ri + 1, :], (GRID_W, LANES))
            rolled = pltpu.roll(rows, 0, 1, stride=1, stride_axis=0)
            toep.append(jnp.where(valid, rolled[:, :GRID_W], NEG_INF))
        for d in range(NA_SLOTS):
            for j0 in range(0, NA_WIN_ROWS, LANES // GRID_W):
                pieces = []
                for j in range(j0, j0 + LANES // GRID_W):
                    visible = first_visible[d] <= j < first_visible[d] + NA_ROWS
                    pieces.append(toep[j - d + NA_ROWS - 1] if visible else masked)
                bias_ref[hh, d, :, j0 * GRID_W:j0 * GRID_W + LANES] = jnp.concatenate(pieces, axis=1)


def _na_bias_rows(rpb):
    n_h, n_ri, n_ci = rpb.shape
    return jnp.concatenate([rpb[..., NA_COLS - 1:], jnp.zeros((n_h, n_ri, LANES - n_ci), F32),
                            rpb[..., :NA_COLS - 1]], axis=-1)


def _na_attention(q, k, v, kc, vc, rpb):
    n_pairs = N_HEADS * HEAD_DIM // LANES
    lat = pl.BlockSpec((DEC_SEQ, LANES), lambda hp, b: (b, hp))
    ctx = pl.BlockSpec((None, LANES // HEAD_DIM, HEAD_DIM, PAST_LEN), lambda hp, b: (b, hp, 0, 0))
    return pl.pallas_call(
        _na_kernel,
        grid=(n_pairs, DEC_BATCH),
        in_specs=[lat, lat, lat, ctx, ctx,
                  pl.BlockSpec((LANES // HEAD_DIM, 2 * NA_ROWS - 1, LANES), lambda hp, b: (hp, 0, 0))],
        out_specs=lat,
        out_shape=jax.ShapeDtypeStruct((DEC_BATCH * DEC_SEQ, N_HEADS * HEAD_DIM), BF16),
        scratch_shapes=[pltpu.VMEM((LANES // HEAD_DIM, DEC_SEQ, DEC_SEQ), F32),
                        pltpu.VMEM((LANES // HEAD_DIM, DEC_SEQ, PAST_LEN), F32),
                        pltpu.VMEM((LANES // HEAD_DIM, DEC_SEQ, DEC_SEQ), BF16),
                        pltpu.VMEM((LANES // HEAD_DIM, DEC_SEQ, PAST_LEN), BF16),
                        pltpu.VMEM((LANES // HEAD_DIM, DEC_SEQ, 1), F32),
                        pltpu.VMEM((LANES // HEAD_DIM, NA_SLOTS, GRID_W, NA_WIN_KEYS), F32)],
        compiler_params=_params(2),
        name="na_attn",
    )(q, k, v, kc, vc, _na_bias_rows(rpb))


def _grid_angles(n, rot_dim):
    pos = jnp.arange(n, dtype=jnp.int32)
    row = (pos // GRID_W).astype(F32)
    col = (pos % GRID_W).astype(F32)
    n_ax = rot_dim // 4
    inv = ROPE_THETA ** (-jnp.arange(n_ax, dtype=F32) / n_ax)
    return jnp.concatenate([row[:, None] * inv, col[:, None] * inv], axis=-1)


def _pair_tables(ang):
    cos = jnp.repeat(jnp.cos(ang), 2, axis=-1)
    sin = jnp.stack([-jnp.sin(ang), jnp.sin(ang)], axis=-1).reshape(ang.shape[0], -1)
    return cos, sin


def _gqa_rope_tables():
    cos, sin = _pair_tables(_grid_angles(DEC_SEQ, HEAD_DIM))
    return jnp.tile(cos, (1, LANES // HEAD_DIM)), jnp.tile(sin, (1, LANES // HEAD_DIM))


def _mla_rope_tables():
    cos, sin = _pair_tables(_grid_angles(DEC_SEQ, C_ROPE))
    pad = LANES - C_QK
    cos = jnp.concatenate([jnp.ones((DEC_SEQ, C_NOPE), F32), cos, jnp.ones((DEC_SEQ, pad), F32)], axis=-1)
    sin = jnp.concatenate([jnp.zeros((DEC_SEQ, C_NOPE), F32), sin, jnp.zeros((DEC_SEQ, pad), F32)], axis=-1)
    return cos, sin


def _mla_weights(w_down, q_lnorm, kv_lnorm, w_uq, w_ukv, q_norm, k_norm):
    pad_head = lambda a: jnp.pad(a, [(0, 0)] * (a.ndim - 1) + [(0, C_HEAD_PAD - a.shape[-1])])
    ukv = w_ukv.reshape(C_KV_RANK, N_HEADS, C_NOPE + C_V)
    down = jnp.pad(w_down, ((0, 0), (0, C_DOWN_PAD - w_down.shape[1]))).astype(BF16)
    uq = pad_head(w_uq.reshape(C_Q_RANK, N_HEADS, C_QK)).reshape(C_Q_RANK, -1).astype(BF16)
    return {
        "down": down,
        "down_rope": jnp.concatenate([down, _swap_pair_columns(down[:, C_DOWN_PAD - LANES:])], axis=1),
        "q_lnorm": q_lnorm.reshape(1, C_Q_RANK),
        "kv_lnorm": kv_lnorm.reshape(1, C_KV_RANK),
        "uq": uq,
        "uq_sw": _swap_pair_columns(uq),
        "uk": pad_head(ukv[:, :, :C_NOPE]).reshape(C_KV_RANK, -1).astype(BF16),
        "uv": ukv[:, :, C_NOPE:].reshape(C_KV_RANK, -1).astype(BF16),
        "q_norm": pad_head(q_norm).reshape(1, C_HEAD_PAD),
        "k_norm": pad_head(k_norm).reshape(1, C_HEAD_PAD),
    }


def kernel(x_prompt, x_sample, cache_a_k, cache_a_v, cache_b_k, cache_b_v, cache_c_kv, cache_c_krope, cache_d_k, cache_d_v, c, c_ctx, mod_w, mod_b, norm_ff1, norm_mix, norm_ff2, ff1_w_gu, ff1_w_down, ff2_w_gu, ff2_w_down, a_w_qkv, a_q_norm, a_k_norm, a_sink, a_w_o, b_w_qkv, b_q_norm, b_k_norm, b_w_o, c_w_down, c_q_lnorm, c_kv_lnorm, c_w_uq, c_w_ukv, c_q_norm, c_k_norm, c_w_o, d_w_qkv, d_q_norm, d_k_norm, d_rpb, d_w_o):
    n_p, n_s = BATCH * SEQ, DEC_BATCH * DEC_SEQ
    xp = x_prompt.reshape(n_p, D_MODEL)
    xs = x_sample.reshape(n_s, D_MODEL)
    cond = jnp.concatenate([c_ctx[None], c, jnp.zeros((MOD_ROWS - 1 - DEC_BATCH, D_MODEL), F32)], axis=0)
    modv = _mod_table(cond, mod_w, mod_b)
    s_row = _sample_row(TOKEN_TILE)
    gqa_tabs = _gqa_rope_tables()
    gqa_scale = 1.0
    flat_cache = lambda a: a.reshape(DEC_BATCH * PAST_LEN, -1)
    keys_last = lambda a: jnp.transpose(a, (0, 2, 3, 1))
    keys_first = lambda a: jnp.transpose(a, (0, 3, 1, 2))[:, None]
    ffn_w = [(ff1_w_gu, ff1_w_down), (ff2_w_gu, ff2_w_down)]
    new = {}

    for i in range(DEPTH):
        kind, j = i % N_MIXERS, i // N_MIXERS
        xp = _half_ffn(xp, modv, i, 0, norm_ff1[i], *ffn_w[0], False)
        xs = _half_ffn(xs, modv, i, 0, norm_ff1[i], *ffn_w[0], True)

        if kind in (0, 1, 3):
            w_qkv, q_norm, k_norm, w_o, n_kv, cache_k, cache_v = {
                0: (a_w_qkv, a_q_norm, a_k_norm, a_w_o, GQA_KV_HEADS, cache_a_k, cache_a_v),
                1: (b_w_qkv, b_q_norm, b_k_norm, b_w_o, GQA_KV_HEADS, cache_b_k, cache_b_v),
                3: (d_w_qkv, d_q_norm, d_k_norm, d_w_o, N_HEADS, cache_d_k, cache_d_v)}[kind]
            w_qkv = w_qkv[j].astype(BF16)
            qp, kp, vp = _gqa_project(xp, modv, i, norm_mix[i], w_qkv, q_norm[j], k_norm[j], n_kv,
                                      _prompt_row, None, True)
            qs, ks, vs = _gqa_project(xs, modv, i, norm_mix[i], w_qkv, q_norm[j], k_norm[j], n_kv,
                                      s_row, gqa_tabs if kind != 3 else None, False)
            op = _attention(qp, [[(kp, vp, SEQ, "fullT")]], n_batch=BATCH, n_tok=SEQ, tq=SEQ, n_kv=n_kv,
                            kv_step=n_kv, dk=HEAD_DIM, scale=gqa_scale, sink=a_sink[j] if kind == 0 else None)
            name = "abcd"[kind]
            new[name + "_k"], new[name + "_v"] = keys_first(kp), keys_first(vp)
            kc, vc = keys_last(cache_k[:, j]), keys_last(cache_v[:, j])
            if kind == 0:
                tq = 2 * A_WINDOW
                os_ = _attention(qs, [[(ks, vs, A_WINDOW, "prev"), (ks, vs, tq, "band"),
                                       (ks, vs, A_WINDOW, "next")], [(kc, vc, PAST_LEN, "fullT")]],
                                 n_batch=DEC_BATCH, n_tok=DEC_SEQ, tq=tq, n_kv=n_kv, kv_step=n_kv, dk=HEAD_DIM,
                                 scale=gqa_scale, sink=a_sink[j])
            elif kind == 1:
                os_ = _attention(qs, [[(ks, vs, DEC_SEQ, "full")], [(kc, vc, PAST_LEN, "fullT")]],
                                 n_batch=DEC_BATCH, n_tok=DEC_SEQ, tq=256, n_kv=n_kv, kv_step=n_kv, dk=HEAD_DIM,
                                 scale=gqa_scale)
            else:
                os_ = _na_attention(qs, ks, vs, kc, vc, d_rpb[j])
        else:
            w = _mla_weights(c_w_down[j], c_q_lnorm[j], c_kv_lnorm[j], c_w_uq[j], c_w_ukv[j],
                             c_q_norm[j], c_k_norm[j])
            w_o = c_w_o
            qp, kp, vp, ckv_p, kr_p = _mla_project(xp, modv, i, norm_mix[i], w, _prompt_row, None)
            qs, ks, vs, _, _ = _mla_project(xs, modv, i, norm_mix[i], w, s_row, _mla_rope_tables())
            new["c_kv"] = ckv_p.reshape(BATCH, 1, SEQ, C_KV_RANK)
            new["c_krope"] = kr_p[:, C_NOPE:C_QK].reshape(BATCH, 1, SEQ, C_ROPE)
            kr_cache = jnp.pad(flat_cache(cache_c_krope[:, j]), ((0, 0), (C_NOPE, LANES - C_QK)))
            kc, vc = _mla_expand_cache(flat_cache(cache_c_kv[:, j]), kr_cache, w)
            mla_scale = C_QK ** -0.5
            op = _attention(qp, [[(kp, vp, SEQ, "full")]], n_batch=BATCH, n_tok=SEQ, tq=SEQ,
                            n_kv=N_HEADS, kv_step=N_HEADS, dk=C_HEAD_PAD, scale=mla_scale)
            os_ = _attention(qs, [[(ks, vs, DEC_SEQ, "full")], [(kc, vc, PAST_LEN, "full")]],
                             n_batch=DEC_BATCH, n_tok=DEC_SEQ, tq=DEC_SEQ, n_kv=N_HEADS, kv_step=4, dk=C_HEAD_PAD,
                             scale=mla_scale)

        w_o = w_o[j].astype(BF16)
        xp = _half_ffn(xp, modv, i, 6, norm_ff2[i], *ffn_w[1], False, mixer_out=(op, w_o))
        xs = _half_ffn(xs, modv, i, 6, norm_ff2[i], *ffn_w[1], True, mixer_out=(os_, w_o))

    return (xp.reshape(BATCH, SEQ, D_MODEL), xs.reshape(DEC_BATCH, DEC_SEQ, D_MODEL),
            new["a_k"], new["a_v"], new["b_k"], new["b_v"], new["c_kv"], new["c_krope"],
            new["d_k"], new["d_v"])
```

```python
import functools

import jax
import jax.numpy as jnp
from jax import lax
from jax.experimental import pallas as pl
from jax.experimental.pallas import tpu as pltpu

F32 = jnp.float32
BF16 = jnp.bfloat16

D_MODEL = 1024
BATCH = 16
SEQ = 256
DEPTH = 4
DEC_BATCH = 8
DEC_SEQ = 1024
PAST_LEN = 512
GRID_W = 64
N_MIXERS = 4
RMS_EPS = 1e-6
ROPE_THETA = 10000.0
NEG_INF = -1e30
D_FF = 2816
FFN_RES = 0.5
N_MOD = 9
HEAD_DIM = 64
N_HEADS = 16
GQA_KV_HEADS = 4
A_WINDOW = 128
C_Q_RANK = 384
C_KV_RANK = 256
C_NOPE = 64
C_ROPE = 32
C_V = 64
C_QK = C_NOPE + C_ROPE
NA_ROWS = 8
NA_COLS = 16

LANES = 128
MOD_PARTS_PER_STEP = 3
MOD_ROWS = 16
C_DOWN_PAD = 768
C_HEAD_PAD = LANES
VMEM_LIMIT_BYTES = 56 * 1024 * 1024
TOKEN_TILE = 512
N_PROMPT_TILES = BATCH * SEQ // TOKEN_TILE
N_LATENT_TILES = DEC_BATCH * DEC_SEQ // TOKEN_TILE
N_TILES = N_PROMPT_TILES + N_LATENT_TILES
FF_CHUNK = 256
N_WEIGHT_SLOTS = 2
KEY_CHUNK = 1024
LOG2_E = 1.4426950408889634
GQA_SCALE = HEAD_DIM ** -0.5
GRID_ROWS = DEC_SEQ // GRID_W
NA_WIN_ROWS = NA_ROWS + 2
NA_WIN_KEYS = NA_WIN_ROWS * GRID_W
NA_SLOTS = NA_WIN_ROWS


def _params(n_axes):
    return pltpu.CompilerParams(dimension_semantics=("arbitrary",) * n_axes,
                                vmem_limit_bytes=VMEM_LIMIT_BYTES)


def _resident(shape):
    nd = len(shape)
    return pl.BlockSpec(shape, lambda *_: (0,) * nd, pipeline_mode=pl.Buffered(1))


def _mod_spec(layer, part, row_fn):
    base = (layer * N_MOD + part) * MOD_ROWS
    return pl.BlockSpec((None, 1, D_MODEL), lambda i: (base + row_fn(i), 0, 0))


def _prompt_row(i):
    return 0


def _sample_row(tile):
    per_batch = DEC_SEQ // tile
    return lambda i: 1 + i // per_batch


def _silu(a):
    return a * (1.0 / (1.0 + jnp.exp(-a)))


def _modulate(x, g, shift, scale):
    y = x * lax.rsqrt(jnp.mean(x * x, axis=-1, keepdims=True) + RMS_EPS)
    return (y * g) * (1.0 + scale) + shift


def _swap_pair_columns(a):
    even = jnp.arange(a.shape[-1]) % 2 == 0
    return jnp.where(even, jnp.roll(a, -1, axis=-1), jnp.roll(a, 1, axis=-1))


def _mod_kernel(c_ref, w_ref, b_ref, o_ref):
    s = _silu(c_ref[...]).astype(BF16)
    for p in range(MOD_PARTS_PER_STEP):
        cols = slice(p * D_MODEL, (p + 1) * D_MODEL)
        o_ref[p] = jnp.dot(s, w_ref[:, cols].astype(BF16), preferred_element_type=F32) + b_ref[:, cols]


def _mod_table(cond, mod_w, mod_b):
    out = pl.pallas_call(
        _mod_kernel,
        grid=(DEPTH, N_MOD // MOD_PARTS_PER_STEP),
        in_specs=[
            pl.BlockSpec((MOD_ROWS, D_MODEL), lambda l, j: (0, 0)),
            pl.BlockSpec((None, D_MODEL, MOD_PARTS_PER_STEP * D_MODEL), lambda l, j: (l, 0, j)),
            pl.BlockSpec((None, 1, MOD_PARTS_PER_STEP * D_MODEL), lambda l, j: (l, 0, j)),
        ],
        out_specs=pl.BlockSpec((None, MOD_PARTS_PER_STEP, MOD_ROWS, D_MODEL), lambda l, j: (l, j, 0, 0)),
        out_shape=jax.ShapeDtypeStruct((DEPTH, N_MOD, MOD_ROWS, D_MODEL), F32),
        compiler_params=_params(2),
        name="mod_table",
    )(cond, mod_w, mod_b.reshape(DEPTH, 1, N_MOD * D_MODEL))
    return out.reshape(DEPTH * N_MOD * MOD_ROWS, 1, D_MODEL)


def _ffn_kernel(*refs, n_x, n_attn, tile0, layer):
    x_refs, refs = refs[:n_x], refs[n_x:]
    if n_attn:
        attn_refs, (gm_ref, wo_ref), refs = refs[:n_attn], refs[n_attn:n_attn + 2], refs[n_attn + 2:]
    (sh_ref, sc_ref, gt_ref, g_ref, wgu_hbm, wd_hbm, o_ref,
     act_ref, wgu_ref, wd_ref, gate_stage, up_stage, down_stage, sems) = refs
    n_chunks = D_FF // FF_CHUNK

    def pick(sources):
        if len(sources) == 1:
            return sources[0][...]
        is_prompt = pl.program_id(0) + tile0 < N_PROMPT_TILES
        return jnp.where(is_prompt, sources[0][...], sources[1][...])

    def residual_in():
        if not n_attn:
            return pick(x_refs)
        return pick(x_refs) + gm_ref[...] * jnp.dot(pick(attn_refs), wo_ref[...], preferred_element_type=F32)

    def chunk_copies(c, slot):
        c0 = c * FF_CHUNK
        return (pltpu.make_async_copy(wgu_hbm.at[layer, :, pl.ds(c0, FF_CHUNK)], gate_stage.at[slot],
                                      sems.at[0, slot]),
                pltpu.make_async_copy(wgu_hbm.at[layer, :, pl.ds(D_FF + c0, FF_CHUNK)], up_stage.at[slot],
                                      sems.at[1, slot]),
                pltpu.make_async_copy(wd_hbm.at[layer, pl.ds(c0, FF_CHUNK), :], down_stage.at[slot],
                                      sems.at[2, slot]))

    def fetch_chunk(c):
        slot = c % N_WEIGHT_SLOTS
        c0 = c * FF_CHUNK
        for copy in chunk_copies(c, slot):
            copy.wait()
        wgu_ref[:, c0:c0 + FF_CHUNK] = gate_stage[slot].astype(BF16)
        wgu_ref[:, D_FF + c0:D_FF + c0 + FF_CHUNK] = up_stage[slot].astype(BF16)
        wd_ref[c0:c0 + FF_CHUNK, :] = down_stage[slot].astype(BF16)
        if c + N_WEIGHT_SLOTS < n_chunks:
            for copy in chunk_copies(c + N_WEIGHT_SLOTS, slot):
                copy.start()

    def tile(stream_weights):
        if stream_weights:
            for c in range(min(N_WEIGHT_SLOTS, n_chunks)):
                for copy in chunk_copies(c, c):
                    copy.start()
        x = residual_in()
        h = _modulate(x, g_ref[...], sh_ref[...], sc_ref[...]).astype(BF16)
        for c in range(n_chunks):
            if stream_weights:
                fetch_chunk(c)
            c0 = c * FF_CHUNK
            a = jnp.dot(h, wgu_ref[:, c0:c0 + FF_CHUNK], preferred_element_type=F32)
            u = jnp.dot(h, wgu_ref[:, D_FF + c0:D_FF + c0 + FF_CHUNK], preferred_element_type=F32)
            act_ref[:, c0:c0 + FF_CHUNK] = (_silu(a) * u).astype(BF16)
        y = jnp.dot(act_ref[...], wd_ref[...], preferred_element_type=F32)
        o_ref[...] = x + (FFN_RES * gt_ref[...]) * y

    @pl.when(pl.program_id(0) == 0)
    def _():
        tile(True)

    @pl.when(pl.program_id(0) > 0)
    def _():
        tile(False)


def _tile_row(t):
    return jnp.where(t < N_PROMPT_TILES, 0, 1 + (t - N_PROMPT_TILES) // (DEC_SEQ // TOKEN_TILE))


def _half_ffn(x_parts, modv, layer, part0, g, wgu, wd, tiles, attn_parts=None, w_o=None):
    tile0, n_tiles = tiles

    def part_specs(parts):
        specs = []
        for _, first, count in parts:
            specs.append(pl.BlockSpec((TOKEN_TILE, D_MODEL),
                                      lambda i, first=first, count=count: (jnp.clip(i + tile0 - first, 0, count - 1), 0)))
        return specs

    def mod_spec(part):
        base = (layer * N_MOD + part) * MOD_ROWS
        return pl.BlockSpec((None, 1, D_MODEL), lambda i: (base + _tile_row(i + tile0), 0, 0))

    in_specs, args = part_specs(x_parts), [a for a, _, _ in x_parts]
    if attn_parts is not None:
        in_specs += part_specs(attn_parts) + [mod_spec(5), _resident((D_MODEL, D_MODEL))]
        args += [a for a, _, _ in attn_parts] + [modv, w_o]
    in_specs += [mod_spec(part0), mod_spec(part0 + 1), mod_spec(part0 + 2),
                 _resident((1, D_MODEL)), pl.BlockSpec(memory_space=pl.ANY), pl.BlockSpec(memory_space=pl.ANY)]
    args += [modv, modv, modv, g.reshape(1, D_MODEL), wgu, wd]
    return pl.pallas_call(
        functools.partial(_ffn_kernel, n_x=len(x_parts), n_attn=len(attn_parts or ()), tile0=tile0, layer=layer),
        grid=(n_tiles,),
        in_specs=in_specs,
        out_specs=pl.BlockSpec((TOKEN_TILE, D_MODEL), lambda i: (i, 0)),
        out_shape=jax.ShapeDtypeStruct((n_tiles * TOKEN_TILE, D_MODEL), F32),
        scratch_shapes=[pltpu.VMEM((TOKEN_TILE, D_FF), BF16),
                        pltpu.VMEM((D_MODEL, 2 * D_FF), BF16),
                        pltpu.VMEM((D_FF, D_MODEL), BF16),
                        pltpu.VMEM((N_WEIGHT_SLOTS, D_MODEL, FF_CHUNK), F32),
                        pltpu.VMEM((N_WEIGHT_SLOTS, D_MODEL, FF_CHUNK), F32),
                        pltpu.VMEM((N_WEIGHT_SLOTS, FF_CHUNK, D_MODEL), F32),
                        pltpu.SemaphoreType.DMA((3, N_WEIGHT_SLOTS))],
        compiler_params=_params(1),
        name=f"ffn_{n_tiles}" + ("_mix" if attn_parts is not None else ""),
    )(*args)


def _head_pair_rstd(yp, lo):
    sq = yp * yp
    s_lo = jnp.sum(jnp.where(lo, sq, 0.0), axis=-1, keepdims=True)
    s_hi = jnp.sum(jnp.where(lo, 0.0, sq), axis=-1, keepdims=True)
    ms = jnp.where(lo, s_lo, s_hi) * (1.0 / HEAD_DIM)
    return lax.rsqrt(ms + RMS_EPS)


def _gqa_proj_kernel(*refs, n_q, n_kv, rope, kv_transposed):
    x_ref, sh_ref, sc_ref, g_ref, w_ref, qn_ref, kn_ref = refs[:7]
    if rope:
        wsw_ref, qc_ref, qs_ref, kc_ref, ks_ref = refs[7:12]
        q_ref, k_ref, v_ref = refs[12:]
    else:
        q_ref, k_ref, v_ref = refs[7:]
    h = _modulate(x_ref[...], g_ref[...], sh_ref[...], sc_ref[...]).astype(BF16)
    lane = lax.broadcasted_iota(jnp.int32, (1, LANES), 1)
    lo = lane < HEAD_DIM
    q_cols = n_q * HEAD_DIM
    k_cols = n_kv * HEAD_DIM

    def store(out_ref, o0, y, transposed):
        if not transposed:
            out_ref[:, o0:o0 + LANES] = y.astype(out_ref.dtype)
            return
        yt = y.T
        for b_i in range(y.shape[0] // SEQ):
            for h_i in range(LANES // HEAD_DIM):
                out_ref[b_i, o0 // HEAD_DIM + h_i] = yt[h_i * HEAD_DIM:(h_i + 1) * HEAD_DIM,
                                                        b_i * SEQ:(b_i + 1) * SEQ]

    def normed(c0, gain_ref, tabs, out_ref, o0, post_scale, transposed):
        y = jnp.dot(h, w_ref[:, c0:c0 + 2 * LANES], preferred_element_type=F32)
        if rope:
            y_sw = jnp.dot(h, wsw_ref[:, c0:c0 + 2 * LANES], preferred_element_type=F32)
        for p in range(2):
            part = slice(p * LANES, (p + 1) * LANES)
            rstd = _head_pair_rstd(y[:, part], lo)
            if rope:
                yn = (y[:, part] * tabs[0][...] + y_sw[:, part] * tabs[1][...]) * rstd
            else:
                yn = (y[:, part] * rstd) * gain_ref[...]
            if post_scale is not None:
                yn = yn * post_scale
            store(out_ref, o0 + p * LANES, yn, transposed)

    for c0 in range(0, q_cols, 2 * LANES):
        normed(c0, qn_ref, (qc_ref, qs_ref) if rope else None, q_ref, c0, GQA_SCALE, False)
    for c0 in range(0, k_cols, 2 * LANES):
        normed(q_cols + c0, kn_ref, (kc_ref, ks_ref) if rope else None, k_ref, c0, None, kv_transposed)
    for c0 in range(0, k_cols, 2 * LANES):
        v = jnp.dot(h, w_ref[:, q_cols + k_cols + c0:q_cols + k_cols + c0 + 2 * LANES],
                    preferred_element_type=F32)
        for p in range(2):
            store(v_ref, c0 + p * LANES, v[:, p * LANES:(p + 1) * LANES], kv_transposed)


def _gqa_project(x, tiles, modv, layer, g_mix, w_qkv, q_norm, k_norm, n_kv, row_fn, rope_tabs, kv_transposed):
    tile0, n_tiles = tiles
    n_tok = n_tiles * TOKEN_TILE
    q_cols, k_cols = N_HEADS * HEAD_DIM, n_kv * HEAD_DIM
    rope = rope_tabs is not None
    if kv_transposed:
        per_tile = TOKEN_TILE // SEQ
        kv_spec = pl.BlockSpec((per_tile, n_kv, HEAD_DIM, SEQ), lambda i: (i, 0, 0, 0))
        kv_shape = jax.ShapeDtypeStruct((n_tok // SEQ, n_kv, HEAD_DIM, SEQ), F32)
    else:
        kv_spec = pl.BlockSpec((TOKEN_TILE, k_cols), lambda i: (i, 0))
        kv_shape = jax.ShapeDtypeStruct((n_tok, k_cols), BF16)
    tile = lambda w: pl.BlockSpec((TOKEN_TILE, w), lambda i: (i, 0))
    in_specs = [pl.BlockSpec((TOKEN_TILE, D_MODEL), lambda i: (i + tile0, 0)),
                _mod_spec(layer, 3, row_fn), _mod_spec(layer, 4, row_fn),
                _resident((1, D_MODEL)), _resident((D_MODEL, q_cols + 2 * k_cols)),
                _resident((1, LANES)), _resident((1, LANES))]
    pair_tile = lambda g: jnp.tile(g, LANES // HEAD_DIM).reshape(1, LANES)
    args = [x, modv, modv, g_mix.reshape(1, D_MODEL), w_qkv, pair_tile(q_norm), pair_tile(k_norm)]
    if rope:
        per_batch = DEC_SEQ // TOKEN_TILE
        tab = pl.BlockSpec((TOKEN_TILE, LANES), lambda i: (i % per_batch, 0))
        cos, sin = rope_tabs
        in_specs += [_resident((D_MODEL, q_cols + k_cols)), tab, tab, tab, tab]
        args += [_swap_pair_columns(w_qkv[:, :q_cols + k_cols]),
                 cos * pair_tile(q_norm), sin * pair_tile(_swap_pair_columns(q_norm)),
                 cos * pair_tile(k_norm), sin * pair_tile(_swap_pair_columns(k_norm))]
    return pl.pallas_call(
        functools.partial(_gqa_proj_kernel, n_q=N_HEADS, n_kv=n_kv, rope=rope, kv_transposed=kv_transposed),
        grid=(n_tok // TOKEN_TILE,),
        in_specs=in_specs,
        out_specs=[tile(q_cols), kv_spec, kv_spec],
        out_shape=[jax.ShapeDtypeStruct((n_tok, q_cols), BF16), kv_shape, kv_shape],
        compiler_params=_params(1),
        name=f"gqa_proj_{n_tok}_kv{n_kv}",
    )(*args)


def _mla_group_rstd(y):
    ms = jnp.sum(y * y, axis=-1, keepdims=True) * (1.0 / C_QK)
    return lax.rsqrt(ms + RMS_EPS)


def _mla_expand_kv(ckv, kr128, wk_ref, wv_ref, kn_ref, rope, k_ref, v_ref):
    for c0 in range(0, N_HEADS * C_HEAD_PAD, 2 * LANES):
        y = jnp.dot(ckv, wk_ref[:, c0:c0 + 2 * LANES], preferred_element_type=F32)
        for p in range(2):
            kp = y[:, p * LANES:(p + 1) * LANES] + kr128
            rstd = _mla_group_rstd(kp)
            if rope is None:
                kn = (kp * rstd) * kn_ref[...]
            else:
                kn = (kp * rope[0][...] + rope[2] * rope[1][...]) * rstd
            k_ref[:, c0 + p * LANES:c0 + (p + 1) * LANES] = kn.astype(k_ref.dtype)
    for c0 in range(0, N_HEADS * C_V, 2 * LANES):
        v = jnp.dot(ckv, wv_ref[:, c0:c0 + 2 * LANES], preferred_element_type=F32)
        v_ref[:, c0:c0 + 2 * LANES] = v.astype(v_ref.dtype)


def _mla_proj_kernel(*refs, rope):
    (x_ref, sh_ref, sc_ref, g_ref, wd_ref, qln_ref, kvln_ref, wuq_ref, wk_ref, wv_ref,
     qn_ref, kn_ref) = refs[:12]
    if rope:
        wuq_sw_ref, qc_ref, qs_ref, kc_ref, ks_ref = refs[12:17]
        q_ref, k_ref, v_ref, ckv_ref, kr_ref = refs[17:]
    else:
        q_ref, k_ref, v_ref, ckv_ref, kr_ref = refs[12:]
    h = _modulate(x_ref[...], g_ref[...], sh_ref[...], sc_ref[...]).astype(BF16)
    y = jnp.dot(h, wd_ref[...], preferred_element_type=F32)

    def row_norm(z, gain):
        return (z * lax.rsqrt(jnp.mean(z * z, axis=-1, keepdims=True) + RMS_EPS)) * gain

    cq = row_norm(y[:, :C_Q_RANK], qln_ref[...]).astype(BF16)
    ckv = row_norm(y[:, C_Q_RANK:C_Q_RANK + C_KV_RANK], kvln_ref[...])
    kr128 = pltpu.roll(y[:, C_Q_RANK + C_KV_RANK:C_DOWN_PAD], C_NOPE, 1)
    ckv_ref[...] = ckv
    kr_ref[...] = kr128

    for c0 in range(0, N_HEADS * C_HEAD_PAD, 2 * LANES):
        yq = jnp.dot(cq, wuq_ref[:, c0:c0 + 2 * LANES], preferred_element_type=F32)
        if rope:
            yq_sw = jnp.dot(cq, wuq_sw_ref[:, c0:c0 + 2 * LANES], preferred_element_type=F32)
        for p in range(2):
            part = slice(p * LANES, (p + 1) * LANES)
            rstd = _mla_group_rstd(yq[:, part])
            if rope:
                qn = (yq[:, part] * qc_ref[...] + yq_sw[:, part] * qs_ref[...]) * rstd
            else:
                qn = (yq[:, part] * rstd) * qn_ref[...]
            q_ref[:, c0 + p * LANES:c0 + (p + 1) * LANES] = qn.astype(q_ref.dtype)
    k_rope = (kc_ref, ks_ref, pltpu.roll(y[:, C_DOWN_PAD:], C_NOPE, 1)) if rope else None
    _mla_expand_kv(ckv.astype(BF16), kr128, wk_ref, wv_ref, kn_ref, k_rope, k_ref, v_ref)


def _mla_project(x, tiles, modv, layer, g_mix, w, row_fn, rope_tabs):
    tile0, n_tiles = tiles
    n_tok = n_tiles * TOKEN_TILE
    rope = rope_tabs is not None
    tile = lambda wd: pl.BlockSpec((TOKEN_TILE, wd), lambda i: (i, 0))
    qk_cols = N_HEADS * C_HEAD_PAD
    w_down = w["down_rope"] if rope else w["down"]
    in_specs = [pl.BlockSpec((TOKEN_TILE, D_MODEL), lambda i: (i + tile0, 0)),
                _mod_spec(layer, 3, row_fn), _mod_spec(layer, 4, row_fn),
                _resident((1, D_MODEL)), _resident(w_down.shape),
                _resident((1, C_Q_RANK)), _resident((1, C_KV_RANK)),
                _resident((C_Q_RANK, qk_cols)), _resident((C_KV_RANK, qk_cols)),
                _resident((C_KV_RANK, N_HEADS * C_V)),
                _resident((1, LANES)), _resident((1, LANES))]
    args = [x, modv, modv, g_mix.reshape(1, D_MODEL), w_down, w["q_lnorm"], w["kv_lnorm"],
            w["uq"], w["uk"], w["uv"], w["q_norm"], w["k_norm"]]
    if rope:
        per_batch = DEC_SEQ // TOKEN_TILE
        tab = pl.BlockSpec((TOKEN_TILE, LANES), lambda i: (i % per_batch, 0))
        cos, sin = rope_tabs
        in_specs += [_resident((C_Q_RANK, qk_cols)), tab, tab, tab, tab]
        args += [w["uq_sw"], cos * w["q_norm"], sin * _swap_pair_columns(w["q_norm"]),
                 cos * w["k_norm"], sin * _swap_pair_columns(w["k_norm"])]
    return pl.pallas_call(
        functools.partial(_mla_proj_kernel, rope=rope),
        grid=(n_tok // TOKEN_TILE,),
        in_specs=in_specs,
        out_specs=[tile(qk_cols), tile(qk_cols), tile(N_HEADS * C_V), tile(C_KV_RANK), tile(LANES)],
        out_shape=[jax.ShapeDtypeStruct((n_tok, qk_cols), BF16),
                   jax.ShapeDtypeStruct((n_tok, qk_cols), BF16),
                   jax.ShapeDtypeStruct((n_tok, N_HEADS * C_V), BF16),
                   jax.ShapeDtypeStruct((n_tok, C_KV_RANK), F32),
                   jax.ShapeDtypeStruct((n_tok, LANES), F32)],
        compiler_params=_params(1),
        name=f"mla_proj_{n_tok}",
    )(*args)


def _mla_cache_kernel(ckv_ref, kr_ref, wk_ref, wv_ref, kn_ref, k_ref, v_ref):
    _mla_expand_kv(ckv_ref[...].astype(BF16), kr_ref[...], wk_ref, wv_ref, kn_ref, None, k_ref, v_ref)


def _mla_expand_cache(ckv, kr128, w):
    n_tok = ckv.shape[0]
    tile = lambda wd: pl.BlockSpec((TOKEN_TILE, wd), lambda i: (i, 0))
    qk_cols = N_HEADS * C_HEAD_PAD
    return pl.pallas_call(
        _mla_cache_kernel,
        grid=(n_tok // TOKEN_TILE,),
        in_specs=[tile(C_KV_RANK), tile(LANES), _resident((C_KV_RANK, qk_cols)),
                  _resident((C_KV_RANK, N_HEADS * C_V)), _resident((1, LANES))],
        out_specs=[tile(qk_cols), tile(N_HEADS * C_V)],
        out_shape=[jax.ShapeDtypeStruct((n_tok, qk_cols), BF16),
                   jax.ShapeDtypeStruct((n_tok, N_HEADS * C_V), BF16)],
        compiler_params=_params(1),
        name="mla_cache_kv",
    )(ckv, kr128, w["uk"], w["uv"], w["k_norm"])


def _online_softmax(q, segs, scale, sink):
    dims = (((1,), (1,)), ((), ()))
    to_exp2 = scale * LOG2_E
    m = None
    acc = None
    for k, v1, bias, transposed in segs:
        if transposed:
            s = jnp.dot(q, k, preferred_element_type=F32)
        else:
            s = lax.dot_general(q, k, dims, preferred_element_type=F32)
        if bias is not None:
            s = s + bias
        row_max = jnp.max(s, axis=-1, keepdims=True)
        m_new = row_max if m is None else jnp.maximum(m, row_max)
        p = jnp.exp2((s - m_new) * to_exp2).astype(BF16)
        if transposed:
            pv = lax.dot_general(p, v1, dims, preferred_element_type=F32)
        else:
            pv = jnp.dot(p, v1, preferred_element_type=F32)
        acc = pv if m is None else acc * jnp.exp2((m - m_new) * to_exp2) + pv
        m = m_new
    if sink is not None:
        lane = lax.broadcasted_iota(jnp.int32, (1, 2 * HEAD_DIM), 1)
        acc = acc + jnp.where(lane >= HEAD_DIM, jnp.exp2((sink - m * scale) * LOG2_E), 0.0)
    denom = pltpu.roll(acc, HEAD_DIM, 1)
    return (acc * (1.0 / denom))[:, :HEAD_DIM]


def _attn_kernel(*refs, kv_step, group, dk, scale, seg_kinds, has_sink, tq):
    n_piece = sum(len(kinds) for kinds in seg_kinds)
    q_ref = refs[0]
    kv_refs = refs[1:1 + 2 * n_piece]
    sink_ref = refs[1 + 2 * n_piece] if has_sink else None
    o_ref = refs[-1]
    first_head = pl.program_id(1) * (kv_step * group)
    blk = pl.program_id(2)
    n_blk = pl.num_programs(2)

    def cat(parts, axis):
        return parts[0] if len(parts) == 1 else jnp.concatenate(parts, axis=axis)

    def piece_bias(kind, n_keys):
        r = lax.broadcasted_iota(jnp.int32, (tq, n_keys), 0)
        c = lax.broadcasted_iota(jnp.int32, (tq, n_keys), 1)
        if kind == "prev":
            ok = (c >= r) & (blk > 0)
        elif kind == "next":
            ok = (c <= r - (tq - A_WINDOW)) & (blk < n_blk - 1)
        elif kind == "band":
            ok = jnp.abs(r - c) <= A_WINDOW
        else:
            return jnp.zeros((tq, n_keys), F32)
        return jnp.where(ok, 0.0, NEG_INF)

    seg_refs, seg_bias = [], []
    p_i = 0
    for kinds in seg_kinds:
        pieces = [(kv_refs[2 * (p_i + n)], kv_refs[2 * (p_i + n) + 1]) for n in range(len(kinds))]
        p_i += len(kinds)
        seg_refs.append(pieces)
        if all(kind in ("full", "fullT") for kind in kinds):
            seg_bias.append(None)
        else:
            bias = cat([piece_bias(kind, kr.shape[0]) for kind, (kr, _) in zip(kinds, pieces)], 1)
            seg_bias.append(cat([bias] * group, 0))

    if seg_kinds in ((("full",),), (("fullT",),)):
        transposed = seg_kinds[0][0] == "fullT"
        kr, vr = seg_refs[0][0]
        contract = lambda a_dim, b_dim: (((a_dim,), (b_dim,)), ((), ()))
        logits = []
        for hk in range(kv_step):
            q = cat([q_ref[:, h * dk:(h + 1) * dk] for h in range(hk * group, (hk + 1) * group)], 0)
            k = kr[hk].astype(BF16) if transposed else kr[:, hk * dk:(hk + 1) * dk].astype(BF16)
            logits.append(lax.dot_general(k, q, contract(0 if transposed else 1, 1),
                                          preferred_element_type=F32))
        probs, denoms = [], []
        for hk, s_t in enumerate(logits):
            m = jnp.max(s_t, axis=0, keepdims=True)
            p_t = jnp.exp2((s_t - m) * (scale * LOG2_E))
            denom = jnp.sum(p_t, axis=0, keepdims=True)
            if has_sink:
                sink = cat([jnp.full((1, tq), sink_ref[first_head + h], F32)
                            for h in range(hk * group, (hk + 1) * group)], 1)
                denom = denom + jnp.exp2((sink - m * scale) * LOG2_E)
            probs.append(p_t.astype(BF16))
            denoms.append(denom)
        for hk in range(kv_step):
            v = vr[hk].astype(BF16) if transposed else vr[:, hk * HEAD_DIM:(hk + 1) * HEAD_DIM].astype(BF16)
            o_t = lax.dot_general(v, probs[hk], contract(1 if transposed else 0, 0),
                                  preferred_element_type=F32) * (1.0 / denoms[hk])
            for g_i in range(group):
                h = hk * group + g_i
                o_ref[:, h * HEAD_DIM:(h + 1) * HEAD_DIM] = o_t[:, g_i * tq:(g_i + 1) * tq].T.astype(o_ref.dtype)
        return

    for hk in range(kv_step):
        heads = range(hk * group, (hk + 1) * group)
        q = cat([q_ref[:, h * dk:(h + 1) * dk] for h in heads], 0)
        segs = []
        for kinds, pieces, bias in zip(seg_kinds, seg_refs, seg_bias):
            if kinds == ("fullT",):
                k = pieces[0][0][hk].astype(BF16)
                v = pieces[0][1][hk].astype(BF16)
                v1 = jnp.concatenate([v, jnp.ones_like(v)], axis=0)
                for c0 in range(0, k.shape[1], KEY_CHUNK):
                    segs.append((k[:, c0:c0 + KEY_CHUNK], v1[:, c0:c0 + KEY_CHUNK], None, True))
                continue
            k = cat([kr[:, hk * dk:(hk + 1) * dk].astype(BF16) for kr, _ in pieces], 0)
            v = cat([vr[:, hk * HEAD_DIM:(hk + 1) * HEAD_DIM].astype(BF16) for _, vr in pieces], 0)
            v1 = jnp.concatenate([v, jnp.ones_like(v)], axis=1)
            for c0 in range(0, k.shape[0], KEY_CHUNK):
                chunk = slice(c0, c0 + KEY_CHUNK)
                segs.append((k[chunk], v1[chunk], None if bias is None else bias[:, chunk], False))
        sink = (cat([jnp.full((tq, 2 * HEAD_DIM), sink_ref[first_head + h], F32) for h in heads], 0)
                if has_sink else None)
        out = _online_softmax(q, segs, scale, sink)
        for g_i, h in enumerate(heads):
            o_ref[:, h * HEAD_DIM:(h + 1) * HEAD_DIM] = out[g_i * tq:(g_i + 1) * tq].astype(o_ref.dtype)


def _attention(q, segs, *, n_batch, n_tok, tq, n_kv, kv_step, dk, scale, sink=None):
    n_blk = n_tok // tq
    per_blk = tq // A_WINDOW
    n_win = n_tok // A_WINDOW
    group = N_HEADS // n_kv
    in_specs = [pl.BlockSpec((tq, kv_step * group * dk), lambda b, g, i: (b * n_blk + i, g))]
    args = [q]
    for pieces in segs:
        for k, v, rows, kind in pieces:
            if kind == "fullT":
                idx = lambda b, g, i: (b, g, 0, 0)
                in_specs += [pl.BlockSpec((None, kv_step, dk, rows), idx),
                             pl.BlockSpec((None, kv_step, HEAD_DIM, rows), idx)]
                args += [k, v]
                continue
            if kind == "full":
                idx = lambda b, g, i: (b, g)
            elif kind == "band":
                idx = lambda b, g, i: (b * n_blk + i, g)
            elif kind == "prev":
                idx = lambda b, g, i: (b * n_win + jnp.maximum(i * per_blk - 1, 0), g)
            else:
                idx = lambda b, g, i: (b * n_win + jnp.minimum((i + 1) * per_blk, n_win - 1), g)
            in_specs += [pl.BlockSpec((rows, kv_step * dk), idx), pl.BlockSpec((rows, kv_step * HEAD_DIM), idx)]
            args += [k, v]
    seg_kinds = tuple(tuple(kind for _, _, _, kind in pieces) for pieces in segs)
    if sink is not None:
        in_specs.append(pl.BlockSpec(memory_space=pltpu.SMEM))
        args.append(sink)
    return pl.pallas_call(
        functools.partial(_attn_kernel, kv_step=kv_step, group=group, dk=dk, scale=scale,
                          seg_kinds=seg_kinds, has_sink=sink is not None, tq=tq),
        grid=(n_batch, n_kv // kv_step, n_blk),
        in_specs=in_specs,
        out_specs=pl.BlockSpec((tq, kv_step * group * HEAD_DIM), lambda b, g, i: (b * n_blk + i, g)),
        out_shape=jax.ShapeDtypeStruct((n_batch * n_tok, N_HEADS * HEAD_DIM), BF16),
        compiler_params=_params(3),
        name=f"attn_{n_batch}x{n_tok}_kv{n_kv}_dk{dk}_" + "_".join(k[0] for ks in seg_kinds for k in ks),
    )(*args)


def _na_row_window(r):
    rs = min(max(r - NA_ROWS // 2, 0), GRID_ROWS - NA_ROWS)
    ws = min(rs - rs % 2, GRID_ROWS - NA_WIN_ROWS)
    return ws, r - ws, rs - ws


def _na_kernel(q_ref, k_ref, v_ref, kc_ref, vc_ref, rows_ref, o_ref,
               s_ref, sc_ref, p_ref, pc_ref, l_ref, bias_ref):
    dims = (((1,), (1,)), ((), ()))

    @pl.when((pl.program_id(0) == 0) & (pl.program_id(1) == 0))
    def _():
        p_ref[...] = jnp.zeros_like(p_ref)

    @pl.when(pl.program_id(1) == 0)
    def _():
        _na_build_bias(rows_ref, bias_ref)

    head_cols = lambda hh: slice(hh * HEAD_DIM, (hh + 1) * HEAD_DIM)

    def logits(hh):
        q = q_ref[:, head_cols(hh)]
        s_ref[hh] = lax.dot_general(q, k_ref[:, head_cols(hh)], dims, preferred_element_type=F32)
        sc_ref[hh] = jnp.dot(q, kc_ref[hh].astype(BF16), preferred_element_type=F32)

    def softmax_rows(hh):
        for r in range(GRID_ROWS):
            ws, slot, _ = _na_row_window(r)
            rows = slice(r * GRID_W, (r + 1) * GRID_W)
            win = slice(ws * GRID_W, (ws + NA_WIN_ROWS) * GRID_W)
            s_n = s_ref[hh, rows, win] + bias_ref[hh, slot]
            s_c = sc_ref[hh, rows, :]
            m = jnp.maximum(jnp.max(s_n, axis=-1, keepdims=True), jnp.max(s_c, axis=-1, keepdims=True))
            p_n = jnp.exp(s_n - m)
            p_c = jnp.exp(s_c - m)
            l_ref[hh, rows, :] = jnp.sum(p_n, axis=-1, keepdims=True) + jnp.sum(p_c, axis=-1, keepdims=True)
            p_ref[hh, rows, win] = p_n.astype(BF16)
            pc_ref[hh, rows, :] = p_c.astype(BF16)

    def values(hh):
        out = (jnp.dot(p_ref[hh], v_ref[:, head_cols(hh)], preferred_element_type=F32)
               + lax.dot_general(pc_ref[hh], vc_ref[hh].astype(BF16), dims, preferred_element_type=F32))
        o_ref[:, head_cols(hh)] = (out * (1.0 / l_ref[hh])).astype(o_ref.dtype)

    logits(0)
    logits(1)
    softmax_rows(0)
    values(0)
    softmax_rows(1)
    values(1)


def _na_build_bias(rows_ref, bias_ref):
    first_visible = {slot: first for _, slot, first in map(_na_row_window, range(GRID_ROWS))}
    assert sorted(first_visible) == list(range(NA_SLOTS))
    qcol = lax.broadcasted_iota(jnp.int32, (GRID_W, GRID_W), 0)
    kcol = lax.broadcasted_iota(jnp.int32, (GRID_W, GRID_W), 1)
    cs = jnp.clip(qcol - NA_COLS // 2, 0, GRID_W - NA_COLS)
    valid = (kcol >= cs) & (kcol < cs + NA_COLS)
    masked = jnp.full((GRID_W, GRID_W), NEG_INF, F32)
    for hh in range(LANES // HEAD_DIM):
        toep = []
        for ri in range(2 * NA_ROWS - 1):
            rows = jnp.broadcast_to(rows_ref[hh, ri:ri + 1, :], (GRID_W, LANES))
            rolled = pltpu.roll(rows, 0, 1, stride=1, stride_axis=0)
            toep.append(jnp.where(valid, rolled[:, :GRID_W], NEG_INF))
        for d in range(NA_SLOTS):
            for j0 in range(0, NA_WIN_ROWS, LANES // GRID_W):
                pieces = []
                for j in range(j0, j0 + LANES // GRID_W):
                    visible = first_visible[d] <= j < first_visible[d] + NA_ROWS
                    pieces.append(toep[j - d + NA_ROWS - 1] if visible else masked)
                bias_ref[hh, d, :, j0 * GRID_W:j0 * GRID_W + LANES] = jnp.concatenate(pieces, axis=1)


def _na_bias_rows(rpb):
    n_h, n_ri, n_ci = rpb.shape
    return jnp.concatenate([rpb[..., NA_COLS - 1:], jnp.zeros((n_h, n_ri, LANES - n_ci), F32),
                            rpb[..., :NA_COLS - 1]], axis=-1)


def _na_attention(q, k, v, kc, vc, rpb):
    n_pairs = N_HEADS * HEAD_DIM // LANES
    lat = pl.BlockSpec((DEC_SEQ, LANES), lambda hp, b: (b, hp))
    ctx = pl.BlockSpec((None, LANES // HEAD_DIM, HEAD_DIM, PAST_LEN), lambda hp, b: (b, hp, 0, 0))
    return pl.pallas_call(
        _na_kernel,
        grid=(n_pairs, DEC_BATCH),
        in_specs=[lat, lat, lat, ctx, ctx,
                  pl.BlockSpec((LANES // HEAD_DIM, 2 * NA_ROWS - 1, LANES), lambda hp, b: (hp, 0, 0))],
        out_specs=lat,
        out_shape=jax.ShapeDtypeStruct((DEC_BATCH * DEC_SEQ, N_HEADS * HEAD_DIM), BF16),
        scratch_shapes=[pltpu.VMEM((LANES // HEAD_DIM, DEC_SEQ, DEC_SEQ), F32),
                        pltpu.VMEM((LANES // HEAD_DIM, DEC_SEQ, PAST_LEN), F32),
                        pltpu.VMEM((LANES // HEAD_DIM, DEC_SEQ, DEC_SEQ), BF16),
                        pltpu.VMEM((LANES // HEAD_DIM, DEC_SEQ, PAST_LEN), BF16),
                        pltpu.VMEM((LANES // HEAD_DIM, DEC_SEQ, 1), F32),
                        pltpu.VMEM((LANES // HEAD_DIM, NA_SLOTS, GRID_W, NA_WIN_KEYS), F32)],
        compiler_params=_params(2),
        name="na_attn",
    )(q, k, v, kc, vc, _na_bias_rows(rpb))


def _grid_angles(n, rot_dim):
    pos = jnp.arange(n, dtype=jnp.int32)
    row = (pos // GRID_W).astype(F32)
    col = (pos % GRID_W).astype(F32)
    n_ax = rot_dim // 4
    inv = ROPE_THETA ** (-jnp.arange(n_ax, dtype=F32) / n_ax)
    return jnp.concatenate([row[:, None] * inv, col[:, None] * inv], axis=-1)


def _pair_tables(ang):
    cos = jnp.repeat(jnp.cos(ang), 2, axis=-1)
    sin = jnp.stack([-jnp.sin(ang), jnp.sin(ang)], axis=-1).reshape(ang.shape[0], -1)
    return cos, sin


def _gqa_rope_tables():
    cos, sin = _pair_tables(_grid_angles(DEC_SEQ, HEAD_DIM))
    return jnp.tile(cos, (1, LANES // HEAD_DIM)), jnp.tile(sin, (1, LANES // HEAD_DIM))


def _mla_rope_tables():
    cos, sin = _pair_tables(_grid_angles(DEC_SEQ, C_ROPE))
    pad = LANES - C_QK
    cos = jnp.concatenate([jnp.ones((DEC_SEQ, C_NOPE), F32), cos, jnp.ones((DEC_SEQ, pad), F32)], axis=-1)
    sin = jnp.concatenate([jnp.zeros((DEC_SEQ, C_NOPE), F32), sin, jnp.zeros((DEC_SEQ, pad), F32)], axis=-1)
    return cos, sin


def _mla_weights(w_down, q_lnorm, kv_lnorm, w_uq, w_ukv, q_norm, k_norm):
    pad_head = lambda a: jnp.pad(a, [(0, 0)] * (a.ndim - 1) + [(0, C_HEAD_PAD - a.shape[-1])])
    ukv = w_ukv.reshape(C_KV_RANK, N_HEADS, C_NOPE + C_V)
    down = jnp.pad(w_down, ((0, 0), (0, C_DOWN_PAD - w_down.shape[1]))).astype(BF16)
    uq = pad_head(w_uq.reshape(C_Q_RANK, N_HEADS, C_QK)).reshape(C_Q_RANK, -1).astype(BF16)
    return {
        "down": down,
        "down_rope": jnp.concatenate([down, _swap_pair_columns(down[:, C_DOWN_PAD - LANES:])], axis=1),
        "q_lnorm": q_lnorm.reshape(1, C_Q_RANK),
        "kv_lnorm": kv_lnorm.reshape(1, C_KV_RANK),
        "uq": uq,
        "uq_sw": _swap_pair_columns(uq),
        "uk": pad_head(ukv[:, :, :C_NOPE]).reshape(C_KV_RANK, -1).astype(BF16),
        "uv": ukv[:, :, C_NOPE:].reshape(C_KV_RANK, -1).astype(BF16),
        "q_norm": pad_head(q_norm).reshape(1, C_HEAD_PAD),
        "k_norm": pad_head(k_norm).reshape(1, C_HEAD_PAD),
    }


def kernel(x_prompt, x_sample, cache_a_k, cache_a_v, cache_b_k, cache_b_v, cache_c_kv, cache_c_krope, cache_d_k, cache_d_v, c, c_ctx, mod_w, mod_b, norm_ff1, norm_mix, norm_ff2, ff1_w_gu, ff1_w_down, ff2_w_gu, ff2_w_down, a_w_qkv, a_q_norm, a_k_norm, a_sink, a_w_o, b_w_qkv, b_q_norm, b_k_norm, b_w_o, c_w_down, c_q_lnorm, c_kv_lnorm, c_w_uq, c_w_ukv, c_q_norm, c_k_norm, c_w_o, d_w_qkv, d_q_norm, d_k_norm, d_rpb, d_w_o):
    n_p, n_s = BATCH * SEQ, DEC_BATCH * DEC_SEQ
    prompt_tiles, latent_tiles, all_tiles = (0, N_PROMPT_TILES), (N_PROMPT_TILES, N_LATENT_TILES), (0, N_TILES)
    x_parts = [(x_prompt.reshape(n_p, D_MODEL),) + prompt_tiles, (x_sample.reshape(n_s, D_MODEL),) + latent_tiles]
    cond = jnp.concatenate([c_ctx[None], c, jnp.zeros((MOD_ROWS - 1 - DEC_BATCH, D_MODEL), F32)], axis=0)
    modv = _mod_table(cond, mod_w, mod_b)
    s_row = _sample_row(TOKEN_TILE)
    gqa_tabs = _gqa_rope_tables()
    gqa_scale = 1.0
    flat_cache = lambda a: a.reshape(DEC_BATCH * PAST_LEN, -1)
    keys_last = lambda a: jnp.transpose(a, (0, 2, 3, 1))
    keys_first = lambda a: jnp.transpose(a, (0, 3, 1, 2))[:, None]
    ffn_w = [(ff1_w_gu, ff1_w_down), (ff2_w_gu, ff2_w_down)]
    new = {}

    for i in range(DEPTH):
        kind, j = i % N_MIXERS, i // N_MIXERS
        x = _half_ffn(x_parts, modv, i, 0, norm_ff1[i], *ffn_w[0], all_tiles)

        if kind in (0, 1, 3):
            w_qkv, q_norm, k_norm, w_o, n_kv, cache_k, cache_v = {
                0: (a_w_qkv, a_q_norm, a_k_norm, a_w_o, GQA_KV_HEADS, cache_a_k, cache_a_v),
                1: (b_w_qkv, b_q_norm, b_k_norm, b_w_o, GQA_KV_HEADS, cache_b_k, cache_b_v),
                3: (d_w_qkv, d_q_norm, d_k_norm, d_w_o, N_HEADS, cache_d_k, cache_d_v)}[kind]
            w_qkv = w_qkv[j].astype(BF16)
            qp, kp, vp = _gqa_project(x, prompt_tiles, modv, i, norm_mix[i], w_qkv, q_norm[j], k_norm[j], n_kv,
                                      _prompt_row, None, True)
            qs, ks, vs = _gqa_project(x, latent_tiles, modv, i, norm_mix[i], w_qkv, q_norm[j], k_norm[j], n_kv,
                                      s_row, gqa_tabs if kind != 3 else None, False)
            op = _attention(qp, [[(kp, vp, SEQ, "fullT")]], n_batch=BATCH, n_tok=SEQ, tq=SEQ, n_kv=n_kv,
                            kv_step=n_kv, dk=HEAD_DIM, scale=gqa_scale, sink=a_sink[j] if kind == 0 else None)
            name = "abcd"[kind]
            new[name + "_k"], new[name + "_v"] = keys_first(kp), keys_first(vp)
            kc, vc = keys_last(cache_k[:, j]), keys_last(cache_v[:, j])
            if kind == 0:
                tq = 2 * A_WINDOW
                os_ = _attention(qs, [[(ks, vs, A_WINDOW, "prev"), (ks, vs, tq, "band"),
                                       (ks, vs, A_WINDOW, "next")], [(kc, vc, PAST_LEN, "fullT")]],
                                 n_batch=DEC_BATCH, n_tok=DEC_SEQ, tq=tq, n_kv=n_kv, kv_step=n_kv, dk=HEAD_DIM,
                                 scale=gqa_scale, sink=a_sink[j])
            elif kind == 1:
                os_ = _attention(qs, [[(ks, vs, DEC_SEQ, "full")], [(kc, vc, PAST_LEN, "fullT")]],
                                 n_batch=DEC_BATCH, n_tok=DEC_SEQ, tq=256, n_kv=n_kv, kv_step=n_kv, dk=HEAD_DIM,
                                 scale=gqa_scale)
            else:
                os_ = _na_attention(qs, ks, vs, kc, vc, d_rpb[j])
        else:
            w = _mla_weights(c_w_down[j], c_q_lnorm[j], c_kv_lnorm[j], c_w_uq[j], c_w_ukv[j],
                             c_q_norm[j], c_k_norm[j])
            w_o = c_w_o
            qp, kp, vp, ckv_p, kr_p = _mla_project(x, prompt_tiles, modv, i, norm_mix[i], w, _prompt_row, None)
            qs, ks, vs, _, _ = _mla_project(x, latent_tiles, modv, i, norm_mix[i], w, s_row, _mla_rope_tables())
            new["c_kv"] = ckv_p.reshape(BATCH, 1, SEQ, C_KV_RANK)
            new["c_krope"] = kr_p[:, C_NOPE:C_QK].reshape(BATCH, 1, SEQ, C_ROPE)
            kr_cache = jnp.pad(flat_cache(cache_c_krope[:, j]), ((0, 0), (C_NOPE, LANES - C_QK)))
            kc, vc = _mla_expand_cache(flat_cache(cache_c_kv[:, j]), kr_cache, w)
            mla_scale = C_QK ** -0.5
            op = _attention(qp, [[(kp, vp, SEQ, "full")]], n_batch=BATCH, n_tok=SEQ, tq=SEQ,
                            n_kv=N_HEADS, kv_step=N_HEADS, dk=C_HEAD_PAD, scale=mla_scale)
            os_ = _attention(qs, [[(ks, vs, DEC_SEQ, "full")], [(kc, vc, PAST_LEN, "full")]],
                             n_batch=DEC_BATCH, n_tok=DEC_SEQ, tq=DEC_SEQ, n_kv=N_HEADS, kv_step=4, dk=C_HEAD_PAD,
                             scale=mla_scale)

        w_o = w_o[j].astype(BF16)
        x_parts = [(x,) + all_tiles]
        attn_p, attn_s = (op,) + prompt_tiles, (os_,) + latent_tiles
        if i + 1 < DEPTH:
            x_parts = [(_half_ffn(x_parts, modv, i, 6, norm_ff2[i], *ffn_w[1], all_tiles, [attn_p, attn_s], w_o),)
                       + all_tiles]
    yp = _half_ffn(x_parts, modv, DEPTH - 1, 6, norm_ff2[-1], *ffn_w[1], prompt_tiles, [attn_p], w_o)
    ys = _half_ffn(x_parts, modv, DEPTH - 1, 6, norm_ff2[-1], *ffn_w[1], latent_tiles, [attn_s], w_o)

    return (yp.reshape(BATCH, SEQ, D_MODEL), ys.reshape(DEC_BATCH, DEC_SEQ, D_MODEL),
            new["a_k"], new["a_v"], new["b_k"], new["b_v"], new["c_kv"], new["c_krope"],
            new["d_k"], new["d_v"])
```

```python
import functools

import jax
import jax.numpy as jnp
import numpy as np
from jax import lax
from jax.experimental import pallas as pl
from jax.experimental.pallas import tpu as pltpu

F32 = jnp.float32
BF16 = jnp.bfloat16

D_MODEL = 1024
BATCH = 16
SEQ = 256
DEPTH = 4
DEC_BATCH = 8
DEC_SEQ = 1024
PAST_LEN = 512
GRID_W = 64
N_MIXERS = 4
RMS_EPS = 1e-6
ROPE_THETA = 10000.0
NEG_INF = -1e30
D_FF = 2816
FFN_RES = 0.5
N_MOD = 9
HEAD_DIM = 64
N_HEADS = 16
GQA_KV_HEADS = 4
A_WINDOW = 128
C_Q_RANK = 384
C_KV_RANK = 256
C_NOPE = 64
C_ROPE = 32
C_V = 64
C_QK = C_NOPE + C_ROPE
NA_ROWS = 8
NA_COLS = 16

LANES = 128
MOD_PARTS_PER_STEP = 3
MOD_ROWS = 16
C_DOWN_PAD = 768
C_HEAD_PAD = LANES
VMEM_LIMIT_BYTES = 56 * 1024 * 1024
TOKEN_TILE = 512
N_PROMPT_TILES = BATCH * SEQ // TOKEN_TILE
N_LATENT_TILES = DEC_BATCH * DEC_SEQ // TOKEN_TILE
N_TILES = N_PROMPT_TILES + N_LATENT_TILES
FF_CHUNK = 256
N_WEIGHT_SLOTS = 2
KEY_CHUNK = 1024
LOG2_E = 1.4426950408889634
GQA_SCALE = HEAD_DIM ** -0.5
GRID_ROWS = DEC_SEQ // GRID_W
NA_WIN_ROWS = NA_ROWS + 2
NA_WIN_KEYS = NA_WIN_ROWS * GRID_W
NA_SLOTS = NA_WIN_ROWS


def _params(n_axes):
    return pltpu.CompilerParams(dimension_semantics=("arbitrary",) * n_axes,
                                vmem_limit_bytes=VMEM_LIMIT_BYTES)


def _resident(shape):
    nd = len(shape)
    return pl.BlockSpec(shape, lambda *_: (0,) * nd, pipeline_mode=pl.Buffered(1))


def _mod_spec(layer, part, row_fn):
    base = (layer * N_MOD + part) * MOD_ROWS
    return pl.BlockSpec((None, 1, D_MODEL), lambda i: (base + row_fn(i), 0, 0))


def _prompt_row(i):
    return 0


def _sample_row(tile):
    per_batch = DEC_SEQ // tile
    return lambda i: 1 + i // per_batch


def _silu(a):
    return a * (1.0 / (1.0 + jnp.exp(-a)))


def _modulate(x, g, shift, scale):
    y = x * lax.rsqrt(jnp.mean(x * x, axis=-1, keepdims=True) + RMS_EPS)
    return (y * g) * (1.0 + scale) + shift


def _swap_pair_columns(a):
    even = jnp.arange(a.shape[-1]) % 2 == 0
    return jnp.where(even, jnp.roll(a, -1, axis=-1), jnp.roll(a, 1, axis=-1))


def _mod_kernel(c_ref, w_ref, b_ref, o_ref):
    s = _silu(c_ref[...]).astype(BF16)
    for p in range(MOD_PARTS_PER_STEP):
        cols = slice(p * D_MODEL, (p + 1) * D_MODEL)
        o_ref[p] = jnp.dot(s, w_ref[:, cols].astype(BF16), preferred_element_type=F32) + b_ref[:, cols]


def _mod_table(cond, mod_w, mod_b):
    out = pl.pallas_call(
        _mod_kernel,
        grid=(DEPTH, N_MOD // MOD_PARTS_PER_STEP),
        in_specs=[
            pl.BlockSpec((MOD_ROWS, D_MODEL), lambda l, j: (0, 0)),
            pl.BlockSpec((None, D_MODEL, MOD_PARTS_PER_STEP * D_MODEL), lambda l, j: (l, 0, j)),
            pl.BlockSpec((None, 1, MOD_PARTS_PER_STEP * D_MODEL), lambda l, j: (l, 0, j)),
        ],
        out_specs=pl.BlockSpec((None, MOD_PARTS_PER_STEP, MOD_ROWS, D_MODEL), lambda l, j: (l, j, 0, 0)),
        out_shape=jax.ShapeDtypeStruct((DEPTH, N_MOD, MOD_ROWS, D_MODEL), F32),
        compiler_params=_params(2),
        name="mod_table",
    )(cond, mod_w, mod_b.reshape(DEPTH, 1, N_MOD * D_MODEL))
    return out.reshape(DEPTH * N_MOD * MOD_ROWS, 1, D_MODEL)


def _ffn_kernel(*refs, n_x, n_attn, tile0, layer):
    x_refs, refs = refs[:n_x], refs[n_x:]
    if n_attn:
        attn_refs, (gm_ref, wo_ref), refs = refs[:n_attn], refs[n_attn:n_attn + 2], refs[n_attn + 2:]
    (sh_ref, sc_ref, gt_ref, g_ref, wgu_hbm, wd_hbm, o_ref,
     act_ref, wgu_ref, wd_ref, gate_stage, up_stage, down_stage, sems) = refs
    n_chunks = D_FF // FF_CHUNK

    def pick(sources):
        if len(sources) == 1:
            return sources[0][...]
        is_prompt = pl.program_id(0) + tile0 < N_PROMPT_TILES
        return jnp.where(is_prompt, sources[0][...], sources[1][...])

    def residual_in():
        if not n_attn:
            return pick(x_refs)
        return pick(x_refs) + gm_ref[...] * jnp.dot(pick(attn_refs), wo_ref[...], preferred_element_type=F32)

    def chunk_copies(c, slot):
        c0 = c * FF_CHUNK
        return (pltpu.make_async_copy(wgu_hbm.at[layer, :, pl.ds(c0, FF_CHUNK)], gate_stage.at[slot],
                                      sems.at[0, slot]),
                pltpu.make_async_copy(wgu_hbm.at[layer, :, pl.ds(D_FF + c0, FF_CHUNK)], up_stage.at[slot],
                                      sems.at[1, slot]),
                pltpu.make_async_copy(wd_hbm.at[layer, pl.ds(c0, FF_CHUNK), :], down_stage.at[slot],
                                      sems.at[2, slot]))

    def fetch_chunk(c):
        slot = c % N_WEIGHT_SLOTS
        c0 = c * FF_CHUNK
        for copy in chunk_copies(c, slot):
            copy.wait()
        wgu_ref[:, c0:c0 + FF_CHUNK] = gate_stage[slot].astype(BF16)
        wgu_ref[:, D_FF + c0:D_FF + c0 + FF_CHUNK] = up_stage[slot].astype(BF16)
        wd_ref[c0:c0 + FF_CHUNK, :] = down_stage[slot].astype(BF16)
        if c + N_WEIGHT_SLOTS < n_chunks:
            for copy in chunk_copies(c + N_WEIGHT_SLOTS, slot):
                copy.start()

    def tile(stream_weights):
        if stream_weights:
            for c in range(min(N_WEIGHT_SLOTS, n_chunks)):
                for copy in chunk_copies(c, c):
                    copy.start()
        x = residual_in()
        h = _modulate(x, g_ref[...], sh_ref[...], sc_ref[...]).astype(BF16)
        for c in range(n_chunks):
            if stream_weights:
                fetch_chunk(c)
            c0 = c * FF_CHUNK
            a = jnp.dot(h, wgu_ref[:, c0:c0 + FF_CHUNK], preferred_element_type=F32)
            u = jnp.dot(h, wgu_ref[:, D_FF + c0:D_FF + c0 + FF_CHUNK], preferred_element_type=F32)
            act_ref[:, c0:c0 + FF_CHUNK] = (_silu(a) * u).astype(BF16)
        y = jnp.dot(act_ref[...], wd_ref[...], preferred_element_type=F32)
        o_ref[...] = x + (FFN_RES * gt_ref[...]) * y

    @pl.when(pl.program_id(0) == 0)
    def _():
        tile(True)

    @pl.when(pl.program_id(0) > 0)
    def _():
        tile(False)


def _tile_row(t):
    return jnp.where(t < N_PROMPT_TILES, 0, 1 + (t - N_PROMPT_TILES) // (DEC_SEQ // TOKEN_TILE))


def _half_ffn(x_parts, modv, layer, part0, g, wgu, wd, tiles, attn_parts=None, w_o=None):
    tile0, n_tiles = tiles

    def part_specs(parts):
        specs = []
        for _, first, count in parts:
            specs.append(pl.BlockSpec((TOKEN_TILE, D_MODEL),
                                      lambda i, first=first, count=count: (jnp.clip(i + tile0 - first, 0, count - 1), 0)))
        return specs

    def mod_spec(part):
        base = (layer * N_MOD + part) * MOD_ROWS
        return pl.BlockSpec((None, 1, D_MODEL), lambda i: (base + _tile_row(i + tile0), 0, 0))

    in_specs, args = part_specs(x_parts), [a for a, _, _ in x_parts]
    if attn_parts is not None:
        in_specs += part_specs(attn_parts) + [mod_spec(5), _resident((D_MODEL, D_MODEL))]
        args += [a for a, _, _ in attn_parts] + [modv, w_o]
    in_specs += [mod_spec(part0), mod_spec(part0 + 1), mod_spec(part0 + 2),
                 _resident((1, D_MODEL)), pl.BlockSpec(memory_space=pl.ANY), pl.BlockSpec(memory_space=pl.ANY)]
    args += [modv, modv, modv, g.reshape(1, D_MODEL), wgu, wd]
    return pl.pallas_call(
        functools.partial(_ffn_kernel, n_x=len(x_parts), n_attn=len(attn_parts or ()), tile0=tile0, layer=layer),
        grid=(n_tiles,),
        in_specs=in_specs,
        out_specs=pl.BlockSpec((TOKEN_TILE, D_MODEL), lambda i: (i, 0)),
        out_shape=jax.ShapeDtypeStruct((n_tiles * TOKEN_TILE, D_MODEL), F32),
        scratch_shapes=[pltpu.VMEM((TOKEN_TILE, D_FF), BF16),
                        pltpu.VMEM((D_MODEL, 2 * D_FF), BF16),
                        pltpu.VMEM((D_FF, D_MODEL), BF16),
                        pltpu.VMEM((N_WEIGHT_SLOTS, D_MODEL, FF_CHUNK), F32),
                        pltpu.VMEM((N_WEIGHT_SLOTS, D_MODEL, FF_CHUNK), F32),
                        pltpu.VMEM((N_WEIGHT_SLOTS, FF_CHUNK, D_MODEL), F32),
                        pltpu.SemaphoreType.DMA((3, N_WEIGHT_SLOTS))],
        compiler_params=_params(1),
        name=f"ffn_{n_tiles}" + ("_mix" if attn_parts is not None else ""),
    )(*args)


def _head_pair_rstd(yp, lo):
    sq = yp * yp
    s_lo = jnp.sum(jnp.where(lo, sq, 0.0), axis=-1, keepdims=True)
    s_hi = jnp.sum(jnp.where(lo, 0.0, sq), axis=-1, keepdims=True)
    ms = jnp.where(lo, s_lo, s_hi) * (1.0 / HEAD_DIM)
    return lax.rsqrt(ms + RMS_EPS)


def _gqa_proj_kernel(*refs, n_q, n_kv, rope, kv_transposed):
    x_ref, sh_ref, sc_ref, g_ref, w_ref, qn_ref, kn_ref = refs[:7]
    if rope:
        wsw_ref, qc_ref, qs_ref, kc_ref, ks_ref = refs[7:12]
        q_ref, k_ref, v_ref = refs[12:]
    else:
        q_ref, k_ref, v_ref = refs[7:]
    h = _modulate(x_ref[...], g_ref[...], sh_ref[...], sc_ref[...]).astype(BF16)
    lane = lax.broadcasted_iota(jnp.int32, (1, LANES), 1)
    lo = lane < HEAD_DIM
    q_cols = n_q * HEAD_DIM
    k_cols = n_kv * HEAD_DIM

    def store(out_ref, o0, y, transposed):
        if not transposed:
            out_ref[:, o0:o0 + LANES] = y.astype(out_ref.dtype)
            return
        yt = y.T
        for b_i in range(y.shape[0] // SEQ):
            for h_i in range(LANES // HEAD_DIM):
                out_ref[b_i, o0 // HEAD_DIM + h_i] = yt[h_i * HEAD_DIM:(h_i + 1) * HEAD_DIM,
                                                        b_i * SEQ:(b_i + 1) * SEQ]

    def normed(c0, gain_ref, tabs, out_ref, o0, post_scale, transposed):
        y = jnp.dot(h, w_ref[:, c0:c0 + 2 * LANES], preferred_element_type=F32)
        if rope:
            y_sw = jnp.dot(h, wsw_ref[:, c0:c0 + 2 * LANES], preferred_element_type=F32)
        for p in range(2):
            part = slice(p * LANES, (p + 1) * LANES)
            rstd = _head_pair_rstd(y[:, part], lo)
            if rope:
                yn = (y[:, part] * tabs[0][...] + y_sw[:, part] * tabs[1][...]) * rstd
            else:
                yn = (y[:, part] * rstd) * gain_ref[...]
            if post_scale is not None:
                yn = yn * post_scale
            store(out_ref, o0 + p * LANES, yn, transposed)

    for c0 in range(0, q_cols, 2 * LANES):
        normed(c0, qn_ref, (qc_ref, qs_ref) if rope else None, q_ref, c0, GQA_SCALE, False)
    for c0 in range(0, k_cols, 2 * LANES):
        normed(q_cols + c0, kn_ref, (kc_ref, ks_ref) if rope else None, k_ref, c0, None, kv_transposed)
    for c0 in range(0, k_cols, 2 * LANES):
        v = jnp.dot(h, w_ref[:, q_cols + k_cols + c0:q_cols + k_cols + c0 + 2 * LANES],
                    preferred_element_type=F32)
        for p in range(2):
            store(v_ref, c0 + p * LANES, v[:, p * LANES:(p + 1) * LANES], kv_transposed)


def _gqa_project(x, tiles, modv, layer, g_mix, w_qkv, q_norm, k_norm, n_kv, row_fn, rope_tabs, kv_transposed):
    tile0, n_tiles = tiles
    n_tok = n_tiles * TOKEN_TILE
    q_cols, k_cols = N_HEADS * HEAD_DIM, n_kv * HEAD_DIM
    rope = rope_tabs is not None
    if kv_transposed:
        per_tile = TOKEN_TILE // SEQ
        kv_spec = pl.BlockSpec((per_tile, n_kv, HEAD_DIM, SEQ), lambda i: (i, 0, 0, 0))
        kv_shape = jax.ShapeDtypeStruct((n_tok // SEQ, n_kv, HEAD_DIM, SEQ), F32)
    else:
        kv_spec = pl.BlockSpec((TOKEN_TILE, k_cols), lambda i: (i, 0))
        kv_shape = jax.ShapeDtypeStruct((n_tok, k_cols), BF16)
    tile = lambda w: pl.BlockSpec((TOKEN_TILE, w), lambda i: (i, 0))
    in_specs = [pl.BlockSpec((TOKEN_TILE, D_MODEL), lambda i: (i + tile0, 0)),
                _mod_spec(layer, 3, row_fn), _mod_spec(layer, 4, row_fn),
                _resident((1, D_MODEL)), _resident((D_MODEL, q_cols + 2 * k_cols)),
                _resident((1, LANES)), _resident((1, LANES))]
    pair_tile = lambda g: jnp.tile(g, LANES // HEAD_DIM).reshape(1, LANES)
    args = [x, modv, modv, g_mix.reshape(1, D_MODEL), w_qkv, pair_tile(q_norm), pair_tile(k_norm)]
    if rope:
        per_batch = DEC_SEQ // TOKEN_TILE
        tab = pl.BlockSpec((TOKEN_TILE, LANES), lambda i: (i % per_batch, 0))
        cos, sin = rope_tabs
        in_specs += [_resident((D_MODEL, q_cols + k_cols)), tab, tab, tab, tab]
        args += [_swap_pair_columns(w_qkv[:, :q_cols + k_cols]),
                 cos * pair_tile(q_norm), sin * pair_tile(_swap_pair_columns(q_norm)),
                 cos * pair_tile(k_norm), sin * pair_tile(_swap_pair_columns(k_norm))]
    return pl.pallas_call(
        functools.partial(_gqa_proj_kernel, n_q=N_HEADS, n_kv=n_kv, rope=rope, kv_transposed=kv_transposed),
        grid=(n_tok // TOKEN_TILE,),
        in_specs=in_specs,
        out_specs=[tile(q_cols), kv_spec, kv_spec],
        out_shape=[jax.ShapeDtypeStruct((n_tok, q_cols), BF16), kv_shape, kv_shape],
        compiler_params=_params(1),
        name=f"gqa_proj_{n_tok}_kv{n_kv}",
    )(*args)


def _mla_group_rstd(y):
    ms = jnp.sum(y * y, axis=-1, keepdims=True) * (1.0 / C_QK)
    return lax.rsqrt(ms + RMS_EPS)


def _mla_expand_kv(ckv, kr128, wk_ref, wv_ref, kn_ref, rope, k_ref, v_ref):
    for c0 in range(0, N_HEADS * C_HEAD_PAD, 2 * LANES):
        y = jnp.dot(ckv, wk_ref[:, c0:c0 + 2 * LANES], preferred_element_type=F32)
        for p in range(2):
            kp = y[:, p * LANES:(p + 1) * LANES] + kr128
            rstd = _mla_group_rstd(kp)
            if rope is None:
                kn = (kp * rstd) * kn_ref[...]
            else:
                kn = (kp * rope[0][...] + rope[2] * rope[1][...]) * rstd
            k_ref[:, c0 + p * LANES:c0 + (p + 1) * LANES] = kn.astype(k_ref.dtype)
    for c0 in range(0, N_HEADS * C_V, 2 * LANES):
        v = jnp.dot(ckv, wv_ref[:, c0:c0 + 2 * LANES], preferred_element_type=F32)
        v_ref[:, c0:c0 + 2 * LANES] = v.astype(v_ref.dtype)


def _mla_proj_kernel(*refs, rope):
    (x_ref, sh_ref, sc_ref, g_ref, wd_ref, qln_ref, kvln_ref, wuq_ref, wk_ref, wv_ref,
     qn_ref, kn_ref) = refs[:12]
    if rope:
        wuq_sw_ref, qc_ref, qs_ref, kc_ref, ks_ref = refs[12:17]
        q_ref, k_ref, v_ref, ckv_ref, kr_ref = refs[17:]
    else:
        q_ref, k_ref, v_ref, ckv_ref, kr_ref = refs[12:]
    h = _modulate(x_ref[...], g_ref[...], sh_ref[...], sc_ref[...]).astype(BF16)
    y = jnp.dot(h, wd_ref[...], preferred_element_type=F32)

    def row_norm(z, gain):
        return (z * lax.rsqrt(jnp.mean(z * z, axis=-1, keepdims=True) + RMS_EPS)) * gain

    cq = row_norm(y[:, :C_Q_RANK], qln_ref[...]).astype(BF16)
    ckv = row_norm(y[:, C_Q_RANK:C_Q_RANK + C_KV_RANK], kvln_ref[...])
    kr128 = pltpu.roll(y[:, C_Q_RANK + C_KV_RANK:C_DOWN_PAD], C_NOPE, 1)
    ckv_ref[...] = ckv
    kr_ref[...] = kr128

    for c0 in range(0, N_HEADS * C_HEAD_PAD, 2 * LANES):
        yq = jnp.dot(cq, wuq_ref[:, c0:c0 + 2 * LANES], preferred_element_type=F32)
        if rope:
            yq_sw = jnp.dot(cq, wuq_sw_ref[:, c0:c0 + 2 * LANES], preferred_element_type=F32)
        for p in range(2):
            part = slice(p * LANES, (p + 1) * LANES)
            rstd = _mla_group_rstd(yq[:, part])
            if rope:
                qn = (yq[:, part] * qc_ref[...] + yq_sw[:, part] * qs_ref[...]) * rstd
            else:
                qn = (yq[:, part] * rstd) * qn_ref[...]
            q_ref[:, c0 + p * LANES:c0 + (p + 1) * LANES] = qn.astype(q_ref.dtype)
    k_rope = (kc_ref, ks_ref, pltpu.roll(y[:, C_DOWN_PAD:], C_NOPE, 1)) if rope else None
    _mla_expand_kv(ckv.astype(BF16), kr128, wk_ref, wv_ref, kn_ref, k_rope, k_ref, v_ref)


def _mla_project(x, tiles, modv, layer, g_mix, w, row_fn, rope_tabs):
    tile0, n_tiles = tiles
    n_tok = n_tiles * TOKEN_TILE
    rope = rope_tabs is not None
    tile = lambda wd: pl.BlockSpec((TOKEN_TILE, wd), lambda i: (i, 0))
    qk_cols = N_HEADS * C_HEAD_PAD
    w_down = w["down_rope"] if rope else w["down"]
    in_specs = [pl.BlockSpec((TOKEN_TILE, D_MODEL), lambda i: (i + tile0, 0)),
                _mod_spec(layer, 3, row_fn), _mod_spec(layer, 4, row_fn),
                _resident((1, D_MODEL)), _resident(w_down.shape),
                _resident((1, C_Q_RANK)), _resident((1, C_KV_RANK)),
                _resident((C_Q_RANK, qk_cols)), _resident((C_KV_RANK, qk_cols)),
                _resident((C_KV_RANK, N_HEADS * C_V)),
                _resident((1, LANES)), _resident((1, LANES))]
    args = [x, modv, modv, g_mix.reshape(1, D_MODEL), w_down, w["q_lnorm"], w["kv_lnorm"],
            w["uq"], w["uk"], w["uv"], w["q_norm"], w["k_norm"]]
    if rope:
        per_batch = DEC_SEQ // TOKEN_TILE
        tab = pl.BlockSpec((TOKEN_TILE, LANES), lambda i: (i % per_batch, 0))
        cos, sin = rope_tabs
        in_specs += [_resident((C_Q_RANK, qk_cols)), tab, tab, tab, tab]
        args += [w["uq_sw"], cos * w["q_norm"], sin * _swap_pair_columns(w["q_norm"]),
                 cos * w["k_norm"], sin * _swap_pair_columns(w["k_norm"])]
    return pl.pallas_call(
        functools.partial(_mla_proj_kernel, rope=rope),
        grid=(n_tok // TOKEN_TILE,),
        in_specs=in_specs,
        out_specs=[tile(qk_cols), tile(qk_cols), tile(N_HEADS * C_V), tile(C_KV_RANK), tile(LANES)],
        out_shape=[jax.ShapeDtypeStruct((n_tok, qk_cols), BF16),
                   jax.ShapeDtypeStruct((n_tok, qk_cols), BF16),
                   jax.ShapeDtypeStruct((n_tok, N_HEADS * C_V), BF16),
                   jax.ShapeDtypeStruct((n_tok, C_KV_RANK), F32),
                   jax.ShapeDtypeStruct((n_tok, LANES), F32)],
        compiler_params=_params(1),
        name=f"mla_proj_{n_tok}",
    )(*args)


def _mla_cache_kernel(ckv_ref, kr_ref, wk_ref, wv_ref, kn_ref, k_ref, v_ref):
    _mla_expand_kv(ckv_ref[...].astype(BF16), kr_ref[...], wk_ref, wv_ref, kn_ref, None, k_ref, v_ref)


def _mla_expand_cache(ckv, kr128, w):
    n_tok = ckv.shape[0]
    tile = lambda wd: pl.BlockSpec((TOKEN_TILE, wd), lambda i: (i, 0))
    qk_cols = N_HEADS * C_HEAD_PAD
    return pl.pallas_call(
        _mla_cache_kernel,
        grid=(n_tok // TOKEN_TILE,),
        in_specs=[tile(C_KV_RANK), tile(LANES), _resident((C_KV_RANK, qk_cols)),
                  _resident((C_KV_RANK, N_HEADS * C_V)), _resident((1, LANES))],
        out_specs=[tile(qk_cols), tile(N_HEADS * C_V)],
        out_shape=[jax.ShapeDtypeStruct((n_tok, qk_cols), BF16),
                   jax.ShapeDtypeStruct((n_tok, N_HEADS * C_V), BF16)],
        compiler_params=_params(1),
        name="mla_cache_kv",
    )(ckv, kr128, w["uk"], w["uv"], w["k_norm"])


def _online_softmax(q, segs, scale, sink):
    dims = (((1,), (1,)), ((), ()))
    to_exp2 = scale * LOG2_E
    m = None
    acc = None
    for k, v1, bias, transposed in segs:
        if transposed:
            s = jnp.dot(q, k, preferred_element_type=F32)
        else:
            s = lax.dot_general(q, k, dims, preferred_element_type=F32)
        if bias is not None:
            s = s + bias
        row_max = jnp.max(s, axis=-1, keepdims=True)
        m_new = row_max if m is None else jnp.maximum(m, row_max)
        p = jnp.exp2((s - m_new) * to_exp2).astype(BF16)
        if transposed:
            pv = lax.dot_general(p, v1, dims, preferred_element_type=F32)
        else:
            pv = jnp.dot(p, v1, preferred_element_type=F32)
        acc = pv if m is None else acc * jnp.exp2((m - m_new) * to_exp2) + pv
        m = m_new
    if sink is not None:
        lane = lax.broadcasted_iota(jnp.int32, (1, 2 * HEAD_DIM), 1)
        acc = acc + jnp.where(lane >= HEAD_DIM, jnp.exp2((sink - m * scale) * LOG2_E), 0.0)
    denom = pltpu.roll(acc, HEAD_DIM, 1)
    return (acc * (1.0 / denom))[:, :HEAD_DIM]


def _attn_kernel(*refs, kv_step, group, dk, scale, seg_kinds, has_sink, tq):
    n_piece = sum(len(kinds) for kinds in seg_kinds)
    q_ref = refs[0]
    kv_refs = refs[1:1 + 2 * n_piece]
    sink_ref = refs[1 + 2 * n_piece] if has_sink else None
    o_ref = refs[-1]
    first_head = pl.program_id(1) * (kv_step * group)
    blk = pl.program_id(2)
    n_blk = pl.num_programs(2)

    def cat(parts, axis):
        return parts[0] if len(parts) == 1 else jnp.concatenate(parts, axis=axis)

    def piece_bias(kind, n_keys):
        r = lax.broadcasted_iota(jnp.int32, (tq, n_keys), 0)
        c = lax.broadcasted_iota(jnp.int32, (tq, n_keys), 1)
        if kind == "prev":
            ok = (c >= r) & (blk > 0)
        elif kind == "next":
            ok = (c <= r - (tq - A_WINDOW)) & (blk < n_blk - 1)
        elif kind == "band":
            ok = jnp.abs(r - c) <= A_WINDOW
        else:
            return jnp.zeros((tq, n_keys), F32)
        return jnp.where(ok, 0.0, NEG_INF)

    seg_refs, seg_bias = [], []
    p_i = 0
    for kinds in seg_kinds:
        pieces = [(kv_refs[2 * (p_i + n)], kv_refs[2 * (p_i + n) + 1]) for n in range(len(kinds))]
        p_i += len(kinds)
        seg_refs.append(pieces)
        if all(kind in ("full", "fullT") for kind in kinds):
            seg_bias.append(None)
        else:
            bias = cat([piece_bias(kind, kr.shape[0]) for kind, (kr, _) in zip(kinds, pieces)], 1)
            seg_bias.append(cat([bias] * group, 0))

    if seg_kinds in ((("full",),), (("fullT",),)):
        transposed = seg_kinds[0][0] == "fullT"
        kr, vr = seg_refs[0][0]
        contract = lambda a_dim, b_dim: (((a_dim,), (b_dim,)), ((), ()))
        logits = []
        for hk in range(kv_step):
            q = cat([q_ref[:, h * dk:(h + 1) * dk] for h in range(hk * group, (hk + 1) * group)], 0)
            k = kr[hk].astype(BF16) if transposed else kr[:, hk * dk:(hk + 1) * dk].astype(BF16)
            logits.append(lax.dot_general(k, q, contract(0 if transposed else 1, 1),
                                          preferred_element_type=F32))
        probs, denoms = [], []
        for hk, s_t in enumerate(logits):
            m = jnp.max(s_t, axis=0, keepdims=True)
            p_t = jnp.exp2((s_t - m) * (scale * LOG2_E))
            denom = jnp.sum(p_t, axis=0, keepdims=True)
            if has_sink:
                sink = cat([jnp.full((1, tq), sink_ref[first_head + h], F32)
                            for h in range(hk * group, (hk + 1) * group)], 1)
                denom = denom + jnp.exp2((sink - m * scale) * LOG2_E)
            probs.append(p_t.astype(BF16))
            denoms.append(denom)
        for hk in range(kv_step):
            v = vr[hk].astype(BF16) if transposed else vr[:, hk * HEAD_DIM:(hk + 1) * HEAD_DIM].astype(BF16)
            o_t = lax.dot_general(v, probs[hk], contract(1 if transposed else 0, 0),
                                  preferred_element_type=F32) * (1.0 / denoms[hk])
            for g_i in range(group):
                h = hk * group + g_i
                o_ref[:, h * HEAD_DIM:(h + 1) * HEAD_DIM] = o_t[:, g_i * tq:(g_i + 1) * tq].T.astype(o_ref.dtype)
        return

    for hk in range(kv_step):
        heads = range(hk * group, (hk + 1) * group)
        q = cat([q_ref[:, h * dk:(h + 1) * dk] for h in heads], 0)
        segs = []
        for kinds, pieces, bias in zip(seg_kinds, seg_refs, seg_bias):
            if kinds == ("fullT",):
                k = pieces[0][0][hk].astype(BF16)
                v = pieces[0][1][hk].astype(BF16)
                v1 = jnp.concatenate([v, jnp.ones_like(v)], axis=0)
                for c0 in range(0, k.shape[1], KEY_CHUNK):
                    segs.append((k[:, c0:c0 + KEY_CHUNK], v1[:, c0:c0 + KEY_CHUNK], None, True))
                continue
            k = cat([kr[:, hk * dk:(hk + 1) * dk].astype(BF16) for kr, _ in pieces], 0)
            v = cat([vr[:, hk * HEAD_DIM:(hk + 1) * HEAD_DIM].astype(BF16) for _, vr in pieces], 0)
            v1 = jnp.concatenate([v, jnp.ones_like(v)], axis=1)
            for c0 in range(0, k.shape[0], KEY_CHUNK):
                chunk = slice(c0, c0 + KEY_CHUNK)
                segs.append((k[chunk], v1[chunk], None if bias is None else bias[:, chunk], False))
        sink = (cat([jnp.full((tq, 2 * HEAD_DIM), sink_ref[first_head + h], F32) for h in heads], 0)
                if has_sink else None)
        out = _online_softmax(q, segs, scale, sink)
        for g_i, h in enumerate(heads):
            o_ref[:, h * HEAD_DIM:(h + 1) * HEAD_DIM] = out[g_i * tq:(g_i + 1) * tq].astype(o_ref.dtype)


def _attention(q, segs, *, n_batch, n_tok, tq, n_kv, kv_step, dk, scale, sink=None):
    n_blk = n_tok // tq
    per_blk = tq // A_WINDOW
    n_win = n_tok // A_WINDOW
    group = N_HEADS // n_kv
    in_specs = [pl.BlockSpec((tq, kv_step * group * dk), lambda b, g, i: (b * n_blk + i, g))]
    args = [q]
    for pieces in segs:
        for k, v, rows, kind in pieces:
            if kind == "fullT":
                idx = lambda b, g, i: (b, g, 0, 0)
                in_specs += [pl.BlockSpec((None, kv_step, dk, rows), idx),
                             pl.BlockSpec((None, kv_step, HEAD_DIM, rows), idx)]
                args += [k, v]
                continue
            if kind == "full":
                idx = lambda b, g, i: (b, g)
            elif kind == "band":
                idx = lambda b, g, i: (b * n_blk + i, g)
            elif kind == "prev":
                idx = lambda b, g, i: (b * n_win + jnp.maximum(i * per_blk - 1, 0), g)
            else:
                idx = lambda b, g, i: (b * n_win + jnp.minimum((i + 1) * per_blk, n_win - 1), g)
            in_specs += [pl.BlockSpec((rows, kv_step * dk), idx), pl.BlockSpec((rows, kv_step * HEAD_DIM), idx)]
            args += [k, v]
    seg_kinds = tuple(tuple(kind for _, _, _, kind in pieces) for pieces in segs)
    if sink is not None:
        in_specs.append(pl.BlockSpec(memory_space=pltpu.SMEM))
        args.append(sink)
    return pl.pallas_call(
        functools.partial(_attn_kernel, kv_step=kv_step, group=group, dk=dk, scale=scale,
                          seg_kinds=seg_kinds, has_sink=sink is not None, tq=tq),
        grid=(n_batch, n_kv // kv_step, n_blk),
        in_specs=in_specs,
        out_specs=pl.BlockSpec((tq, kv_step * group * HEAD_DIM), lambda b, g, i: (b * n_blk + i, g)),
        out_shape=jax.ShapeDtypeStruct((n_batch * n_tok, N_HEADS * HEAD_DIM), BF16),
        compiler_params=_params(3),
        name=f"attn_{n_batch}x{n_tok}_kv{n_kv}_dk{dk}_" + "_".join(k[0] for ks in seg_kinds for k in ks),
    )(*args)


def _na_row_window(r):
    rs = min(max(r - NA_ROWS // 2, 0), GRID_ROWS - NA_ROWS)
    ws = min(rs - rs % 2, GRID_ROWS - NA_WIN_ROWS)
    return ws, r - ws, rs - ws


def _na_kernel(q_ref, k_ref, v_ref, kc_ref, vc_ref, rows_ref, o_ref,
               s_ref, sc_ref, p_ref, pc_ref, l_ref, bias_ref):
    dims = (((1,), (1,)), ((), ()))

    @pl.when((pl.program_id(0) == 0) & (pl.program_id(1) == 0))
    def _():
        p_ref[...] = jnp.zeros_like(p_ref)

    @pl.when(pl.program_id(1) == 0)
    def _():
        _na_build_bias(rows_ref, bias_ref)

    head_cols = lambda hh: slice(hh * HEAD_DIM, (hh + 1) * HEAD_DIM)

    def logits(hh):
        q = q_ref[:, head_cols(hh)]
        s_ref[hh] = lax.dot_general(q, k_ref[:, head_cols(hh)], dims, preferred_element_type=F32)
        sc_ref[hh] = jnp.dot(q, kc_ref[hh].astype(BF16), preferred_element_type=F32)

    def softmax_rows(hh):
        for r in range(GRID_ROWS):
            ws, slot, _ = _na_row_window(r)
            rows = slice(r * GRID_W, (r + 1) * GRID_W)
            win = slice(ws * GRID_W, (ws + NA_WIN_ROWS) * GRID_W)
            s_n = s_ref[hh, rows, win] + bias_ref[hh, slot]
            s_c = sc_ref[hh, rows, :]
            m = jnp.maximum(jnp.max(s_n, axis=-1, keepdims=True), jnp.max(s_c, axis=-1, keepdims=True))
            p_n = jnp.exp(s_n - m)
            p_c = jnp.exp(s_c - m)
            l_ref[hh, rows, :] = jnp.sum(p_n, axis=-1, keepdims=True) + jnp.sum(p_c, axis=-1, keepdims=True)
            p_ref[hh, rows, win] = p_n.astype(BF16)
            pc_ref[hh, rows, :] = p_c.astype(BF16)

    def values(hh):
        out = (jnp.dot(p_ref[hh], v_ref[:, head_cols(hh)], preferred_element_type=F32)
               + lax.dot_general(pc_ref[hh], vc_ref[hh].astype(BF16), dims, preferred_element_type=F32))
        o_ref[:, head_cols(hh)] = (out * (1.0 / l_ref[hh])).astype(o_ref.dtype)

    logits(0)
    logits(1)
    softmax_rows(0)
    values(0)
    softmax_rows(1)
    values(1)


def _na_build_bias(rows_ref, bias_ref):
    first_visible = {slot: first for _, slot, first in map(_na_row_window, range(GRID_ROWS))}
    assert sorted(first_visible) == list(range(NA_SLOTS))
    qcol = lax.broadcasted_iota(jnp.int32, (GRID_W, GRID_W), 0)
    kcol = lax.broadcasted_iota(jnp.int32, (GRID_W, GRID_W), 1)
    cs = jnp.clip(qcol - NA_COLS // 2, 0, GRID_W - NA_COLS)
    valid = (kcol >= cs) & (kcol < cs + NA_COLS)
    masked = jnp.full((GRID_W, GRID_W), NEG_INF, F32)
    for hh in range(LANES // HEAD_DIM):
        toep = []
        for ri in range(2 * NA_ROWS - 1):
            rows = jnp.broadcast_to(rows_ref[hh, ri:ri + 1, :], (GRID_W, LANES))
            rolled = pltpu.roll(rows, 0, 1, stride=1, stride_axis=0)
            toep.append(jnp.where(valid, rolled[:, :GRID_W], NEG_INF))
        for d in range(NA_SLOTS):
            for j0 in range(0, NA_WIN_ROWS, LANES // GRID_W):
                pieces = []
                for j in range(j0, j0 + LANES // GRID_W):
                    visible = first_visible[d] <= j < first_visible[d] + NA_ROWS
                    pieces.append(toep[j - d + NA_ROWS - 1] if visible else masked)
                bias_ref[hh, d, :, j0 * GRID_W:j0 * GRID_W + LANES] = jnp.concatenate(pieces, axis=1)


def _na_bias_rows(rpb):
    n_h, n_ri, n_ci = rpb.shape
    return jnp.concatenate([rpb[..., NA_COLS - 1:], jnp.zeros((n_h, n_ri, LANES - n_ci), F32),
                            rpb[..., :NA_COLS - 1]], axis=-1)


def _na_attention(q, k, v, kc, vc, rpb):
    n_pairs = N_HEADS * HEAD_DIM // LANES
    lat = pl.BlockSpec((DEC_SEQ, LANES), lambda hp, b: (b, hp))
    ctx = pl.BlockSpec((None, LANES // HEAD_DIM, HEAD_DIM, PAST_LEN), lambda hp, b: (b, hp, 0, 0))
    return pl.pallas_call(
        _na_kernel,
        grid=(n_pairs, DEC_BATCH),
        in_specs=[lat, lat, lat, ctx, ctx,
                  pl.BlockSpec((LANES // HEAD_DIM, 2 * NA_ROWS - 1, LANES), lambda hp, b: (hp, 0, 0))],
        out_specs=lat,
        out_shape=jax.ShapeDtypeStruct((DEC_BATCH * DEC_SEQ, N_HEADS * HEAD_DIM), BF16),
        scratch_shapes=[pltpu.VMEM((LANES // HEAD_DIM, DEC_SEQ, DEC_SEQ), F32),
                        pltpu.VMEM((LANES // HEAD_DIM, DEC_SEQ, PAST_LEN), F32),
                        pltpu.VMEM((LANES // HEAD_DIM, DEC_SEQ, DEC_SEQ), BF16),
                        pltpu.VMEM((LANES // HEAD_DIM, DEC_SEQ, PAST_LEN), BF16),
                        pltpu.VMEM((LANES // HEAD_DIM, DEC_SEQ, 1), F32),
                        pltpu.VMEM((LANES // HEAD_DIM, NA_SLOTS, GRID_W, NA_WIN_KEYS), F32)],
        compiler_params=_params(2),
        name="na_attn",
    )(q, k, v, kc, vc, _na_bias_rows(rpb))


def _grid_angles(n, rot_dim):
    pos = np.arange(n)
    n_ax = rot_dim // 4
    inv = ROPE_THETA ** (-np.arange(n_ax, dtype=np.float64) / n_ax)
    return np.concatenate([(pos // GRID_W)[:, None] * inv, (pos % GRID_W)[:, None] * inv], axis=-1)


def _pair_tables(ang):
    cos = np.repeat(np.cos(ang), 2, axis=-1)
    sin = np.stack([-np.sin(ang), np.sin(ang)], axis=-1).reshape(ang.shape[0], -1)
    return cos, sin


def _gqa_rope_tables():
    cos, sin = _pair_tables(_grid_angles(DEC_SEQ, HEAD_DIM))
    reps = (1, LANES // HEAD_DIM)
    return np.tile(cos, reps).astype(np.float32), np.tile(sin, reps).astype(np.float32)


def _mla_rope_tables():
    cos, sin = _pair_tables(_grid_angles(DEC_SEQ, C_ROPE))
    pad = LANES - C_QK
    cos = np.concatenate([np.ones((DEC_SEQ, C_NOPE)), cos, np.ones((DEC_SEQ, pad))], axis=-1)
    sin = np.concatenate([np.zeros((DEC_SEQ, C_NOPE)), sin, np.zeros((DEC_SEQ, pad))], axis=-1)
    return cos.astype(np.float32), sin.astype(np.float32)


def _mla_weights(w_down, q_lnorm, kv_lnorm, w_uq, w_ukv, q_norm, k_norm):
    pad_head = lambda a: jnp.pad(a, [(0, 0)] * (a.ndim - 1) + [(0, C_HEAD_PAD - a.shape[-1])])
    ukv = w_ukv.reshape(C_KV_RANK, N_HEADS, C_NOPE + C_V)
    down = jnp.pad(w_down, ((0, 0), (0, C_DOWN_PAD - w_down.shape[1]))).astype(BF16)
    uq = pad_head(w_uq.reshape(C_Q_RANK, N_HEADS, C_QK)).reshape(C_Q_RANK, -1).astype(BF16)
    return {
        "down": down,
        "down_rope": jnp.concatenate([down, _swap_pair_columns(down[:, C_DOWN_PAD - LANES:])], axis=1),
        "q_lnorm": q_lnorm.reshape(1, C_Q_RANK),
        "kv_lnorm": kv_lnorm.reshape(1, C_KV_RANK),
        "uq": uq,
        "uq_sw": _swap_pair_columns(uq),
        "uk": pad_head(ukv[:, :, :C_NOPE]).reshape(C_KV_RANK, -1).astype(BF16),
        "uv": ukv[:, :, C_NOPE:].reshape(C_KV_RANK, -1).astype(BF16),
        "q_norm": pad_head(q_norm).reshape(1, C_HEAD_PAD),
        "k_norm": pad_head(k_norm).reshape(1, C_HEAD_PAD),
    }


def kernel(x_prompt, x_sample, cache_a_k, cache_a_v, cache_b_k, cache_b_v, cache_c_kv, cache_c_krope, cache_d_k, cache_d_v, c, c_ctx, mod_w, mod_b, norm_ff1, norm_mix, norm_ff2, ff1_w_gu, ff1_w_down, ff2_w_gu, ff2_w_down, a_w_qkv, a_q_norm, a_k_norm, a_sink, a_w_o, b_w_qkv, b_q_norm, b_k_norm, b_w_o, c_w_down, c_q_lnorm, c_kv_lnorm, c_w_uq, c_w_ukv, c_q_norm, c_k_norm, c_w_o, d_w_qkv, d_q_norm, d_k_norm, d_rpb, d_w_o):
    n_p, n_s = BATCH * SEQ, DEC_BATCH * DEC_SEQ
    prompt_tiles, latent_tiles, all_tiles = (0, N_PROMPT_TILES), (N_PROMPT_TILES, N_LATENT_TILES), (0, N_TILES)
    x_parts = [(x_prompt.reshape(n_p, D_MODEL),) + prompt_tiles, (x_sample.reshape(n_s, D_MODEL),) + latent_tiles]
    cond = jnp.concatenate([c_ctx[None], c, jnp.zeros((MOD_ROWS - 1 - DEC_BATCH, D_MODEL), F32)], axis=0)
    modv = _mod_table(cond, mod_w, mod_b)
    s_row = _sample_row(TOKEN_TILE)
    gqa_tabs = _gqa_rope_tables()
    gqa_scale = 1.0
    flat_cache = lambda a: a.reshape(DEC_BATCH * PAST_LEN, -1)
    keys_last = lambda a: jnp.transpose(a, (0, 2, 3, 1))
    keys_first = lambda a: jnp.transpose(a, (0, 3, 1, 2))[:, None]
    ffn_w = [(ff1_w_gu, ff1_w_down), (ff2_w_gu, ff2_w_down)]
    new = {}

    for i in range(DEPTH):
        kind, j = i % N_MIXERS, i // N_MIXERS
        x = _half_ffn(x_parts, modv, i, 0, norm_ff1[i], *ffn_w[0], all_tiles)

        if kind in (0, 1, 3):
            w_qkv, q_norm, k_norm, w_o, n_kv, cache_k, cache_v = {
                0: (a_w_qkv, a_q_norm, a_k_norm, a_w_o, GQA_KV_HEADS, cache_a_k, cache_a_v),
                1: (b_w_qkv, b_q_norm, b_k_norm, b_w_o, GQA_KV_HEADS, cache_b_k, cache_b_v),
                3: (d_w_qkv, d_q_norm, d_k_norm, d_w_o, N_HEADS, cache_d_k, cache_d_v)}[kind]
            w_qkv = w_qkv[j].astype(BF16)
            qp, kp, vp = _gqa_project(x, prompt_tiles, modv, i, norm_mix[i], w_qkv, q_norm[j], k_norm[j], n_kv,
                                      _prompt_row, None, True)
            qs, ks, vs = _gqa_project(x, latent_tiles, modv, i, norm_mix[i], w_qkv, q_norm[j], k_norm[j], n_kv,
                                      s_row, gqa_tabs if kind != 3 else None, False)
            op = _attention(qp, [[(kp, vp, SEQ, "fullT")]], n_batch=BATCH, n_tok=SEQ, tq=SEQ, n_kv=n_kv,
                            kv_step=n_kv, dk=HEAD_DIM, scale=gqa_scale, sink=a_sink[j] if kind == 0 else None)
            name = "abcd"[kind]
            new[name + "_k"], new[name + "_v"] = keys_first(kp), keys_first(vp)
            kc, vc = keys_last(cache_k[:, j]), keys_last(cache_v[:, j])
            if kind == 0:
                tq = 2 * A_WINDOW
                os_ = _attention(qs, [[(ks, vs, A_WINDOW, "prev"), (ks, vs, tq, "band"),
                                       (ks, vs, A_WINDOW, "next")], [(kc, vc, PAST_LEN, "fullT")]],
                                 n_batch=DEC_BATCH, n_tok=DEC_SEQ, tq=tq, n_kv=n_kv, kv_step=n_kv, dk=HEAD_DIM,
                                 scale=gqa_scale, sink=a_sink[j])
            elif kind == 1:
                os_ = _attention(qs, [[(ks, vs, DEC_SEQ, "full")], [(kc, vc, PAST_LEN, "fullT")]],
                                 n_batch=DEC_BATCH, n_tok=DEC_SEQ, tq=256, n_kv=n_kv, kv_step=n_kv, dk=HEAD_DIM,
                                 scale=gqa_scale)
            else:
                os_ = _na_attention(qs, ks, vs, kc, vc, d_rpb[j])
        else:
            w = _mla_weights(c_w_down[j], c_q_lnorm[j], c_kv_lnorm[j], c_w_uq[j], c_w_ukv[j],
                             c_q_norm[j], c_k_norm[j])
            w_o = c_w_o
            qp, kp, vp, ckv_p, kr_p = _mla_project(x, prompt_tiles, modv, i, norm_mix[i], w, _prompt_row, None)
            qs, ks, vs, _, _ = _mla_project(x, latent_tiles, modv, i, norm_mix[i], w, s_row, _mla_rope_tables())
            new["c_kv"] = ckv_p.reshape(BATCH, 1, SEQ, C_KV_RANK)
            new["c_krope"] = kr_p[:, C_NOPE:C_QK].reshape(BATCH, 1, SEQ, C_ROPE)
            kr_cache = jnp.pad(flat_cache(cache_c_krope[:, j]), ((0, 0), (C_NOPE, LANES - C_QK)))
            kc, vc = _mla_expand_cache(flat_cache(cache_c_kv[:, j]), kr_cache, w)
            mla_scale = C_QK ** -0.5
            op = _attention(qp, [[(kp, vp, SEQ, "full")]], n_batch=BATCH, n_tok=SEQ, tq=SEQ,
                            n_kv=N_HEADS, kv_step=N_HEADS, dk=C_HEAD_PAD, scale=mla_scale)
            os_ = _attention(qs, [[(ks, vs, DEC_SEQ, "full")], [(kc, vc, PAST_LEN, "full")]],
                             n_batch=DEC_BATCH, n_tok=DEC_SEQ, tq=DEC_SEQ, n_kv=N_HEADS, kv_step=4, dk=C_HEAD_PAD,
                             scale=mla_scale)

        w_o = w_o[j].astype(BF16)
        x_parts = [(x,) + all_tiles]
        attn_p, attn_s = (op,) + prompt_tiles, (os_,) + latent_tiles
        if i + 1 < DEPTH:
            x_parts = [(_half_ffn(x_parts, modv, i, 6, norm_ff2[i], *ffn_w[1], all_tiles, [attn_p, attn_s], w_o),)
                       + all_tiles]
    yp = _half_ffn(x_parts, modv, DEPTH - 1, 6, norm_ff2[-1], *ffn_w[1], prompt_tiles, [attn_p], w_o)
    ys = _half_ffn(x_parts, modv, DEPTH - 1, 6, norm_ff2[-1], *ffn_w[1], latent_tiles, [attn_s], w_o)

    return (yp.reshape(BATCH, SEQ, D_MODEL), ys.reshape(DEC_BATCH, DEC_SEQ, D_MODEL),
            new["a_k"], new["a_v"], new["b_k"], new["b_v"], new["c_kv"], new["c_krope"],
            new["d_k"], new["d_v"])
```
